```python
import jax, jax.numpy as jnp
from jax import lax
import numpy as np

D_MODEL = 1024
BATCH = 32
SEQ = 2048
DEPTH = 4

N_MEM = 256
RMS_EPS = 1e-6
ROPE_THETA = 10000.0
NEG_INF = -1e30
M_INIT = -1e30
ATT_WIDTH = D_MODEL // 2
ATT_HEAD_DIM = 64
ATT_HEADS = ATT_WIDTH // ATT_HEAD_DIM
ATT_KV_HEADS = 2
ATT_GROUP = ATT_HEADS // ATT_KV_HEADS
IDX_HEADS = 8
IDX_DIM = 64
INDEX_TOPK_MAX = 256
DSA_QBLOCK = 64
HG_WIDTH = D_MODEL // 4
HG_HEADS = 4
HG_VDIM = HG_WIDTH // HG_HEADS
HG_KDIM = 64
ML_WIDTH = D_MODEL - ATT_WIDTH - HG_WIDTH
ML_HEADS = 4
ML_V_DIM = ML_WIDTH // ML_HEADS
ML_QK_DIM = ML_V_DIM
ML_CONV = 4
CHUNK = 64
MIX_WIDTH = ATT_WIDTH + HG_WIDTH + ML_WIDTH
CROSS_HEADS = 4
CROSS_HEAD_DIM = D_MODEL // CROSS_HEADS
MLP_HIDDEN = 4 * D_MODEL
IN_SPLITS = (
    ATT_HEADS * ATT_HEAD_DIM,
    ATT_KV_HEADS * ATT_HEAD_DIM,
    ATT_KV_HEADS * ATT_HEAD_DIM,
    IDX_HEADS * IDX_DIM,
    IDX_DIM,
    IDX_HEADS,
    HG_HEADS * HG_KDIM,
    HG_HEADS * HG_KDIM,
    HG_WIDTH,
    HG_WIDTH,
    2 * ML_HEADS * ML_QK_DIM,
    ML_WIDTH,
    ML_WIDTH,
    2 * ML_HEADS,
)
IN_WIDTH = sum(IN_SPLITS)

kernel_name = "hybrid_dsa_hgrn2_mlstm_trunk"


def _rmsnorm(x, g):
    x32 = x.astype(jnp.float32)
    y = x32 * lax.rsqrt(jnp.mean(x32 * x32, axis=-1, keepdims=True) + RMS_EPS)
    return (y * g.astype(jnp.float32)).astype(x.dtype)


def _rope(x, pos):
    d = x.shape[-1]
    inv = ROPE_THETA ** (-jnp.arange(0, d, 2, dtype=jnp.float32) / d)
    ang = pos.astype(jnp.float32)[:, None] * inv[None, :]
    cos = jnp.cos(ang)[:, None, :]
    sin = jnp.sin(ang)[:, None, :]
    x1, x2 = x[..., : d // 2], x[..., d // 2:]
    return jnp.concatenate([x1 * cos - x2 * sin, x2 * cos + x1 * sin], axis=-1)


def _split_cols(a, sizes):
    offs = [int(o) for o in np.cumsum(sizes)[:-1]]
    return jnp.split(a, offs, axis=-1)


def _dsa_attention(q, k, v, qi, ki, wi):
    B, T = q.shape[:2]
    topk = min(INDEX_TOPK_MAX, T // 4)
    nb = T // DSA_QBLOCK
    key_pos = jnp.arange(T)
    gather = jax.vmap(lambda a, i: a[i])

    def blocks(a):
        return jnp.moveaxis(a.reshape((B, nb, DSA_QBLOCK) + a.shape[2:]), 1, 0)

    def one_block(args):
        qb, qib, wib, blk = args
        qpos = blk * DSA_QBLOCK + jnp.arange(DSA_QBLOCK)
        causal = key_pos[None, :] <= qpos[:, None]
        s = jax.nn.relu(jnp.einsum('bqhd,bsd->bqhs', qib, ki))
        score = jnp.einsum('bqhs,bqh->bqs', s, wib)
        score = jnp.where(causal[None], score, -jnp.inf)
        _, idx = lax.top_k(score, topk)
        valid = idx <= qpos[None, :, None]
        k_sel = gather(k, idx)
        v_sel = gather(v, idx)
        qg = qb.reshape(B, DSA_QBLOCK, ATT_KV_HEADS, ATT_GROUP, ATT_HEAD_DIM)
        logits = jnp.einsum('bqkgd,bqnkd->bqkgn', qg, k_sel) * (ATT_HEAD_DIM ** -0.5)
        logits = jnp.where(valid[:, :, None, None, :], logits, NEG_INF)
        p = jax.nn.softmax(logits, axis=-1)
        o = jnp.einsum('bqkgn,bqnkd->bqkgd', p, v_sel)
        return o.reshape(B, DSA_QBLOCK, ATT_HEADS * ATT_HEAD_DIM)

    out = lax.map(one_block, (blocks(q), blocks(qi), blocks(wi), jnp.arange(nb)))
    return jnp.moveaxis(out, 0, 1).reshape(B, T, ATT_HEADS * ATT_HEAD_DIM)


def _hgrn2(q, f, i, lb):
    B, T, H, K = q.shape
    V = i.shape[-1]
    q = jax.nn.silu(q) * (K ** -0.5)
    forget = lb + (1.0 - lb) * jax.nn.sigmoid(f)
    logf = jnp.log(forget)
    kk = 1.0 - forget
    nc = T // CHUNK
    tri = jnp.tril(jnp.ones((CHUNK, CHUNK), dtype=bool))

    def chunks(a):
        return jnp.transpose(a.reshape(B, nc, CHUNK, H, a.shape[-1]), (1, 0, 3, 2, 4))

    def step(S, inp):
        qc, kc, vc, lfc = inp
        G = jnp.cumsum(lfc, axis=-2)
        diff = G[:, :, :, None, :] - G[:, :, None, :, :]
        decay = jnp.exp(jnp.where(tri[:, :, None], diff, -jnp.inf))
        A = jnp.einsum('bhtk,bhsk,bhtsk->bhts', qc, kc, decay)
        o = (jnp.einsum('bhtk,bhkv->bhtv', qc * jnp.exp(G), S)
             + jnp.einsum('bhts,bhsv->bhtv', A, vc))
        G_last = G[:, :, -1:, :]
        S = (jnp.exp(G_last[:, :, 0, :])[..., None] * S
             + jnp.einsum('bhsk,bhsv->bhkv', kc * jnp.exp(G_last - G), vc))
        return S, o

    S0 = jnp.zeros((B, H, K, V), jnp.float32)
    _, o = lax.scan(step, S0, (chunks(q), chunks(kk), chunks(i), chunks(logf)))
    return jnp.transpose(o, (1, 0, 3, 2, 4)).reshape(B, T, H, V)


def _mlstm(q, k, v, ig, fg):
    B, T, H, DK = q.shape
    DV = v.shape[-1]
    k = k * (DK ** -0.5)
    logf = jax.nn.log_sigmoid(fg)
    nc = T // CHUNK
    tri = jnp.tril(jnp.ones((CHUNK, CHUNK), dtype=bool))

    def chunks(a):
        return jnp.transpose(a.reshape(B, nc, CHUNK, H, a.shape[-1]), (1, 0, 3, 2, 4))

    def gchunks(a):
        return jnp.transpose(a.reshape(B, nc, CHUNK, H), (1, 0, 3, 2))

    def step(carry, inp):
        C, n, m = carry
        qc, kc, vc, ic, lfc = inp
        b = jnp.cumsum(lfc, axis=-1)
        D = jnp.where(tri, b[..., :, None] - b[..., None, :] + ic[..., None, :], -jnp.inf)
        inter = b + m[..., None]
        m_t = jnp.maximum(inter, jnp.max(D, axis=-1))
        w = jnp.exp(D - m_t[..., None])
        a = jnp.exp(inter - m_t)
        s = jnp.einsum('bhtd,bhsd->bhts', qc, kc) * w
        num = (a[..., None] * jnp.einsum('bhtd,bhdv->bhtv', qc, C)
               + jnp.einsum('bhts,bhsv->bhtv', s, vc))
        den = a * jnp.einsum('bhtd,bhd->bht', qc, n) + jnp.sum(s, axis=-1)
        h = num / jnp.maximum(jnp.abs(den), jnp.exp(-m_t))[..., None]
        b_last = b[..., -1]
        wl = b_last[..., None] - b + ic
        m_new = jnp.maximum(b_last + m, jnp.max(wl, axis=-1))
        dec = jnp.exp(b_last + m - m_new)
        ws = jnp.exp(wl - m_new[..., None])
        C = dec[..., None, None] * C + jnp.einsum('bhs,bhsd,bhsv->bhdv', ws, kc, vc)
        n = dec[..., None] * n + jnp.einsum('bhs,bhsd->bhd', ws, kc)
        return (C, n, m_new), h

    init = (jnp.zeros((B, H, DK, DV), jnp.float32),
            jnp.zeros((B, H, DK), jnp.float32),
            jnp.full((B, H), M_INIT, jnp.float32))
    _, h = lax.scan(step, init, (chunks(q), chunks(k), chunks(v), gchunks(ig), gchunks(logf)))
    return jnp.transpose(h, (1, 0, 3, 2, 4)).reshape(B, T, H, DV)


def _causal_dwconv(x, w, b):
    kw, ch = w.shape
    y = lax.conv_general_dilated(x, w[:, None, :], window_strides=(1,),
                                 padding=[(kw - 1, 0)],
                                 dimension_numbers=('NWC', 'WIO', 'NWC'),
                                 feature_group_count=ch)
    return y + b


def _cross_attention(h, mem_n, wq, wkv, wo):
    B, T, _ = h.shape
    M = mem_n.shape[1]
    q = (h @ wq).astype(jnp.float32).reshape(B, T, CROSS_HEADS, CROSS_HEAD_DIM)
    k, v = jnp.split((mem_n @ wkv).astype(jnp.float32), 2, axis=-1)
    k = k.reshape(B, M, CROSS_HEADS, CROSS_HEAD_DIM)
    v = v.reshape(B, M, CROSS_HEADS, CROSS_HEAD_DIM)
    logits = jnp.einsum('bthd,bmhd->bhtm', q, k) * (CROSS_HEAD_DIM ** -0.5)
    p = jax.nn.softmax(logits, axis=-1)
    o = jnp.einsum('bhtm,bmhd->bthd', p, v).reshape(B, T, D_MODEL).astype(h.dtype)
    return o @ wo


def setup_inputs(seed: int = 0) -> dict:
    key = jax.random.key(seed)
    ks = jax.random.split(key, 24)
    f32 = jnp.float32

    def w(k, shape, fan_in):
        return jax.random.normal(k, shape, f32) * (fan_in ** -0.5)

    def gain(k, shape):
        return 1.0 + 0.05 * jax.random.normal(k, shape, f32)

    ig_b = 0.1 * jax.random.normal(ks[5], (DEPTH, ML_HEADS), f32)
    fg_b = (jnp.linspace(3.0, 6.0, ML_HEADS, dtype=f32)[None, :]
            + 0.1 * jax.random.normal(ks[6], (DEPTH, ML_HEADS), f32))
    return {
        "x": jax.random.normal(ks[0], (BATCH, SEQ, D_MODEL), f32),
        "mem": jax.random.normal(ks[1], (BATCH, N_MEM, D_MODEL), f32),
        "mix_pre_g": gain(ks[2], (DEPTH, D_MODEL)),
        "w_in": w(ks[3], (DEPTH, D_MODEL, IN_WIDTH), D_MODEL),
        "ml_conv_w": w(ks[4], (DEPTH, ML_CONV, 2 * ML_HEADS * ML_QK_DIM), ML_CONV),
        "ml_conv_b": 0.02 * jax.random.normal(ks[7], (DEPTH, 2 * ML_HEADS * ML_QK_DIM), f32),
        "ml_gate_b": jnp.concatenate([ig_b, fg_b], axis=-1),
        "hg_lb": 0.1 * jax.random.normal(ks[8], (DEPTH, HG_HEADS * HG_KDIM), f32),
        "hg_norm_g": gain(ks[9], (DEPTH, HG_VDIM)),
        "ml_norm_g": gain(ks[10], (DEPTH, ML_V_DIM)),
        "w_out": w(ks[11], (DEPTH, MIX_WIDTH, D_MODEL), MIX_WIDTH),
        "mix_post_g": gain(ks[12], (DEPTH, D_MODEL)),
        "cross_pre_g": gain(ks[13], (DEPTH, D_MODEL)),
        "mem_norm_g": gain(ks[14], (DEPTH, D_MODEL)),
        "w_cq": w(ks[15], (DEPTH, D_MODEL, D_MODEL), D_MODEL),
        "w_ckv": w(ks[16], (DEPTH, D_MODEL, 2 * D_MODEL), D_MODEL),
        "w_co": w(ks[17], (DEPTH, D_MODEL, D_MODEL), D_MODEL),
        "cross_post_g": gain(ks[18], (DEPTH, D_MODEL)),
        "mlp_pre_g": gain(ks[19], (DEPTH, D_MODEL)),
        "w_up": w(ks[20], (DEPTH, D_MODEL, MLP_HIDDEN), D_MODEL),
        "w_down": w(ks[21], (DEPTH, MLP_HIDDEN, D_MODEL), MLP_HIDDEN),
        "mlp_post_g": gain(ks[22], (DEPTH, D_MODEL)),
    }


def reference(x, mem, mix_pre_g, w_in, ml_conv_w, ml_conv_b, ml_gate_b, hg_lb,
              hg_norm_g, ml_norm_g, w_out, mix_post_g, cross_pre_g, mem_norm_g,
              w_cq, w_ckv, w_co, cross_post_g, mlp_pre_g, w_up, w_down, mlp_post_g):
    f32 = jnp.float32
    B, T, _ = x.shape
    pos = jnp.arange(T)
    lbs = jnp.cumsum(jax.nn.softmax(hg_lb.astype(f32), axis=0), axis=0)
    lbs = lbs - lbs[0:1]

    for l in range(DEPTH):
        h = _rmsnorm(x, mix_pre_g[l])
        proj = (h @ w_in[l]).astype(f32)
        (a_q, a_k, a_v, x_q, x_k, x_w, h_q, h_f, h_i, h_g,
         m_qk, m_v, m_o, m_g) = _split_cols(proj, IN_SPLITS)

        q = _rope(a_q.reshape(B, T, ATT_HEADS, ATT_HEAD_DIM), pos)
        k = _rope(a_k.reshape(B, T, ATT_KV_HEADS, ATT_HEAD_DIM), pos)
        v = a_v.reshape(B, T, ATT_KV_HEADS, ATT_HEAD_DIM)
        qi = _rope(x_q.reshape(B, T, IDX_HEADS, IDX_DIM), pos)
        ki = _rope(x_k[:, :, None, :], pos)[:, :, 0, :]
        wi = x_w * ((IDX_HEADS ** -0.5) * (IDX_DIM ** -0.5))
        a_out = _dsa_attention(q, k, v, qi, ki, wi)

        b_o = _hgrn2(h_q.reshape(B, T, HG_HEADS, HG_KDIM),
                     h_f.reshape(B, T, HG_HEADS, HG_KDIM),
                     h_i.reshape(B, T, HG_HEADS, HG_VDIM),
                     lbs[l].reshape(HG_HEADS, HG_KDIM))
        b_out = (_rmsnorm(b_o, hg_norm_g[l])
                 * jax.nn.silu(h_g.reshape(B, T, HG_HEADS, HG_VDIM))).reshape(B, T, HG_WIDTH)

        m_qk = jax.nn.silu(_causal_dwconv(m_qk, ml_conv_w[l].astype(f32), ml_conv_b[l].astype(f32)))
        m_q, m_k = jnp.split(m_qk, 2, axis=-1)
        gates = m_g + ml_gate_b[l].astype(f32)
        c_h = _mlstm(m_q.reshape(B, T, ML_HEADS, ML_QK_DIM),
                     m_k.reshape(B, T, ML_HEADS, ML_QK_DIM),
                     m_v.reshape(B, T, ML_HEADS, ML_V_DIM),
                     gates[..., :ML_HEADS], gates[..., ML_HEADS:])
        c_out = (_rmsnorm(c_h, ml_norm_g[l])
                 * jax.nn.sigmoid(m_o.reshape(B, T, ML_HEADS, ML_V_DIM))).reshape(B, T, ML_WIDTH)

        y = jnp.concatenate([a_out, b_out, c_out], axis=-1).astype(x.dtype) @ w_out[l]
        x = x + _rmsnorm(y, mix_post_g[l])

        h = _rmsnorm(x, cross_pre_g[l])
        mem_n = _rmsnorm(mem, mem_norm_g[l])
        x = x + _rmsnorm(_cross_attention(h, mem_n, w_cq[l], w_ckv[l], w_co[l]), cross_post_g[l])

        h = _rmsnorm(x, mlp_pre_g[l])
        u = jnp.square(jax.nn.relu(h @ w_up[l]))
        x = x + _rmsnorm(u @ w_down[l], mlp_post_g[l])
    return x
```

```python
import functools

import jax
import jax.numpy as jnp
from jax import lax
from jax.experimental import pallas as pl
from jax.experimental.pallas import tpu as pltpu

F32 = jnp.float32
I32 = jnp.int32
MXU_DTYPE = jnp.bfloat16

D_MODEL = 1024
RMS_EPS = 1e-6
ROPE_THETA = 10000.0
NEG_BIG = -1e30
M_INIT = -1e30
HEAD_DIM = 64
ATT_HEADS = 8
ATT_KV_HEADS = 2
ATT_GROUP = ATT_HEADS // ATT_KV_HEADS
IDX_HEADS = 8
INDEX_TOPK_MAX = 256
HG_HEADS = 4
HG_WIDTH = HG_HEADS * HEAD_DIM
ML_HEADS = 4
ML_WIDTH = ML_HEADS * HEAD_DIM
ML_CONV = 4
CROSS_HEADS = 4
CROSS_HEAD_DIM = D_MODEL // CROSS_HEADS
MLP_HIDDEN = 4 * D_MODEL

_O_AQ, _O_AK, _O_AV, _O_XQ, _O_XK, _O_XW = 0, 512, 640, 768, 1280, 1344
_O_HQ, _O_MQK, _O_MV, _O_MO, _O_MG, _IN_WIDTH = 1352, 2376, 2888, 3144, 3400, 3408

NAT_WIDTH = 128 + 128 + 1024 + 1024 + 128
TR_WIDTH = 512 + 512 + 128 + 16

LANE = 128
ROW_TILE = 256
HG_SUB = 16
HG_ROWS = 256
ML_CHUNK = 128
VMEM_LIMIT = 56 * 1024 * 1024

_NT = (((1,), (1,)), ((), ()))


def _mx(a):
    return a.astype(MXU_DTYPE)


def _dot(a, b):
    return jnp.dot(_mx(a), _mx(b), preferred_element_type=F32)


def _dot_nt(a, b):
    return lax.dot_general(_mx(a), _mx(b), _NT, preferred_element_type=F32)


def _split3(x):
    hi = _mx(x)
    r1 = x - hi.astype(F32)
    mid = _mx(r1)
    lo = _mx(r1 - mid.astype(F32))
    return hi, mid, lo


def _dot01_left(m01, x):
    hi, mid, lo = _split3(x)
    f = lambda p: jnp.dot(m01, p, preferred_element_type=F32)
    return f(hi) + f(mid) + f(lo)


def _dot01_right(x, m01):
    hi, mid, lo = _split3(x)
    f = lambda p: jnp.dot(p, m01, preferred_element_type=F32)
    return f(hi) + f(mid) + f(lo)


def _sigmoid(x):
    return 1.0 / (1.0 + jnp.exp(-x))


def _silu(x):
    return x * _sigmoid(x)


def _log_sigmoid(x):
    return jnp.minimum(x, 0.0) - jnp.log(1.0 + jnp.exp(-jnp.abs(x)))


def _iota(shape, dim):
    return lax.broadcasted_iota(I32, shape, dim)


def _block_ones(n, blk):
    same = (_iota((n, n), 0) // blk) == (_iota((n, n), 1) // blk)
    return jnp.where(same, 1.0, 0.0)


def _params(*sem):
    return pltpu.CompilerParams(dimension_semantics=sem, vmem_limit_bytes=VMEM_LIMIT)


def _const_spec(shape):
    nd = len(shape)
    return pl.BlockSpec(shape, lambda *_: (0,) * nd, pipeline_mode=pl.Buffered(1))


def _in_proj_kernel(x_ref, g_ref, wn_ref, wt_ref, cosn_ref, sina_ref, sinb_ref, cost_ref, sint_ref,
                    knat_ref, hg_ref, ml_ref, qT_ref, qiT_ref, vT_ref, wgT_ref, *, idx_scale):
    tm = x_ref.shape[0]
    x = x_ref[...]
    ms = jnp.mean(x * x, axis=-1, keepdims=True)
    h = _mx(x * lax.rsqrt(ms + RMS_EPS) * g_ref[...])
    nat = jnp.dot(h, wn_ref[...], preferred_element_type=F32)
    tr = lax.dot_general(wt_ref[...], h, _NT, preferred_element_type=F32)

    cosn, sina, sinb = cosn_ref[...], sina_ref[...], sinb_ref[...]

    def rope_nat(z):
        return z * cosn + pltpu.roll(z, 96, 1) * sina + pltpu.roll(z, 32, 1) * sinb

    knat_ref[:, 0:128] = rope_nat(nat[:, 0:128]).astype(knat_ref.dtype)
    knat_ref[:, 128:256] = rope_nat(nat[:, 128:256]).astype(knat_ref.dtype)
    hg_ref[...] = nat[:, 256:1280]
    ml_ref[...] = nat[:, 1280:NAT_WIDTH]

    cost, sint = cost_ref[...], sint_ref[...]

    def rope_t(z, scale):
        outs = []
        for hd in range(8):
            x1 = z[hd * 64:hd * 64 + 32]
            x2 = z[hd * 64 + 32:hd * 64 + 64]
            outs.append((x1 * cost - x2 * sint) * scale)
            outs.append((x2 * cost + x1 * sint) * scale)
        return jnp.concatenate(outs, axis=0)

    qT = rope_t(tr[0:512], HEAD_DIM ** -0.5).astype(qT_ref.dtype)
    qiT = rope_t(tr[512:1024], 1.0).astype(qiT_ref.dtype)
    vT = tr[1024:1152].astype(vT_ref.dtype)
    wg = tr[1152:1168]
    wg = jnp.where(_iota(wg.shape, 0) < 8, wg * idx_scale, wg)
    for c in range(tm // LANE):
        sl = slice(c * LANE, (c + 1) * LANE)
        qT_ref[c] = qT[:, sl]
        qiT_ref[c] = qiT[:, sl]
        vT_ref[c] = vT[:, sl]
        wgT_ref[c] = wg[:, sl]


def _in_proj(x2, g, wn, wt, tabs, seq):
    n = x2.shape[0]
    tm = ROW_TILE
    nt_seq = seq // tm
    cosn, sina, sinb, cost, sint = tabs
    idx_scale = (IDX_HEADS ** -0.5) * (HEAD_DIM ** -0.5)
    row = lambda w: pl.BlockSpec((tm, w), lambda i: (i, 0))
    tabn = pl.BlockSpec((tm, LANE), lambda i: (i % nt_seq, 0))
    tabt = pl.BlockSpec((32, tm), lambda i: (0, i % nt_seq))
    t3 = lambda r: pl.BlockSpec((tm // LANE, r, LANE), lambda i: (i, 0, 0))
    nl = n // LANE
    return pl.pallas_call(
        functools.partial(_in_proj_kernel, idx_scale=idx_scale),
        grid=(n // tm,),
        in_specs=[row(D_MODEL), _const_spec((1, D_MODEL)), _const_spec((D_MODEL, NAT_WIDTH)),
                  _const_spec((TR_WIDTH, D_MODEL)), tabn, tabn, tabn, tabt, tabt],
        out_specs=[row(256), row(1024), row(1152), t3(512), t3(512), t3(128), t3(16)],
        out_shape=[jax.ShapeDtypeStruct((n, 256), MXU_DTYPE),
                   jax.ShapeDtypeStruct((n, 1024), F32),
                   jax.ShapeDtypeStruct((n, 1152), F32),
                   jax.ShapeDtypeStruct((nl, 512, LANE), MXU_DTYPE),
                   jax.ShapeDtypeStruct((nl, 512, LANE), MXU_DTYPE),
                   jax.ShapeDtypeStruct((nl, 128, LANE), MXU_DTYPE),
                   jax.ShapeDtypeStruct((nl, 16, LANE), F32)],
        compiler_params=_params("parallel"),
        name="in_proj",
    )(x2, g, wn, wt, cosn, sina, sinb, cost, sint)


def _dsa_kernel(qiT_ref, wgT_ref, knat_ref, qT_ref, vT_ref, o_ref,
                key_scr, bias_scr, m_scr, l_scr, acc_scr, *, topk):
    tq = LANE
    j = pl.program_id(1)
    nk = j + 1
    int_min = jnp.int32(-2 ** 31)
    srow = _iota((tq, tq), 0)
    tcol = _iota((tq, tq), 1)
    causal = srow <= tcol

    w = wgT_ref[0]
    qi = qiT_ref[0]

    def score_tile(kt, carry):
        kblk = knat_ref[kt][:, 128:192]
        acc = jnp.zeros((tq, tq), F32)
        for hd in range(IDX_HEADS):
            y = jnp.dot(kblk, qi[hd * 64:(hd + 1) * 64], preferred_element_type=F32)
            acc = acc + w[hd:hd + 1, :] * jnp.maximum(y, 0.0)
        bits = pltpu.bitcast(acc, I32)
        key_scr[kt] = bits ^ ((bits >> 31) & jnp.int32(0x7FFFFFFF))
        return carry

    lax.fori_loop(0, nk, score_tile, 0)
    key_scr[j] = jnp.where(causal, key_scr[j], int_min)

    def count(pred_fn):
        def body(kt, c):
            ind = jnp.where(pred_fn(key_scr[kt]), 1, 0).astype(I32)
            return c + jnp.sum(ind.reshape(tq // 8, 8, tq), axis=0)
        c = lax.fori_loop(0, nk, body, jnp.zeros((8, tq), I32))
        return jnp.sum(c, axis=0, keepdims=True)

    def search(i, lo):
        cand = lo + (jnp.int32(1) << (31 - i))
        cnt = count(lambda k: k >= cand)
        return jnp.where(cnt >= topk, cand, lo)

    thr = lax.fori_loop(0, 32, search, jnp.full((1, tq), int_min, I32))
    n_gt = count(lambda k: k > thr)
    need = (topk - n_gt).astype(F32)

    tri = _mx(jnp.where(srow >= tcol, 1.0, 0.0))

    def mask_tile(kt, off):
        k = key_scr[kt]
        eq = k == thr
        eqf = jnp.where(eq, 1.0, 0.0)
        rank = jnp.dot(tri, _mx(eqf), preferred_element_type=F32) + off
        tie_bias = jnp.where(rank <= need, 0.0, NEG_BIG)
        bias_scr[kt] = jnp.where(k > thr, 0.0, jnp.where(eq, tie_bias, NEG_BIG))
        return off + jnp.sum(eqf, axis=0, keepdims=True)

    lax.fori_loop(0, nk, mask_tile, jnp.zeros((1, tq), F32))
    bias_scr[j] = jnp.where(causal, bias_scr[j], NEG_BIG)

    m_scr[...] = jnp.full(m_scr.shape, NEG_BIG, F32)
    l_scr[...] = jnp.zeros(l_scr.shape, F32)
    acc_scr[...] = jnp.zeros(acc_scr.shape, F32)
    q = qT_ref[0]
    zeros64 = jnp.zeros((64, tq), q.dtype)
    qpad = []
    for hd in range(ATT_HEADS):
        qh = q[hd * 64:(hd + 1) * 64]
        qpad.append(jnp.concatenate([qh, zeros64] if hd < ATT_GROUP else [zeros64, qh], axis=0))

    def attend(kt, carry):
        kb = knat_ref[kt][:, 0:128]
        vtb = vT_ref[kt]
        bias = bias_scr[kt]
        for hd in range(ATT_HEADS):
            g = hd // ATT_GROUP
            lt = jnp.dot(kb, qpad[hd], preferred_element_type=F32) + bias
            m_prev = m_scr[hd:hd + 1, :]
            m_new = jnp.maximum(m_prev, jnp.max(lt, axis=0, keepdims=True))
            alpha = jnp.exp(m_prev - m_new)
            p = jnp.exp(lt - m_new)
            l_scr[hd:hd + 1, :] = alpha * l_scr[hd:hd + 1, :] + jnp.sum(p, axis=0, keepdims=True)
            pv = jnp.dot(vtb[g * 64:(g + 1) * 64], _mx(p), preferred_element_type=F32)
            acc_scr[hd * 64:(hd + 1) * 64, :] = alpha * acc_scr[hd * 64:(hd + 1) * 64, :] + pv
            m_scr[hd:hd + 1, :] = m_new
        return carry

    lax.fori_loop(0, nk, attend, 0)
    outs = [acc_scr[hd * 64:(hd + 1) * 64, :] / l_scr[hd:hd + 1, :] for hd in range(ATT_HEADS)]
    o_ref[...] = jnp.concatenate(outs, axis=0).T.astype(o_ref.dtype)


def _dsa(qiT3, wgT3, knat3, qT3, vT3, batch, seq):
    tq = LANE
    nq = seq // tq
    n = batch * seq
    topk = min(INDEX_TOPK_MAX, seq // 4)
    qtile = lambda r: pl.BlockSpec((1, r, tq), lambda b, j: (b * nq + j, 0, 0))
    return pl.pallas_call(
        functools.partial(_dsa_kernel, topk=topk),
        grid=(batch, nq),
        in_specs=[qtile(512), qtile(16),
                  pl.BlockSpec((nq, tq, 256), lambda b, j: (b, 0, 0)),
                  qtile(512),
                  pl.BlockSpec((nq, 128, tq), lambda b, j: (b, 0, 0))],
        out_specs=pl.BlockSpec((tq, 512), lambda b, j: (b * nq + j, 0)),
        out_shape=jax.ShapeDtypeStruct((n, 512), MXU_DTYPE),
        scratch_shapes=[pltpu.VMEM((nq, tq, tq), I32), pltpu.VMEM((nq, tq, tq), F32),
                        pltpu.VMEM((8, tq), F32), pltpu.VMEM((8, tq), F32),
                        pltpu.VMEM((512, tq), F32)],
        compiler_params=_params("parallel", "arbitrary"),
        name="dsa",
    )(qiT3, wgT3, knat3, qT3, vT3)


def _hgrn_kernel(hg_ref, lb_ref, gn_ref, o_ref, st_scr, q_scr, k_scr, g_scr, qg_scr, kd_scr, o_scr):
    rows = hg_ref.shape[0]
    c = HG_SUB
    w = HG_WIDTH

    @pl.when(pl.program_id(1) == 0)
    def _():
        st_scr[...] = jnp.zeros(st_scr.shape, F32)

    lb = lb_ref[...]
    q = _silu(hg_ref[:, 0:w]) * (HEAD_DIM ** -0.5)
    forget = lb + (1.0 - lb) * _sigmoid(hg_ref[:, w:2 * w])
    logf = jnp.log(forget)
    kk = 1.0 - forget
    v = hg_ref[:, 2 * w:3 * w]

    ri = _iota((rows, rows), 0)
    ci = _iota((rows, rows), 1)
    same = (ri // c) == (ci // c)
    tri = _mx(jnp.where(same & (ci <= ri), 1.0, 0.0))
    ones_blk = _mx(jnp.where(same, 1.0, 0.0))
    gl = _dot01_left(tri, logf)
    glast = _dot01_left(ones_blk, logf)
    q_scr[...] = q
    k_scr[...] = kk
    g_scr[...] = gl
    qg_scr[...] = q * jnp.exp(gl)
    kd_scr[...] = kk * jnp.exp(glast - gl)
    dec = jnp.exp(glast)
    vT = _mx(v.T)
    head_ones = _mx(_block_ones(w, HEAD_DIM))
    head_mask = _block_ones(w, HEAD_DIM)
    rsub = _iota((c, w), 0)
    rfull = _iota((rows, w), 0)

    for sc in range(rows // c):
        r0 = sc * c
        qs = q_scr[r0:r0 + c, :]
        ks = k_scr[r0:r0 + c, :]
        gs = g_scr[r0:r0 + c, :]
        vs = v[r0:r0 + c, :]
        xs = []
        for s in range(c):
            e = jnp.exp(jnp.where(rsub >= s, gs - gs[s:s + 1, :], -jnp.inf))
            xs.append(qs * ks[s:s + 1, :] * e)
        y = jnp.dot(_mx(jnp.concatenate(xs, axis=0)), head_ones, preferred_element_type=F32)
        o_diag = jnp.zeros((c, w), F32)
        for s in range(c):
            o_diag = o_diag + y[s * c:(s + 1) * c, :] * vs[s:s + 1, :]
        st = st_scr[...]
        o_inter = _dot_nt(qg_scr[r0:r0 + c, :], st)
        o_scr[r0:r0 + c, :] = o_inter + o_diag
        kd = jnp.where((rfull >= r0) & (rfull < r0 + c), kd_scr[...], 0.0)
        upd = jnp.dot(vT, _mx(kd), preferred_element_type=F32)
        st_scr[...] = st * dec[r0:r0 + 1, :] + upd * head_mask

    o = o_scr[...]
    ms = _dot01_right(o * o, head_ones) * (1.0 / HEAD_DIM)
    y = o * lax.rsqrt(ms + RMS_EPS) * gn_ref[...]
    o_ref[...] = (y * _silu(hg_ref[:, 3 * w:4 * w])).astype(o_ref.dtype)


def _hgrn(hg, lb, gn, batch, seq):
    rows = HG_ROWS
    ns = seq // rows
    n = batch * seq
    vm = lambda: pltpu.VMEM((rows, HG_WIDTH), F32)
    return pl.pallas_call(
        _hgrn_kernel,
        grid=(batch, ns),
        in_specs=[pl.BlockSpec((rows, 1024), lambda b, s: (b * ns + s, 0)),
                  _const_spec((1, HG_WIDTH)), _const_spec((1, HG_WIDTH))],
        out_specs=pl.BlockSpec((rows, HG_WIDTH), lambda b, s: (b * ns + s, 0)),
        out_shape=jax.ShapeDtypeStruct((n, HG_WIDTH), MXU_DTYPE),
        scratch_shapes=[pltpu.VMEM((HG_WIDTH, HG_WIDTH), F32), vm(), vm(), vm(), vm(), vm(), vm()],
        compiler_params=_params("parallel", "arbitrary"),
        name="hgrn",
    )(hg, lb, gn)


def _mlstm_kernel(ml_ref, gT_ref, cw_ref, cb_ref, bcol_ref, brow_ref, gn_ref, o_ref,
                  prev_scr, c_scr, n_scr, m64_scr, m128_scr):
    L = ML_CHUNK
    w = ML_WIDTH

    @pl.when(pl.program_id(1) == 0)
    def _():
        prev_scr[...] = jnp.zeros(prev_scr.shape, F32)
        c_scr[...] = jnp.zeros(c_scr.shape, F32)
        n_scr[...] = jnp.zeros(n_scr.shape, F32)
        m64_scr[...] = jnp.full(m64_scr.shape, M_INIT, F32)
        m128_scr[...] = jnp.full(m128_scr.shape, M_INIT, F32)

    xqk = ml_ref[:, 0:512]
    prev = prev_scr[...]
    rows = _iota((L, 512), 0)
    acc = cb_ref[...] + cw_ref[ML_CONV - 1:ML_CONV, :] * xqk
    for sh in range(1, ML_CONV):
        shifted = jnp.where(rows >= sh, pltpu.roll(xqk, sh, 0), pltpu.roll(prev, sh, 0))
        acc = acc + cw_ref[ML_CONV - 1 - sh:ML_CONV - sh, :] * shifted
    prev_scr[...] = xqk
    qk = _silu(acc)
    q = qk[:, 0:w]
    k = qk[:, w:2 * w] * (HEAD_DIM ** -0.5)
    v = ml_ref[:, 512:768]

    gcol = ml_ref[:, 1024:1152] + bcol_ref[...]
    lfcol = _log_sigmoid(gcol)
    srow = _iota((L, L), 0)
    scol = _iota((L, L), 1)
    tril = srow >= scol
    tri = _mx(jnp.where(tril, 1.0, 0.0))
    triu = _mx(jnp.where(srow <= scol, 1.0, 0.0))
    bcol = _dot01_left(tri, lfcol)
    grow = gT_ref[0] + brow_ref[...]
    brow = _dot01_right(_log_sigmoid(grow), triu)

    def expand(width, lane0):
        src = _iota((LANE, ML_HEADS * width), 0)
        dst = _iota((LANE, ML_HEADS * width), 1)
        return _mx(jnp.where(src == lane0 + dst // width, 1.0, 0.0))

    b128 = _dot01_right(bcol, expand(128, ML_HEADS))
    i128 = _dot01_right(gcol, expand(128, 0))
    b64 = _dot01_right(bcol, expand(64, ML_HEADS))
    i64 = _dot01_right(gcol, expand(64, 0))

    lane = _iota((L, LANE), 1)
    first = lane < HEAD_DIM
    pair_ones = _mx(_block_ones(LANE, HEAD_DIM))
    pair_mask = _block_ones(LANE, HEAD_DIM)
    m_prev128 = m128_scr[0:1, :]
    m_prev64 = m64_scr[0:1, :]
    n_row = n_scr[0:1, :]

    houts = []
    for p in range(ML_HEADS // 2):
        ps = slice(p * LANE, (p + 1) * LANE)
        q_pair, k_pair, v_pair = q[:, ps], k[:, ps], v[:, ps]
        per_head = []
        for hh in range(2):
            hd = 2 * p + hh
            hs = slice(hd * 128, (hd + 1) * 128)
            bt = b128[:, hs]
            d = bt - brow[12 + hd:13 + hd, :] + grow[8 + hd:9 + hd, :]
            d = jnp.where(tril, d, -jnp.inf)
            inter = bt + m_prev128[:, hs]
            m_t = jnp.maximum(inter, jnp.max(d, axis=1, keepdims=True))
            wgt = jnp.exp(d - m_t)
            a = jnp.exp(inter - m_t)
            qm = jnp.where(first if hh == 0 else ~first, q_pair, 0.0)
            s = _dot_nt(qm, k_pair) * wgt
            sv = _dot(s, v_pair)
            rs = jnp.sum(s, axis=1, keepdims=True)
            per_head.append((m_t, a, sv, rs))
        sel = lambda i: jnp.where(first, per_head[0][i], per_head[1][i])
        m_t, a, sv, rs = sel(0), sel(1), sel(2), sel(3)
        qc = _dot(q_pair, c_scr[p])
        qn = _dot(q_pair * n_row[:, ps], pair_ones)
        num = a * qc + sv
        den = a * qn + rs
        houts.append(num / jnp.maximum(jnp.abs(den), jnp.exp(-m_t)))
    hout = jnp.concatenate(houts, axis=1)

    def new_m(bx, ix, m_prev):
        blast = bx[L - 1:L, :]
        wl = blast - bx + ix
        m_new = jnp.maximum(blast + m_prev, jnp.max(wl, axis=0, keepdims=True))
        return blast, wl, m_new

    blast, wl, m_new64 = new_m(b64, i64, m_prev64)
    dec = jnp.exp(blast + m_prev64 - m_new64)
    kw = k * jnp.exp(wl - m_new64)
    for p in range(ML_HEADS // 2):
        ps = slice(p * LANE, (p + 1) * LANE)
        upd = jnp.dot(_mx(kw[:, ps].T), _mx(v[:, ps]), preferred_element_type=F32)
        c_scr[p] = c_scr[p] * dec[:, ps] + upd * pair_mask
    n_scr[...] = jnp.broadcast_to(dec * n_row + jnp.sum(kw, axis=0, keepdims=True), n_scr.shape)
    m64_scr[...] = jnp.broadcast_to(m_new64, m64_scr.shape)
    _, _, m_new128 = new_m(b128, i128, m_prev128)
    m128_scr[...] = jnp.broadcast_to(m_new128, m128_scr.shape)

    head_ones = _mx(_block_ones(w, HEAD_DIM))
    ms = _dot01_right(hout * hout, head_ones) * (1.0 / HEAD_DIM)
    y = hout * lax.rsqrt(ms + RMS_EPS) * gn_ref[...]
    o_ref[...] = (y * _sigmoid(ml_ref[:, 768:1024])).astype(o_ref.dtype)


def _mlstm(ml, wgT3, cw, cb, bcol, brow, gn, batch, seq):
    L = ML_CHUNK
    nc = seq // L
    n = batch * seq
    return pl.pallas_call(
        _mlstm_kernel,
        grid=(batch, nc),
        in_specs=[pl.BlockSpec((L, 1152), lambda b, s: (b * nc + s, 0)),
                  pl.BlockSpec((1, 16, L), lambda b, s: (b * nc + s, 0, 0)),
                  _const_spec((ML_CONV, 512)), _const_spec((1, 512)),
                  _const_spec((1, LANE)), _const_spec((16, L)), _const_spec((1, ML_WIDTH))],
        out_specs=pl.BlockSpec((L, ML_WIDTH), lambda b, s: (b * nc + s, 0)),
        out_shape=jax.ShapeDtypeStruct((n, ML_WIDTH), MXU_DTYPE),
        scratch_shapes=[pltpu.VMEM((L, 512), F32), pltpu.VMEM((2, LANE, LANE), F32),
                        pltpu.VMEM((8, ML_WIDTH), F32), pltpu.VMEM((8, ML_WIDTH), F32),
                        pltpu.VMEM((8, 512), F32)],
        compiler_params=_params("parallel", "arbitrary"),
        name="mlstm",
    )(ml, wgT3, cw, cb, bcol, brow, gn)


def _rms(y, g):
    ms = jnp.mean(y * y, axis=-1, keepdims=True)
    return y * lax.rsqrt(ms + RMS_EPS) * g


def _out_proj_kernel(a_ref, b_ref, c_ref, w_ref, x_ref, g_ref, o_ref):
    y = (jnp.dot(a_ref[...], w_ref[0:512, :], preferred_element_type=F32)
         + jnp.dot(b_ref[...], w_ref[512:768, :], preferred_element_type=F32)
         + jnp.dot(c_ref[...], w_ref[768:1024, :], preferred_element_type=F32))
    o_ref[...] = x_ref[...] + _rms(y, g_ref[...])


def _out_proj(a, b, c, w, x2, g):
    n = x2.shape[0]
    tm = ROW_TILE
    row = lambda wd: pl.BlockSpec((tm, wd), lambda i: (i, 0))
    return pl.pallas_call(
        _out_proj_kernel,
        grid=(n // tm,),
        in_specs=[row(512), row(256), row(256), _const_spec((D_MODEL, D_MODEL)), row(D_MODEL),
                  _const_spec((1, D_MODEL))],
        out_specs=row(D_MODEL),
        out_shape=jax.ShapeDtypeStruct((n, D_MODEL), F32),
        compiler_params=_params("parallel"),
        name="out_proj",
    )(a, b, c, w, x2, g)


def _mem_kv_kernel(m_ref, g_ref, w_ref, k_ref, v_ref):
    h = _mx(_rms(m_ref[...], g_ref[...]))
    kv = jnp.dot(h, w_ref[...], preferred_element_type=F32)
    k_ref[...] = kv[:, 0:D_MODEL].astype(k_ref.dtype)
    v_ref[...] = kv[:, D_MODEL:2 * D_MODEL].astype(v_ref.dtype)


def _mem_kv(mem2, g, w):
    n = mem2.shape[0]
    tm = ROW_TILE
    row = pl.BlockSpec((tm, D_MODEL), lambda i: (i, 0))
    return pl.pallas_call(
        _mem_kv_kernel,
        grid=(n // tm,),
        in_specs=[row, _const_spec((1, D_MODEL)), _const_spec((D_MODEL, 2 * D_MODEL))],
        out_specs=[row, row],
        out_shape=[jax.ShapeDtypeStruct((n, D_MODEL), MXU_DTYPE)] * 2,
        compiler_params=_params("parallel"),
        name="mem_kv",
    )(mem2, g, w)


def _cross_kernel(x_ref, gpre_ref, wq_ref, k_ref, v_ref, wo_ref, gpost_ref, o_ref):
    x = x_ref[...]
    h = _mx(_rms(x, gpre_ref[...]))
    q = jnp.dot(h, wq_ref[...], preferred_element_type=F32) * (CROSS_HEAD_DIM ** -0.5)
    outs = []
    for hd in range(CROSS_HEADS):
        hs = slice(hd * CROSS_HEAD_DIM, (hd + 1) * CROSS_HEAD_DIM)
        logits = _dot_nt(q[:, hs], k_ref[:, hs])
        p = jnp.exp(logits - jnp.max(logits, axis=-1, keepdims=True))
        o = jnp.dot(_mx(p), v_ref[:, hs], preferred_element_type=F32)
        outs.append(o / jnp.sum(p, axis=-1, keepdims=True))
    o = _mx(jnp.concatenate(outs, axis=1))
    y = jnp.dot(o, wo_ref[...], preferred_element_type=F32)
    o_ref[...] = x + _rms(y, gpost_ref[...])


def _cross(x2, gpre, wq, kmem, vmem, wo, gpost, seq, n_mem):
    n = x2.shape[0]
    tm = ROW_TILE
    per_b = seq // tm
    row = pl.BlockSpec((tm, D_MODEL), lambda i: (i, 0))
    memb = pl.BlockSpec((n_mem, D_MODEL), lambda i: (i // per_b, 0))
    return pl.pallas_call(
        _cross_kernel,
        grid=(n // tm,),
        in_specs=[row, _const_spec((1, D_MODEL)), _const_spec((D_MODEL, D_MODEL)), memb, memb,
                  _const_spec((D_MODEL, D_MODEL)), _const_spec((1, D_MODEL))],
        out_specs=row,
        out_shape=jax.ShapeDtypeStruct((n, D_MODEL), F32),
        compiler_params=_params("parallel"),
        name="cross",
    )(x2, gpre, wq, kmem, vmem, wo, gpost)


def _mlp_kernel(x_ref, gpre_ref, wu_ref, wd_ref, gpost_ref, o_ref):
    x = x_ref[...]
    h = _mx(_rms(x, gpre_ref[...]))
    y = jnp.zeros(x.shape, F32)
    for c in range(MLP_HIDDEN // D_MODEL):
        cs = slice(c * D_MODEL, (c + 1) * D_MODEL)
        u = jnp.maximum(jnp.dot(h, wu_ref[:, cs], preferred_element_type=F32), 0.0)
        y = y + jnp.dot(_mx(u * u), wd_ref[cs, :], preferred_element_type=F32)
    o_ref[...] = x + _rms(y, gpost_ref[...])


def _mlp(x2, gpre, wu, wd, gpost):
    n = x2.shape[0]
    tm = ROW_TILE
    row = pl.BlockSpec((tm, D_MODEL), lambda i: (i, 0))
    return pl.pallas_call(
        _mlp_kernel,
        grid=(n // tm,),
        in_specs=[row, _const_spec((1, D_MODEL)), _const_spec((D_MODEL, MLP_HIDDEN)),
                  _const_spec((MLP_HIDDEN, D_MODEL)), _const_spec((1, D_MODEL))],
        out_specs=row,
        out_shape=jax.ShapeDtypeStruct((n, D_MODEL), F32),
        compiler_params=_params("parallel"),
        name="mlp",
    )(x2, gpre, wu, wd, gpost)


def _rope_tables(seq):
    half = HEAD_DIM // 2
    inv = ROPE_THETA ** (-jnp.arange(0, HEAD_DIM, 2, dtype=F32) / HEAD_DIM)
    ang = jnp.arange(seq).astype(F32)[:, None] * inv[None, :]
    cos, sin = jnp.cos(ang), jnp.sin(ang)
    lane = jnp.arange(LANE)
    freq = (lane % HEAD_DIM) % half
    first = (lane % HEAD_DIM) < half
    cosn = cos[:, freq]
    sinn = sin[:, freq]
    sina = jnp.where(first[None, :], -sinn, 0.0)
    sinb = jnp.where(first[None, :], 0.0, sinn)
    return cosn, sina, sinb, cos.T, sin.T


def _layout_w_in(w_in):
    depth = w_in.shape[0]
    z = lambda wd: jnp.zeros((depth, D_MODEL, wd), w_in.dtype)
    col = lambda a, b: w_in[:, :, a:b]
    wn = jnp.concatenate([col(_O_AK, _O_AV), col(_O_XK, _O_XW), z(64),
                          col(_O_HQ, _O_MQK), col(_O_MQK, _O_MG), col(_O_MG, _IN_WIDTH), z(120)], axis=-1)
    wt = jnp.concatenate([col(_O_AQ, _O_AK), col(_O_XQ, _O_XK), col(_O_AV, _O_XQ),
                          col(_O_XW, _O_HQ), col(_O_MG, _IN_WIDTH)], axis=-1)
    return _mx(wn), _mx(jnp.swapaxes(wt, 1, 2))


def kernel(x, mem, mix_pre_g, w_in, ml_conv_w, ml_conv_b, ml_gate_b, hg_lb, hg_norm_g, ml_norm_g,
           w_out, mix_post_g, cross_pre_g, mem_norm_g, w_cq, w_ckv, w_co, cross_post_g, mlp_pre_g,
           w_up, w_down, mlp_post_g):
    batch, seq, _ = x.shape
    n_mem = mem.shape[1]
    depth = w_in.shape[0]
    n = batch * seq
    assert seq % max(ROW_TILE, HG_ROWS, ML_CHUNK) == 0 and (batch * n_mem) % ROW_TILE == 0

    lbs = jnp.cumsum(jax.nn.softmax(hg_lb.astype(F32), axis=0), axis=0)
    lbs = lbs - lbs[0:1]

    tabs = _rope_tables(seq)
    wn_all, wt_all = _layout_w_in(w_in)
    row = lambda a: a.astype(F32)[None, :]
    bcol = jnp.pad(ml_gate_b.astype(F32), ((0, 0), (0, LANE - 2 * ML_HEADS)))
    brow = jnp.pad(ml_gate_b.astype(F32), ((0, 0), (8, 0)))

    x2 = x.reshape(n, D_MODEL)
    mem2 = mem.reshape(batch * n_mem, D_MODEL)
    nl = n // LANE
    for l in range(depth):
        knat, hg, ml, qT3, qiT3, vT3, wgT3 = _in_proj(x2, row(mix_pre_g[l]), wn_all[l], wt_all[l], tabs, seq)
        a_out = _dsa(qiT3, wgT3, knat.reshape(nl, LANE, 256), qT3, vT3, batch, seq)
        b_out = _hgrn(hg, row(lbs[l]), row(jnp.tile(hg_norm_g[l], HG_HEADS)), batch, seq)
        c_out = _mlstm(ml, wgT3, ml_conv_w[l].astype(F32), row(ml_conv_b[l]), bcol[l][None, :],
                       jnp.broadcast_to(brow[l][:, None], (16, ML_CHUNK)),
                       row(jnp.tile(ml_norm_g[l], ML_HEADS)), batch, seq)
        x2 = _out_proj(a_out, b_out, c_out, _mx(w_out[l]), x2, row(mix_post_g[l]))
        kmem, vmem = _mem_kv(mem2, row(mem_norm_g[l]), _mx(w_ckv[l]))
        x2 = _cross(x2, row(cross_pre_g[l]), _mx(w_cq[l]), kmem, vmem, _mx(w_co[l]),
                    row(cross_post_g[l]), seq, n_mem)
        x2 = _mlp(x2, row(mlp_pre_g[l]), _mx(w_up[l]), _mx(w_down[l]), row(mlp_post_g[l]))
    return x2.reshape(batch, seq, D_MODEL)
```

```python
import functools

import jax
import jax.numpy as jnp
from jax import lax
from jax.experimental import pallas as pl
from jax.experimental.pallas import tpu as pltpu

F32 = jnp.float32
I32 = jnp.int32
MXU_DTYPE = jnp.bfloat16

D_MODEL = 1024
RMS_EPS = 1e-6
ROPE_THETA = 10000.0
NEG_BIG = -1e30
M_INIT = -1e30
LOG2E = 1.4426950408889634
HEAD_DIM = 64
ATT_HEADS = 8
ATT_KV_HEADS = 2
ATT_GROUP = ATT_HEADS // ATT_KV_HEADS
IDX_HEADS = 8
INDEX_TOPK_MAX = 256
HG_HEADS = 4
HG_WIDTH = HG_HEADS * HEAD_DIM
ML_HEADS = 4
ML_WIDTH = ML_HEADS * HEAD_DIM
ML_CONV = 4
CROSS_HEADS = 4
CROSS_HEAD_DIM = D_MODEL // CROSS_HEADS
MLP_HIDDEN = 4 * D_MODEL

_O_AQ, _O_AK, _O_AV, _O_XQ, _O_XK, _O_XW = 0, 512, 640, 768, 1280, 1344
_O_HQ, _O_MQK, _O_MV, _O_MO, _O_MG, _IN_WIDTH = 1352, 2376, 2888, 3144, 3400, 3408

NAT_WIDTH = 128 + 128 + 1024 + 1024 + 128
TR_WIDTH = 512 + 512 + 128 + 16

LANE = 128
DSA_TILE = 256
ROW_TILE = 256
HG_SUB = 16
HG_ROWS = 256
ML_CHUNK = 128
VMEM_LIMIT = 56 * 1024 * 1024

_NT = (((1,), (1,)), ((), ()))


def _mx(a):
    return a.astype(MXU_DTYPE)


def _dot(a, b):
    return jnp.dot(_mx(a), _mx(b), preferred_element_type=F32)


def _dot_nt(a, b):
    return lax.dot_general(_mx(a), _mx(b), _NT, preferred_element_type=F32)


def _split3(x):
    hi = _mx(x)
    r1 = x - hi.astype(F32)
    mid = _mx(r1)
    lo = _mx(r1 - mid.astype(F32))
    return hi, mid, lo


def _dot01_left(m01, x):
    hi, mid, lo = _split3(x)
    f = lambda p: jnp.dot(m01, p, preferred_element_type=F32)
    return f(hi) + f(mid) + f(lo)


def _dot01_right(x, m01):
    hi, mid, lo = _split3(x)
    f = lambda p: jnp.dot(p, m01, preferred_element_type=F32)
    return f(hi) + f(mid) + f(lo)


def _sigmoid(x):
    return 1.0 / (1.0 + jnp.exp(-x))


def _silu(x):
    return x * _sigmoid(x)


def _log_sigmoid(x):
    return jnp.minimum(x, 0.0) - jnp.log(1.0 + jnp.exp(-jnp.abs(x)))


def _iota(shape, dim):
    return lax.broadcasted_iota(I32, shape, dim)


def _block_ones(n, blk):
    same = (_iota((n, n), 0) // blk) == (_iota((n, n), 1) // blk)
    return jnp.where(same, 1.0, 0.0)


def _params(*sem):
    return pltpu.CompilerParams(dimension_semantics=sem, vmem_limit_bytes=VMEM_LIMIT)


def _const_spec(shape):
    nd = len(shape)
    return pl.BlockSpec(shape, lambda *_: (0,) * nd, pipeline_mode=pl.Buffered(1))


def _in_proj_kernel(x_ref, g_ref, wn_ref, wt_ref, cosn_ref, sina_ref, sinb_ref, cost_ref, sint_ref,
                    knat_ref, hg_ref, ml_ref, qT_ref, qiT_ref, vT_ref, wgT_ref, *, idx_scale):
    tm = x_ref.shape[0]
    x = x_ref[...]
    ms = jnp.mean(x * x, axis=-1, keepdims=True)
    h = _mx(x * lax.rsqrt(ms + RMS_EPS) * g_ref[...])
    nat = jnp.dot(h, wn_ref[...], preferred_element_type=F32)
    tr = lax.dot_general(wt_ref[...], h, _NT, preferred_element_type=F32)

    cosn, sina, sinb = cosn_ref[...], sina_ref[...], sinb_ref[...]

    def rope_nat(z):
        return z * cosn + pltpu.roll(z, 96, 1) * sina + pltpu.roll(z, 32, 1) * sinb

    knat_ref[:, 0:128] = rope_nat(nat[:, 0:128]).astype(knat_ref.dtype)
    knat_ref[:, 128:256] = rope_nat(nat[:, 128:256]).astype(knat_ref.dtype)
    hg_ref[...] = nat[:, 256:1280]
    ml_ref[...] = nat[:, 1280:NAT_WIDTH]

    cost, sint = cost_ref[...], sint_ref[...]

    def rope_t(z, scale):
        outs = []
        for hd in range(8):
            x1 = z[hd * 64:hd * 64 + 32]
            x2 = z[hd * 64 + 32:hd * 64 + 64]
            outs.append((x1 * cost - x2 * sint) * scale)
            outs.append((x2 * cost + x1 * sint) * scale)
        return jnp.concatenate(outs, axis=0)

    qT = rope_t(tr[0:512], (HEAD_DIM ** -0.5) * LOG2E).astype(qT_ref.dtype)
    qiT = rope_t(tr[512:1024], 1.0).astype(qiT_ref.dtype)
    vT = tr[1024:1152].astype(vT_ref.dtype)
    wg = tr[1152:1168]
    wg = jnp.where(_iota(wg.shape, 0) < 8, wg * idx_scale, wg)
    for c in range(tm // DSA_TILE):
        sl = slice(c * DSA_TILE, (c + 1) * DSA_TILE)
        qT_ref[c] = qT[:, sl]
        qiT_ref[c] = qiT[:, sl]
        vT_ref[c] = vT[:, sl]
        wgT_ref[c] = wg[:, sl]


def _in_proj(x2, g, wn, wt, tabs, seq):
    n = x2.shape[0]
    tm = ROW_TILE
    nt_seq = seq // tm
    cosn, sina, sinb, cost, sint = tabs
    idx_scale = (IDX_HEADS ** -0.5) * (HEAD_DIM ** -0.5)
    row = lambda w: pl.BlockSpec((tm, w), lambda i: (i, 0))
    tabn = pl.BlockSpec((tm, LANE), lambda i: (i % nt_seq, 0))
    tabt = pl.BlockSpec((32, tm), lambda i: (0, i % nt_seq))
    t3 = lambda r: pl.BlockSpec((tm // DSA_TILE, r, DSA_TILE), lambda i: (i, 0, 0))
    nl = n // DSA_TILE
    return pl.pallas_call(
        functools.partial(_in_proj_kernel, idx_scale=idx_scale),
        grid=(n // tm,),
        in_specs=[row(D_MODEL), _const_spec((1, D_MODEL)), _const_spec((D_MODEL, NAT_WIDTH)),
                  _const_spec((TR_WIDTH, D_MODEL)), tabn, tabn, tabn, tabt, tabt],
        out_specs=[row(256), row(1024), row(1152), t3(512), t3(512), t3(128), t3(16)],
        out_shape=[jax.ShapeDtypeStruct((n, 256), MXU_DTYPE),
                   jax.ShapeDtypeStruct((n, 1024), F32),
                   jax.ShapeDtypeStruct((n, 1152), F32),
                   jax.ShapeDtypeStruct((nl, 512, DSA_TILE), MXU_DTYPE),
                   jax.ShapeDtypeStruct((nl, 512, DSA_TILE), MXU_DTYPE),
                   jax.ShapeDtypeStruct((nl, 128, DSA_TILE), MXU_DTYPE),
                   jax.ShapeDtypeStruct((nl, 16, DSA_TILE), F32)],
        compiler_params=_params("parallel"),
        name="in_proj",
    )(x2, g, wn, wt, cosn, sina, sinb, cost, sint)


def _dsa_kernel(qiT_ref, wgT_ref, knat_ref, qT_ref, vT_ref, o_ref,
                key_scr, bias_scr, acc_scr, *, topk):
    tq = DSA_TILE
    j = pl.program_id(1)
    nk = j + 1
    int_min = jnp.int32(-2 ** 31)
    srow = _iota((tq, tq), 0)
    tcol = _iota((tq, tq), 1)
    causal = srow <= tcol

    w = wgT_ref[0]
    qi = qiT_ref[0]

    def score_tile(kt, carry):
        kblk = knat_ref[kt][:, 128:192]
        acc = jnp.zeros((tq, tq), F32)
        for hd in range(IDX_HEADS):
            y = jnp.dot(kblk, qi[hd * 64:(hd + 1) * 64], preferred_element_type=F32)
            acc = acc + w[hd:hd + 1, :] * jnp.maximum(y, 0.0)
        bits = pltpu.bitcast(acc, I32)
        key_scr[kt] = bits ^ ((bits >> 31) & jnp.int32(0x7FFFFFFF))
        return carry

    lax.fori_loop(0, nk, score_tile, 0)
    key_scr[j] = jnp.where(causal, key_scr[j], int_min)

    def count(pred_fn):
        def body(kt, c):
            ind = jnp.where(pred_fn(key_scr[kt]), 1, 0).astype(I32)
            return c + jnp.sum(ind.reshape(tq // 8, 8, tq), axis=0)
        c = lax.fori_loop(0, nk, body, jnp.zeros((8, tq), I32))
        return jnp.sum(c, axis=0, keepdims=True)

    def search(i, lo):
        cand = lo + (jnp.int32(1) << (31 - i))
        cnt = count(lambda k: k >= cand)
        return jnp.where(cnt >= topk, cand, lo)

    thr = lax.fori_loop(0, 32, search, jnp.full((1, tq), int_min, I32))
    n_gt = count(lambda k: k > thr)
    need = (topk - n_gt).astype(F32)

    tri = _mx(jnp.where(srow >= tcol, 1.0, 0.0))

    def mask_tile(kt, off):
        k = key_scr[kt]
        eq = k == thr
        eqf = jnp.where(eq, 1.0, 0.0)
        rank = jnp.dot(tri, _mx(eqf), preferred_element_type=F32) + off
        tie_bias = jnp.where(rank <= need, 0.0, NEG_BIG)
        bias_scr[kt] = jnp.where(k > thr, 0.0, jnp.where(eq, tie_bias, NEG_BIG))
        return off + jnp.sum(eqf, axis=0, keepdims=True)

    lax.fori_loop(0, nk, mask_tile, jnp.zeros((1, tq), F32))
    bias_scr[j] = jnp.where(causal, bias_scr[j], NEG_BIG)

    acc_scr[...] = jnp.zeros(acc_scr.shape, F32)
    q = qT_ref[0]
    zeros64 = jnp.zeros((64, tq), q.dtype)
    qwide = []
    for g in range(ATT_KV_HEADS):
        cols = []
        for hh in range(ATT_GROUP):
            hd = g * ATT_GROUP + hh
            qh = q[hd * 64:(hd + 1) * 64]
            cols.append(jnp.concatenate([qh, zeros64] if g == 0 else [zeros64, qh], axis=0))
        qwide.append(jnp.concatenate(cols, axis=1))

    def attend(kt, carry):
        ms, ls = carry
        kb = knat_ref[kt][:, 0:128]
        vtb = vT_ref[kt]
        bias = bias_scr[kt]
        lts = [jnp.dot(kb, qwide[g], preferred_element_type=F32) for g in range(ATT_KV_HEADS)]
        new_m, new_l = [], []
        for g in range(ATT_KV_HEADS):
            lt = jnp.concatenate([lts[g][:, hh * tq:(hh + 1) * tq] + bias for hh in range(ATT_GROUP)], axis=1)
            m_new = jnp.maximum(ms[g], jnp.max(lt, axis=0, keepdims=True))
            alpha = jnp.exp2(ms[g] - m_new)
            p = jnp.exp2(lt - m_new)
            new_l.append(alpha * ls[g] + jnp.sum(p, axis=0, keepdims=True))
            pv = jnp.dot(vtb[g * 64:(g + 1) * 64], _mx(p), preferred_element_type=F32)
            acc_scr[g] = alpha * acc_scr[g] + pv
            new_m.append(m_new)
        return tuple(new_m), tuple(new_l)

    wide = ATT_GROUP * tq
    init = (tuple(jnp.full((1, wide), NEG_BIG, F32) for _ in range(ATT_KV_HEADS)),
            tuple(jnp.zeros((1, wide), F32) for _ in range(ATT_KV_HEADS)))
    _, ls = lax.fori_loop(0, nk, attend, init)
    outs = []
    for g in range(ATT_KV_HEADS):
        og = acc_scr[g] / ls[g]
        outs += [og[:, hh * tq:(hh + 1) * tq] for hh in range(ATT_GROUP)]
    o_ref[...] = jnp.concatenate(outs, axis=0).T.astype(o_ref.dtype)


def _dsa(qiT3, wgT3, knat3, qT3, vT3, batch, seq):
    tq = DSA_TILE
    nq = seq // tq
    n = batch * seq
    topk = min(INDEX_TOPK_MAX, seq // 4)
    qtile = lambda r: pl.BlockSpec((1, r, tq), lambda b, j: (b * nq + j, 0, 0))
    return pl.pallas_call(
        functools.partial(_dsa_kernel, topk=topk),
        grid=(batch, nq),
        in_specs=[qtile(512), qtile(16),
                  pl.BlockSpec((nq, tq, 256), lambda b, j: (b, 0, 0)),
                  qtile(512),
                  pl.BlockSpec((nq, 128, tq), lambda b, j: (b, 0, 0))],
        out_specs=pl.BlockSpec((tq, 512), lambda b, j: (b * nq + j, 0)),
        out_shape=jax.ShapeDtypeStruct((n, 512), MXU_DTYPE),
        scratch_shapes=[pltpu.VMEM((nq, tq, tq), I32), pltpu.VMEM((nq, tq, tq), F32),
                        pltpu.VMEM((ATT_KV_HEADS, HEAD_DIM, ATT_GROUP * tq), F32)],
        compiler_params=_params("parallel", "arbitrary"),
        name="dsa",
    )(qiT3, wgT3, knat3, qT3, vT3)


def _hgrn_kernel(hg_ref, lb_ref, gn_ref, o_ref, st_scr, q_scr, k_scr, g_scr, qg_scr, kd_scr, o_scr):
    rows = hg_ref.shape[0]
    c = HG_SUB
    w = HG_WIDTH

    @pl.when(pl.program_id(1) == 0)
    def _():
        st_scr[...] = jnp.zeros(st_scr.shape, F32)

    lb = lb_ref[...]
    q = _silu(hg_ref[:, 0:w]) * (HEAD_DIM ** -0.5)
    forget = lb + (1.0 - lb) * _sigmoid(hg_ref[:, w:2 * w])
    logf = jnp.log(forget)
    kk = 1.0 - forget
    v = hg_ref[:, 2 * w:3 * w]

    ri = _iota((rows, rows), 0)
    ci = _iota((rows, rows), 1)
    same = (ri // c) == (ci // c)
    tri = _mx(jnp.where(same & (ci <= ri), 1.0, 0.0))
    ones_blk = _mx(jnp.where(same, 1.0, 0.0))
    gl = _dot01_left(tri, logf)
    glast = _dot01_left(ones_blk, logf)
    q_scr[...] = q
    k_scr[...] = kk
    g_scr[...] = gl
    qg_scr[...] = q * jnp.exp(gl)
    kd_scr[...] = kk * jnp.exp(glast - gl)
    dec = jnp.exp(glast)
    vT = _mx(v.T)
    head_ones = _mx(_block_ones(w, HEAD_DIM))
    head_mask = _block_ones(w, HEAD_DIM)
    rsub = _iota((c, w), 0)
    rfull = _iota((rows, w), 0)

    for sc in range(rows // c):
        r0 = sc * c
        qs = q_scr[r0:r0 + c, :]
        ks = k_scr[r0:r0 + c, :]
        gs = g_scr[r0:r0 + c, :]
        vs = v[r0:r0 + c, :]
        xs = []
        for s in range(c):
            e = jnp.exp(jnp.where(rsub >= s, gs - gs[s:s + 1, :], -jnp.inf))
            xs.append(qs * ks[s:s + 1, :] * e)
        y = jnp.dot(_mx(jnp.concatenate(xs, axis=0)), head_ones, preferred_element_type=F32)
        o_diag = jnp.zeros((c, w), F32)
        for s in range(c):
            o_diag = o_diag + y[s * c:(s + 1) * c, :] * vs[s:s + 1, :]
        st = st_scr[...]
        o_inter = _dot_nt(qg_scr[r0:r0 + c, :], st)
        o_scr[r0:r0 + c, :] = o_inter + o_diag
        kd = jnp.where((rfull >= r0) & (rfull < r0 + c), kd_scr[...], 0.0)
        upd = jnp.dot(vT, _mx(kd), preferred_element_type=F32)
        st_scr[...] = st * dec[r0:r0 + 1, :] + upd * head_mask

    o = o_scr[...]
    ms = _dot01_right(o * o, head_ones) * (1.0 / HEAD_DIM)
    y = o * lax.rsqrt(ms + RMS_EPS) * gn_ref[...]
    o_ref[...] = (y * _silu(hg_ref[:, 3 * w:4 * w])).astype(o_ref.dtype)


def _hgrn(hg, lb, gn, batch, seq):
    rows = HG_ROWS
    ns = seq // rows
    n = batch * seq
    vm = lambda: pltpu.VMEM((rows, HG_WIDTH), F32)
    return pl.pallas_call(
        _hgrn_kernel,
        grid=(batch, ns),
        in_specs=[pl.BlockSpec((rows, 1024), lambda b, s: (b * ns + s, 0)),
                  _const_spec((1, HG_WIDTH)), _const_spec((1, HG_WIDTH))],
        out_specs=pl.BlockSpec((rows, HG_WIDTH), lambda b, s: (b * ns + s, 0)),
        out_shape=jax.ShapeDtypeStruct((n, HG_WIDTH), MXU_DTYPE),
        scratch_shapes=[pltpu.VMEM((HG_WIDTH, HG_WIDTH), F32), vm(), vm(), vm(), vm(), vm(), vm()],
        compiler_params=_params("parallel", "arbitrary"),
        name="hgrn",
    )(hg, lb, gn)


def _mlstm_kernel(ml_ref, gT_ref, cw_ref, cb_ref, bcol_ref, brow_ref, gn_ref, o_ref,
                  prev_scr, c_scr, n_scr, m64_scr, m128_scr):
    L = ML_CHUNK
    w = ML_WIDTH

    @pl.when(pl.program_id(1) == 0)
    def _():
        prev_scr[...] = jnp.zeros(prev_scr.shape, F32)
        c_scr[...] = jnp.zeros(c_scr.shape, F32)
        n_scr[...] = jnp.zeros(n_scr.shape, F32)
        m64_scr[...] = jnp.full(m64_scr.shape, M_INIT, F32)
        m128_scr[...] = jnp.full(m128_scr.shape, M_INIT, F32)

    xqk = ml_ref[:, 0:512]
    prev = prev_scr[...]
    rows = _iota((L, 512), 0)
    acc = cb_ref[...] + cw_ref[ML_CONV - 1:ML_CONV, :] * xqk
    for sh in range(1, ML_CONV):
        shifted = jnp.where(rows >= sh, pltpu.roll(xqk, sh, 0), pltpu.roll(prev, sh, 0))
        acc = acc + cw_ref[ML_CONV - 1 - sh:ML_CONV - sh, :] * shifted
    prev_scr[...] = xqk
    qk = _silu(acc)
    q = qk[:, 0:w]
    k = qk[:, w:2 * w] * (HEAD_DIM ** -0.5)
    v = ml_ref[:, 512:768]

    gcol = ml_ref[:, 1024:1152] + bcol_ref[...]
    lfcol = _log_sigmoid(gcol)
    srow = _iota((L, L), 0)
    scol = _iota((L, L), 1)
    tril = srow >= scol
    tri = _mx(jnp.where(tril, 1.0, 0.0))
    triu = _mx(jnp.where(srow <= scol, 1.0, 0.0))
    bcol = _dot01_left(tri, lfcol)
    grow = gT_ref[0] + brow_ref[...]
    brow = _dot01_right(_log_sigmoid(grow), triu)

    def expand(width, lane0):
        src = _iota((LANE, ML_HEADS * width), 0)
        dst = _iota((LANE, ML_HEADS * width), 1)
        return _mx(jnp.where(src == lane0 + dst // width, 1.0, 0.0))

    b128 = _dot01_right(bcol, expand(128, ML_HEADS))
    i128 = _dot01_right(gcol, expand(128, 0))
    b64 = _dot01_right(bcol, expand(64, ML_HEADS))
    i64 = _dot01_right(gcol, expand(64, 0))

    lane = _iota((L, LANE), 1)
    first = lane < HEAD_DIM
    pair_ones = _mx(_block_ones(LANE, HEAD_DIM))
    pair_mask = _block_ones(LANE, HEAD_DIM)
    m_prev128 = m128_scr[0:1, :]
    m_prev64 = m64_scr[0:1, :]
    n_row = n_scr[0:1, :]

    houts = []
    for p in range(ML_HEADS // 2):
        ps = slice(p * LANE, (p + 1) * LANE)
        q_pair, k_pair, v_pair = q[:, ps], k[:, ps], v[:, ps]
        per_head = []
        for hh in range(2):
            hd = 2 * p + hh
            hs = slice(hd * 128, (hd + 1) * 128)
            bt = b128[:, hs]
            d = bt - brow[12 + hd:13 + hd, :] + grow[8 + hd:9 + hd, :]
            d = jnp.where(tril, d, -jnp.inf)
            inter = bt + m_prev128[:, hs]
            m_t = jnp.maximum(inter, jnp.max(d, axis=1, keepdims=True))
            wgt = jnp.exp(d - m_t)
            a = jnp.exp(inter - m_t)
            qm = jnp.where(first if hh == 0 else ~first, q_pair, 0.0)
            s = _dot_nt(qm, k_pair) * wgt
            sv = _dot(s, v_pair)
            rs = jnp.sum(s, axis=1, keepdims=True)
            per_head.append((m_t, a, sv, rs))
        sel = lambda i: jnp.where(first, per_head[0][i], per_head[1][i])
        m_t, a, sv, rs = sel(0), sel(1), sel(2), sel(3)
        qc = _dot(q_pair, c_scr[p])
        qn = _dot(q_pair * n_row[:, ps], pair_ones)
        num = a * qc + sv
        den = a * qn + rs
        houts.append(num / jnp.maximum(jnp.abs(den), jnp.exp(-m_t)))
    hout = jnp.concatenate(houts, axis=1)

    def new_m(bx, ix, m_prev):
        blast = bx[L - 1:L, :]
        wl = blast - bx + ix
        m_new = jnp.maximum(blast + m_prev, jnp.max(wl, axis=0, keepdims=True))
        return blast, wl, m_new

    blast, wl, m_new64 = new_m(b64, i64, m_prev64)
    dec = jnp.exp(blast + m_prev64 - m_new64)
    kw = k * jnp.exp(wl - m_new64)
    for p in range(ML_HEADS // 2):
        ps = slice(p * LANE, (p + 1) * LANE)
        upd = jnp.dot(_mx(kw[:, ps].T), _mx(v[:, ps]), preferred_element_type=F32)
        c_scr[p] = c_scr[p] * dec[:, ps] + upd * pair_mask
    n_scr[...] = jnp.broadcast_to(dec * n_row + jnp.sum(kw, axis=0, keepdims=True), n_scr.shape)
    m64_scr[...] = jnp.broadcast_to(m_new64, m64_scr.shape)
    _, _, m_new128 = new_m(b128, i128, m_prev128)
    m128_scr[...] = jnp.broadcast_to(m_new128, m128_scr.shape)

    head_ones = _mx(_block_ones(w, HEAD_DIM))
    ms = _dot01_right(hout * hout, head_ones) * (1.0 / HEAD_DIM)
    y = hout * lax.rsqrt(ms + RMS_EPS) * gn_ref[...]
    o_ref[...] = (y * _sigmoid(ml_ref[:, 768:1024])).astype(o_ref.dtype)


def _mlstm(ml, wgT3, cw, cb, bcol, brow, gn, batch, seq):
    L = ML_CHUNK
    nc = seq // L
    n = batch * seq
    per_tile = DSA_TILE // L
    return pl.pallas_call(
        _mlstm_kernel,
        grid=(batch, nc),
        in_specs=[pl.BlockSpec((L, 1152), lambda b, s: (b * nc + s, 0)),
                  pl.BlockSpec((1, 16, L), lambda b, s: ((b * nc + s) // per_tile, 0, (b * nc + s) % per_tile)),
                  _const_spec((ML_CONV, 512)), _const_spec((1, 512)),
                  _const_spec((1, LANE)), _const_spec((16, L)), _const_spec((1, ML_WIDTH))],
        out_specs=pl.BlockSpec((L, ML_WIDTH), lambda b, s: (b * nc + s, 0)),
        out_shape=jax.ShapeDtypeStruct((n, ML_WIDTH), MXU_DTYPE),
        scratch_shapes=[pltpu.VMEM((L, 512), F32), pltpu.VMEM((2, LANE, LANE), F32),
                        pltpu.VMEM((8, ML_WIDTH), F32), pltpu.VMEM((8, ML_WIDTH), F32),
                        pltpu.VMEM((8, 512), F32)],
        compiler_params=_params("parallel", "arbitrary"),
        name="mlstm",
    )(ml, wgT3, cw, cb, bcol, brow, gn)


def _rms(y, g):
    ms = jnp.mean(y * y, axis=-1, keepdims=True)
    return y * lax.rsqrt(ms + RMS_EPS) * g


def _out_proj_kernel(a_ref, b_ref, c_ref, w_ref, x_ref, g_ref, o_ref):
    y = (jnp.dot(a_ref[...], w_ref[0:512, :], preferred_element_type=F32)
         + jnp.dot(b_ref[...], w_ref[512:768, :], preferred_element_type=F32)
         + jnp.dot(c_ref[...], w_ref[768:1024, :], preferred_element_type=F32))
    o_ref[...] = x_ref[...] + _rms(y, g_ref[...])


def _out_proj(a, b, c, w, x2, g):
    n = x2.shape[0]
    tm = ROW_TILE
    row = lambda wd: pl.BlockSpec((tm, wd), lambda i: (i, 0))
    return pl.pallas_call(
        _out_proj_kernel,
        grid=(n // tm,),
        in_specs=[row(512), row(256), row(256), _const_spec((D_MODEL, D_MODEL)), row(D_MODEL),
                  _const_spec((1, D_MODEL))],
        out_specs=row(D_MODEL),
        out_shape=jax.ShapeDtypeStruct((n, D_MODEL), F32),
        compiler_params=_params("parallel"),
        name="out_proj",
    )(a, b, c, w, x2, g)


def _mem_kv_kernel(m_ref, g_ref, w_ref, k_ref, v_ref):
    h = _mx(_rms(m_ref[...], g_ref[...]))
    kv = jnp.dot(h, w_ref[...], preferred_element_type=F32)
    k_ref[...] = kv[:, 0:D_MODEL].astype(k_ref.dtype)
    v_ref[...] = kv[:, D_MODEL:2 * D_MODEL].astype(v_ref.dtype)


def _mem_kv(mem2, g, w):
    n = mem2.shape[0]
    tm = ROW_TILE
    row = pl.BlockSpec((tm, D_MODEL), lambda i: (i, 0))
    return pl.pallas_call(
        _mem_kv_kernel,
        grid=(n // tm,),
        in_specs=[row, _const_spec((1, D_MODEL)), _const_spec((D_MODEL, 2 * D_MODEL))],
        out_specs=[row, row],
        out_shape=[jax.ShapeDtypeStruct((n, D_MODEL), MXU_DTYPE)] * 2,
        compiler_params=_params("parallel"),
        name="mem_kv",
    )(mem2, g, w)


def _cross_kernel(x_ref, gpre_ref, wq_ref, k_ref, v_ref, wo_ref, gpost_ref, o_ref):
    x = x_ref[...]
    h = _mx(_rms(x, gpre_ref[...]))
    q = jnp.dot(h, wq_ref[...], preferred_element_type=F32) * (CROSS_HEAD_DIM ** -0.5)
    outs = []
    for hd in range(CROSS_HEADS):
        hs = slice(hd * CROSS_HEAD_DIM, (hd + 1) * CROSS_HEAD_DIM)
        logits = _dot_nt(q[:, hs], k_ref[:, hs])
        p = jnp.exp(logits - jnp.max(logits, axis=-1, keepdims=True))
        o = jnp.dot(_mx(p), v_ref[:, hs], preferred_element_type=F32)
        outs.append(o / jnp.sum(p, axis=-1, keepdims=True))
    o = _mx(jnp.concatenate(outs, axis=1))
    y = jnp.dot(o, wo_ref[...], preferred_element_type=F32)
    o_ref[...] = x + _rms(y, gpost_ref[...])


def _cross(x2, gpre, wq, kmem, vmem, wo, gpost, seq, n_mem):
    n = x2.shape[0]
    tm = ROW_TILE
    per_b = seq // tm
    row = pl.BlockSpec((tm, D_MODEL), lambda i: (i, 0))
    memb = pl.BlockSpec((n_mem, D_MODEL), lambda i: (i // per_b, 0))
    return pl.pallas_call(
        _cross_kernel,
        grid=(n // tm,),
        in_specs=[row, _const_spec((1, D_MODEL)), _const_spec((D_MODEL, D_MODEL)), memb, memb,
                  _const_spec((D_MODEL, D_MODEL)), _const_spec((1, D_MODEL))],
        out_specs=row,
        out_shape=jax.ShapeDtypeStruct((n, D_MODEL), F32),
        compiler_params=_params("parallel"),
        name="cross",
    )(x2, gpre, wq, kmem, vmem, wo, gpost)


def _mlp_kernel(x_ref, gpre_ref, wu_ref, wd_ref, gpost_ref, o_ref):
    x = x_ref[...]
    h = _mx(_rms(x, gpre_ref[...]))
    y = jnp.zeros(x.shape, F32)
    for c in range(MLP_HIDDEN // D_MODEL):
        cs = slice(c * D_MODEL, (c + 1) * D_MODEL)
        u = jnp.maximum(jnp.dot(h, wu_ref[:, cs], preferred_element_type=F32), 0.0)
        y = y + jnp.dot(_mx(u * u), wd_ref[cs, :], preferred_element_type=F32)
    o_ref[...] = x + _rms(y, gpost_ref[...])


def _mlp(x2, gpre, wu, wd, gpost):
    n = x2.shape[0]
    tm = ROW_TILE
    row = pl.BlockSpec((tm, D_MODEL), lambda i: (i, 0))
    return pl.pallas_call(
        _mlp_kernel,
        grid=(n // tm,),
        in_specs=[row, _const_spec((1, D_MODEL)), _const_spec((D_MODEL, MLP_HIDDEN)),
                  _const_spec((MLP_HIDDEN, D_MODEL)), _const_spec((1, D_MODEL))],
        out_specs=row,
        out_shape=jax.ShapeDtypeStruct((n, D_MODEL), F32),
        compiler_params=_params("parallel"),
        name="mlp",
    )(x2, gpre, wu, wd, gpost)


def _rope_tables(seq):
    half = HEAD_DIM // 2
    inv = ROPE_THETA ** (-jnp.arange(0, HEAD_DIM, 2, dtype=F32) / HEAD_DIM)
    ang = jnp.arange(seq).astype(F32)[:, None] * inv[None, :]
    cos, sin = jnp.cos(ang), jnp.sin(ang)
    lane = jnp.arange(LANE)
    freq = (lane % HEAD_DIM) % half
    first = (lane % HEAD_DIM) < half
    cosn = cos[:, freq]
    sinn = sin[:, freq]
    sina = jnp.where(first[None, :], -sinn, 0.0)
    sinb = jnp.where(first[None, :], 0.0, sinn)
    return cosn, sina, sinb, cos.T, sin.T


def _layout_w_in(w_in):
    depth = w_in.shape[0]
    z = lambda wd: jnp.zeros((depth, D_MODEL, wd), w_in.dtype)
    col = lambda a, b: w_in[:, :, a:b]
    wn = jnp.concatenate([col(_O_AK, _O_AV), col(_O_XK, _O_XW), z(64),
                          col(_O_HQ, _O_MQK), col(_O_MQK, _O_MG), col(_O_MG, _IN_WIDTH), z(120)], axis=-1)
    wt = jnp.concatenate([col(_O_AQ, _O_AK), col(_O_XQ, _O_XK), col(_O_AV, _O_XQ),
                          col(_O_XW, _O_HQ), col(_O_MG, _IN_WIDTH)], axis=-1)
    return _mx(wn), _mx(jnp.swapaxes(wt, 1, 2))


def kernel(x, mem, mix_pre_g, w_in, ml_conv_w, ml_conv_b, ml_gate_b, hg_lb, hg_norm_g, ml_norm_g,
           w_out, mix_post_g, cross_pre_g, mem_norm_g, w_cq, w_ckv, w_co, cross_post_g, mlp_pre_g,
           w_up, w_down, mlp_post_g):
    batch, seq, _ = x.shape
    n_mem = mem.shape[1]
    depth = w_in.shape[0]
    n = batch * seq
    assert seq % max(ROW_TILE, HG_ROWS, ML_CHUNK, DSA_TILE) == 0 and (batch * n_mem) % ROW_TILE == 0
    assert ROW_TILE % DSA_TILE == 0 and DSA_TILE % ML_CHUNK == 0

    lbs = jnp.cumsum(jax.nn.softmax(hg_lb.astype(F32), axis=0), axis=0)
    lbs = lbs - lbs[0:1]

    tabs = _rope_tables(seq)
    wn_all, wt_all = _layout_w_in(w_in)
    row = lambda a: a.astype(F32)[None, :]
    bcol = jnp.pad(ml_gate_b.astype(F32), ((0, 0), (0, LANE - 2 * ML_HEADS)))
    brow = jnp.pad(ml_gate_b.astype(F32), ((0, 0), (8, 0)))

    x2 = x.reshape(n, D_MODEL)
    mem2 = mem.reshape(batch * n_mem, D_MODEL)
    nl = n // DSA_TILE
    for l in range(depth):
        knat, hg, ml, qT3, qiT3, vT3, wgT3 = _in_proj(x2, row(mix_pre_g[l]), wn_all[l], wt_all[l], tabs, seq)
        a_out = _dsa(qiT3, wgT3, knat.reshape(nl, DSA_TILE, 256), qT3, vT3, batch, seq)
        b_out = _hgrn(hg, row(lbs[l]), row(jnp.tile(hg_norm_g[l], HG_HEADS)), batch, seq)
        c_out = _mlstm(ml, wgT3, ml_conv_w[l].astype(F32), row(ml_conv_b[l]), bcol[l][None, :],
                       jnp.broadcast_to(brow[l][:, None], (16, ML_CHUNK)),
                       row(jnp.tile(ml_norm_g[l], ML_HEADS)), batch, seq)
        x2 = _out_proj(a_out, b_out, c_out, _mx(w_out[l]), x2, row(mix_post_g[l]))
        kmem, vmem = _mem_kv(mem2, row(mem_norm_g[l]), _mx(w_ckv[l]))
        x2 = _cross(x2, row(cross_pre_g[l]), _mx(w_cq[l]), kmem, vmem, _mx(w_co[l]),
                    row(cross_post_g[l]), seq, n_mem)
        x2 = _mlp(x2, row(mlp_pre_g[l]), _mx(w_up[l]), _mx(w_down[l]), row(mlp_post_g[l]))
    return x2.reshape(batch, seq, D_MODEL)
```

```python
import functools

import jax
import jax.numpy as jnp
from jax import lax
from jax.experimental import pallas as pl
from jax.experimental.pallas import tpu as pltpu

F32 = jnp.float32
I32 = jnp.int32
MXU_DTYPE = jnp.bfloat16

D_MODEL = 1024
RMS_EPS = 1e-6
ROPE_THETA = 10000.0
NEG_BIG = -1e30
M_INIT = -1e30
LOG2E = 1.4426950408889634
F32_MIN_NORMAL = 1.1754943508222875e-38
HEAD_DIM = 64
ATT_HEADS = 8
ATT_KV_HEADS = 2
ATT_GROUP = ATT_HEADS // ATT_KV_HEADS
IDX_HEADS = 8
INDEX_TOPK_MAX = 256
HG_HEADS = 4
HG_WIDTH = HG_HEADS * HEAD_DIM
ML_HEADS = 4
ML_WIDTH = ML_HEADS * HEAD_DIM
ML_CONV = 4
CROSS_HEADS = 4
CROSS_HEAD_DIM = D_MODEL // CROSS_HEADS
MLP_HIDDEN = 4 * D_MODEL

_O_AQ, _O_AK, _O_AV, _O_XQ, _O_XK, _O_XW = 0, 512, 640, 768, 1280, 1344
_O_HQ, _O_MQK, _O_MV, _O_MO, _O_MG, _IN_WIDTH = 1352, 2376, 2888, 3144, 3400, 3408

NAT_WIDTH = 128 + 128 + 1024 + 1024 + 128
TR_WIDTH = 512 + 512 + 128 + 16

LANE = 128
DSA_TILE = 256
ROW_TILE = 256
HG_SUB = 16
HG_ROWS = 256
ML_CHUNK = 128
VMEM_LIMIT = 56 * 1024 * 1024

_NT = (((1,), (1,)), ((), ()))


def _mx(a):
    return a.astype(MXU_DTYPE)


def _dot(a, b):
    return jnp.dot(_mx(a), _mx(b), preferred_element_type=F32)


def _dot_nt(a, b):
    return lax.dot_general(_mx(a), _mx(b), _NT, preferred_element_type=F32)


def _split3(x):
    hi = _mx(x)
    r1 = x - hi.astype(F32)
    mid = _mx(r1)
    lo = _mx(r1 - mid.astype(F32))
    return hi, mid, lo


def _dot01_left(m01, x):
    hi, mid, lo = _split3(x)
    f = lambda p: jnp.dot(m01, p, preferred_element_type=F32)
    return f(hi) + f(mid) + f(lo)


def _dot01_right(x, m01):
    hi, mid, lo = _split3(x)
    f = lambda p: jnp.dot(p, m01, preferred_element_type=F32)
    return f(hi) + f(mid) + f(lo)


def _sigmoid(x):
    return 1.0 / (1.0 + jnp.exp(-x))


def _silu(x):
    return x * _sigmoid(x)


def _log_sigmoid(x):
    return jnp.minimum(x, 0.0) - jnp.log(1.0 + jnp.exp(-jnp.abs(x)))


def _iota(shape, dim):
    return lax.broadcasted_iota(I32, shape, dim)


def _block_ones(n, blk):
    same = (_iota((n, n), 0) // blk) == (_iota((n, n), 1) // blk)
    return jnp.where(same, 1.0, 0.0)


def _params(*sem):
    return pltpu.CompilerParams(dimension_semantics=sem, vmem_limit_bytes=VMEM_LIMIT)


def _const_spec(shape):
    nd = len(shape)
    return pl.BlockSpec(shape, lambda *_: (0,) * nd, pipeline_mode=pl.Buffered(1))


def _in_proj_kernel(x_ref, g_ref, wn_ref, wt_ref, cosn_ref, sina_ref, sinb_ref, cost_ref, sint_ref,
                    knat_ref, hg_ref, ml_ref, qT_ref, qiT_ref, vT_ref, wgT_ref, *, idx_scale):
    tm = x_ref.shape[0]
    x = x_ref[...]
    ms = jnp.mean(x * x, axis=-1, keepdims=True)
    h = _mx(x * lax.rsqrt(ms + RMS_EPS) * g_ref[...])
    nat = jnp.dot(h, wn_ref[...], preferred_element_type=F32)
    tr = lax.dot_general(wt_ref[...], h, _NT, preferred_element_type=F32)

    cosn, sina, sinb = cosn_ref[...], sina_ref[...], sinb_ref[...]

    def rope_nat(z):
        return z * cosn + pltpu.roll(z, 96, 1) * sina + pltpu.roll(z, 32, 1) * sinb

    knat_ref[:, 0:128] = rope_nat(nat[:, 0:128]).astype(knat_ref.dtype)
    knat_ref[:, 128:256] = rope_nat(nat[:, 128:256]).astype(knat_ref.dtype)
    hg_ref[...] = nat[:, 256:1280]
    ml_ref[...] = nat[:, 1280:NAT_WIDTH]

    cost, sint = cost_ref[...], sint_ref[...]

    def rope_t(z, scale):
        outs = []
        for hd in range(8):
            x1 = z[hd * 64:hd * 64 + 32]
            x2 = z[hd * 64 + 32:hd * 64 + 64]
            outs.append((x1 * cost - x2 * sint) * scale)
            outs.append((x2 * cost + x1 * sint) * scale)
        return jnp.concatenate(outs, axis=0)

    qT = rope_t(tr[0:512], (HEAD_DIM ** -0.5) * LOG2E).astype(qT_ref.dtype)
    qiT = rope_t(tr[512:1024], 1.0).astype(qiT_ref.dtype)
    vT = tr[1024:1152].astype(vT_ref.dtype)
    wg = tr[1152:1168]
    wg = jnp.where(_iota(wg.shape, 0) < 8, wg * idx_scale, wg)
    for c in range(tm // DSA_TILE):
        sl = slice(c * DSA_TILE, (c + 1) * DSA_TILE)
        qT_ref[c] = qT[:, sl]
        qiT_ref[c] = qiT[:, sl]
        vT_ref[c] = vT[:, sl]
        wgT_ref[c] = wg[:, sl]


def _in_proj(x2, g, wn, wt, tabs, seq):
    n = x2.shape[0]
    tm = ROW_TILE
    nt_seq = seq // tm
    cosn, sina, sinb, cost, sint = tabs
    idx_scale = (IDX_HEADS ** -0.5) * (HEAD_DIM ** -0.5)
    row = lambda w: pl.BlockSpec((tm, w), lambda i: (i, 0))
    tabn = pl.BlockSpec((tm, LANE), lambda i: (i % nt_seq, 0))
    tabt = pl.BlockSpec((32, tm), lambda i: (0, i % nt_seq))
    t3 = lambda r: pl.BlockSpec((tm // DSA_TILE, r, DSA_TILE), lambda i: (i, 0, 0))
    nl = n // DSA_TILE
    return pl.pallas_call(
        functools.partial(_in_proj_kernel, idx_scale=idx_scale),
        grid=(n // tm,),
        in_specs=[row(D_MODEL), _const_spec((1, D_MODEL)), _const_spec((D_MODEL, NAT_WIDTH)),
                  _const_spec((TR_WIDTH, D_MODEL)), tabn, tabn, tabn, tabt, tabt],
        out_specs=[row(256), row(1024), row(1152), t3(512), t3(512), t3(128), t3(16)],
        out_shape=[jax.ShapeDtypeStruct((n, 256), MXU_DTYPE),
                   jax.ShapeDtypeStruct((n, 1024), F32),
                   jax.ShapeDtypeStruct((n, 1152), F32),
                   jax.ShapeDtypeStruct((nl, 512, DSA_TILE), MXU_DTYPE),
                   jax.ShapeDtypeStruct((nl, 512, DSA_TILE), MXU_DTYPE),
                   jax.ShapeDtypeStruct((nl, 128, DSA_TILE), MXU_DTYPE),
                   jax.ShapeDtypeStruct((nl, 16, DSA_TILE), F32)],
        compiler_params=_params("parallel"),
        name="in_proj",
    )(x2, g, wn, wt, cosn, sina, sinb, cost, sint)


def _dsa_kernel(qiT_ref, wgT_ref, knat_ref, qT_ref, vT_ref, o_ref,
                key_scr, hi_scr, bias_scr, acc_scr, thr_scr, need_scr, *, topk):
    tq = DSA_TILE
    j = pl.program_id(1)
    nk = j + 1
    int_min = jnp.int32(-2 ** 31)
    srow = _iota((tq, tq), 0)
    tcol = _iota((tq, tq), 1)
    causal = srow <= tcol

    w = wgT_ref[0]
    qi = qiT_ref[0]

    def score_tile(kt, carry):
        kblk = knat_ref[kt][:, 128:192]
        acc = jnp.zeros((tq, tq), F32)
        for hd in range(IDX_HEADS):
            y = jnp.dot(kblk, qi[hd * 64:(hd + 1) * 64], preferred_element_type=F32)
            acc = acc + w[hd:hd + 1, :] * jnp.maximum(y, 0.0)
        acc = jnp.where(jnp.abs(acc) < F32_MIN_NORMAL, 0.0, acc)
        acc = jnp.where(jnp.logical_or(kt < j, causal), acc, -jnp.inf)
        bits = pltpu.bitcast(acc, I32)
        key_scr[kt] = bits ^ ((bits >> 31) & jnp.int32(0x7FFFFFFF))
        hi_scr[kt] = pltpu.bitcast(bits & jnp.int32(-65536), F32).astype(jnp.bfloat16)
        return carry

    lax.fori_loop(0, nk, score_tile, 0)

    key16_neg_inf = (0xFF80 ^ 0x7FFF) - 65536
    one_h = jnp.ones((16, tq), jnp.bfloat16)
    zero_h = jnp.zeros((16, tq), jnp.bfloat16)

    def select_threshold(nks):
        def count16(cand16):
            b16 = cand16 ^ ((cand16 >> 15) & 0x7FFF)
            b16 = jnp.where((cand16 > 0) & (cand16 < 0x80), 0x80, b16)
            cb =jnp.broadcast_to(pltpu.bitcast(b16 << 16, F32).astype(jnp.bfloat16), (16, tq))
            accs = [zero_h] * 4
            n = 0
            for kt in range(nks):
                for r in range(tq // 16):
                    ind = jnp.where(hi_scr[kt, r * 16:(r + 1) * 16, :] >= cb, one_h, zero_h)
                    accs[n % 4] = accs[n % 4] + ind
                    n += 1
            tot = (accs[0] + accs[1]).astype(F32) + (accs[2] + accs[3]).astype(F32)
            return jnp.sum(tot, axis=0, keepdims=True)

        def count32(pred_fn):
            accs = [jnp.zeros((8, tq), I32)] * 4
            n = 0
            for kt in range(nks):
                for r in range(tq // 8):
                    ind = jnp.where(pred_fn(key_scr[kt, r * 8:(r + 1) * 8, :]), 1, 0).astype(I32)
                    accs[n % 4] = accs[n % 4] + ind
                    n += 1
            return jnp.sum((accs[0] + accs[1]) + (accs[2] + accs[3]), axis=0, keepdims=True)

        def stage16(i, lo):
            cand = lo + (jnp.int32(1) << (15 - i))
            ok = (count16(cand) >= topk) | (cand <= key16_neg_inf)
            return jnp.where(ok, cand, lo)

        lo16 = lax.fori_loop(0, 16, stage16, jnp.full((1, tq), -32768, I32))

        def stage32(i, lo):
            cand = lo + (jnp.int32(1) << (15 - i))
            return jnp.where(count32(lambda k: k >= cand) >= topk, cand, lo)

        thr = lax.fori_loop(0, 16, stage32, lo16 << 16)
        n_gt = count32(lambda k: k > thr)
        thr_scr[...] = jnp.broadcast_to(thr, thr_scr.shape)
        need_scr[...] = jnp.broadcast_to((topk - n_gt).astype(F32), need_scr.shape)

    for jj in range(key_scr.shape[0]):
        @pl.when(j == jj)
        def _(jj=jj):
            if (jj + 1) * tq <= topk:
                thr_scr[...] = jnp.full(thr_scr.shape, int_min, I32)
                need_scr[...] = jnp.zeros(need_scr.shape, F32)
            else:
                select_threshold(jj + 1)

    thr = thr_scr[0:1, :]
    need = need_scr[0:1, :]

    tri = _mx(jnp.where(srow >= tcol, 1.0, 0.0))

    def mask_tile(kt, off):
        k = key_scr[kt]
        eq = k == thr
        eqf = jnp.where(eq, 1.0, 0.0)
        rank = jnp.dot(tri, _mx(eqf), preferred_element_type=F32) + off
        tie_bias = jnp.where(rank <= need, 0.0, NEG_BIG)
        bias_scr[kt] = jnp.where(k > thr, 0.0, jnp.where(eq, tie_bias, NEG_BIG))
        return off + jnp.sum(eqf, axis=0, keepdims=True)

    lax.fori_loop(0, nk, mask_tile, jnp.zeros((1, tq), F32))
    bias_scr[j] = jnp.where(causal, bias_scr[j], NEG_BIG)

    acc_scr[...] = jnp.zeros(acc_scr.shape, F32)
    q = qT_ref[0]
    zeros64 = jnp.zeros((64, tq), q.dtype)
    qwide = []
    for g in range(ATT_KV_HEADS):
        cols = []
        for hh in range(ATT_GROUP):
            hd = g * ATT_GROUP + hh
            qh = q[hd * 64:(hd + 1) * 64]
            cols.append(jnp.concatenate([qh, zeros64] if g == 0 else [zeros64, qh], axis=0))
        qwide.append(jnp.concatenate(cols, axis=1))

    def attend(kt, carry):
        ms, ls = carry
        kb = knat_ref[kt][:, 0:128]
        vtb = vT_ref[kt]
        bias = bias_scr[kt]
        lts = [jnp.dot(kb, qwide[g], preferred_element_type=F32) for g in range(ATT_KV_HEADS)]
        new_m, new_l = [], []
        for g in range(ATT_KV_HEADS):
            lt = jnp.concatenate([lts[g][:, hh * tq:(hh + 1) * tq] + bias for hh in range(ATT_GROUP)], axis=1)
            m_new = jnp.maximum(ms[g], jnp.max(lt, axis=0, keepdims=True))
            alpha = jnp.exp2(ms[g] - m_new)
            p = jnp.exp2(lt - m_new)
            new_l.append(alpha * ls[g] + jnp.sum(p, axis=0, keepdims=True))
            pv = jnp.dot(vtb[g * 64:(g + 1) * 64], _mx(p), preferred_element_type=F32)
            acc_scr[g] = alpha * acc_scr[g] + pv
            new_m.append(m_new)
        return tuple(new_m), tuple(new_l)

    wide = ATT_GROUP * tq
    init = (tuple(jnp.full((1, wide), NEG_BIG, F32) for _ in range(ATT_KV_HEADS)),
            tuple(jnp.zeros((1, wide), F32) for _ in range(ATT_KV_HEADS)))
    _, ls = lax.fori_loop(0, nk, attend, init)
    outs = []
    for g in range(ATT_KV_HEADS):
        og = acc_scr[g] / ls[g]
        outs += [og[:, hh * tq:(hh + 1) * tq] for hh in range(ATT_GROUP)]
    o_ref[...] = jnp.concatenate(outs, axis=0).T.astype(o_ref.dtype)


def _dsa(qiT3, wgT3, knat3, qT3, vT3, batch, seq):
    tq = DSA_TILE
    nq = seq // tq
    n = batch * seq
    topk = min(INDEX_TOPK_MAX, seq // 4)
    qtile = lambda r: pl.BlockSpec((1, r, tq), lambda b, j: (b * nq + j, 0, 0))
    return pl.pallas_call(
        functools.partial(_dsa_kernel, topk=topk),
        grid=(batch, nq),
        in_specs=[qtile(512), qtile(16),
                  pl.BlockSpec((nq, tq, 256), lambda b, j: (b, 0, 0)),
                  qtile(512),
                  pl.BlockSpec((nq, 128, tq), lambda b, j: (b, 0, 0))],
        out_specs=pl.BlockSpec((tq, 512), lambda b, j: (b * nq + j, 0)),
        out_shape=jax.ShapeDtypeStruct((n, 512), MXU_DTYPE),
        scratch_shapes=[pltpu.VMEM((nq, tq, tq), I32), pltpu.VMEM((nq, tq, tq), jnp.bfloat16),
                        pltpu.VMEM((nq, tq, tq), F32),
                        pltpu.VMEM((ATT_KV_HEADS, HEAD_DIM, ATT_GROUP * tq), F32),
                        pltpu.VMEM((8, tq), I32), pltpu.VMEM((8, tq), F32)],
        compiler_params=_params("parallel", "arbitrary"),
        name="dsa",
    )(qiT3, wgT3, knat3, qT3, vT3)


def _hgrn_kernel(hg_ref, lb_ref, gn_ref, o_ref, st_scr, q_scr, k_scr, g_scr, qg_scr, kd_scr, o_scr, z_scr,
                 snap_scr):
    rows = hg_ref.shape[0]
    c = HG_SUB
    w = HG_WIDTH

    @pl.when(pl.program_id(1) == 0)
    def _():
        st_scr[...] = jnp.zeros(st_scr.shape, F32)
        z_scr[...] = jnp.zeros(z_scr.shape, z_scr.dtype)

    lb = lb_ref[...]
    q = _silu(hg_ref[:, 0:w]) * (HEAD_DIM ** -0.5)
    forget = lb + (1.0 - lb) * _sigmoid(hg_ref[:, w:2 * w])
    logf = jnp.log2(forget)
    kk = 1.0 - forget
    v = hg_ref[:, 2 * w:3 * w]

    ri = _iota((rows, rows), 0)
    ci = _iota((rows, rows), 1)
    same = (ri // c) == (ci // c)
    tri = _mx(jnp.where(same & (ci <= ri), 1.0, 0.0))
    ones_blk = _mx(jnp.where(same, 1.0, 0.0))
    gl = _dot01_left(tri, logf)
    glast = _dot01_left(ones_blk, logf)
    q_scr[...] = q
    k_scr[...] = gl - jnp.log2(kk)
    g_scr[...] = gl
    qg_scr[...] = q * jnp.exp2(gl)
    kd_scr[...] = kk * jnp.exp2(glast - gl)
    dec = jnp.exp2(glast)
    vT = _mx(v.T)
    head_ones = _mx(_block_ones(w, HEAD_DIM))
    pair_mask = _block_ones(LANE, HEAD_DIM)
    half = c // 2
    rhalf = _iota((half, w), 0)
    nsc = rows // c

    for sc in range(nsc):
        z_scr[sc, sc * c:(sc + 1) * c, :] = kd_scr[sc * c:(sc + 1) * c, :].astype(z_scr.dtype)
    for p in range(HG_HEADS // 2):
        ps = slice(p * LANE, (p + 1) * LANE)
        st = st_scr[p]
        for sc in range(nsc):
            snap_scr[sc, p] = st.astype(snap_scr.dtype)
            upd = jnp.dot(vT[ps, :], z_scr[sc, :, ps], preferred_element_type=F32)
            st = st * dec[sc * c:sc * c + 1, ps] + upd * pair_mask
        st_scr[p] = st

    for sc in range(nsc):
        r0 = sc * c
        q_top, q_bot = q_scr[r0:r0 + half, :], q_scr[r0 + half:r0 + c, :]
        g_top, g_bot = g_scr[r0:r0 + half, :], g_scr[r0 + half:r0 + c, :]
        bcast = lambda row: jnp.broadcast_to(row, (half, w))
        xs_top, xs_bot = [], []
        for s in range(c):
            hsr = bcast(k_scr[r0 + s:r0 + s + 1, :])
            if s < half:
                xs_top.append(q_top * jnp.exp2(jnp.where(rhalf >= s, g_top - hsr, -jnp.inf)))
                xs_bot.append(q_bot * jnp.exp2(g_bot - hsr))
            else:
                xs_bot.append(q_bot * jnp.exp2(jnp.where(rhalf >= s - half, g_bot - hsr, -jnp.inf)))
        y = jnp.dot(_mx(jnp.concatenate(xs_top + xs_bot, axis=0)), head_ones, preferred_element_type=F32)
        o_top = jnp.zeros((half, w), F32)
        o_bot = jnp.zeros((half, w), F32)
        for s in range(c):
            vsr = bcast(hg_ref[r0 + s:r0 + s + 1, 2 * w:3 * w])
            if s < half:
                o_top = o_top + y[s * half:(s + 1) * half, :] * vsr
            o_bot = o_bot + y[(half + s) * half:(half + s + 1) * half, :] * vsr
        o_diag = jnp.concatenate([o_top, o_bot], axis=0)
        o_pairs = [_dot_nt(qg_scr[r0:r0 + c, p * LANE:(p + 1) * LANE], snap_scr[sc, p])
                   for p in range(HG_HEADS // 2)]
        o_scr[r0:r0 + c, :] = jnp.concatenate(o_pairs, axis=1) + o_diag

    o = o_scr[...]
    ms = _dot01_right(o * o, head_ones) * (1.0 / HEAD_DIM)
    y = o * lax.rsqrt(ms + RMS_EPS) * gn_ref[...]
    o_ref[...] = (y * _silu(hg_ref[:, 3 * w:4 * w])).astype(o_ref.dtype)


def _hgrn(hg, lb, gn, batch, seq):
    rows = HG_ROWS
    ns = seq // rows
    n = batch * seq
    vm = lambda: pltpu.VMEM((rows, HG_WIDTH), F32)
    return pl.pallas_call(
        _hgrn_kernel,
        grid=(batch, ns),
        in_specs=[pl.BlockSpec((rows, 1024), lambda b, s: (b * ns + s, 0)),
                  _const_spec((1, HG_WIDTH)), _const_spec((1, HG_WIDTH))],
        out_specs=pl.BlockSpec((rows, HG_WIDTH), lambda b, s: (b * ns + s, 0)),
        out_shape=jax.ShapeDtypeStruct((n, HG_WIDTH), MXU_DTYPE),
        scratch_shapes=[pltpu.VMEM((HG_HEADS // 2, LANE, LANE), F32), vm(), vm(), vm(), vm(), vm(), vm(),
                        pltpu.VMEM((rows // HG_SUB, rows, HG_WIDTH), MXU_DTYPE),
                        pltpu.VMEM((rows // HG_SUB, HG_HEADS // 2, LANE, LANE), MXU_DTYPE)],
        compiler_params=_params("parallel", "arbitrary"),
        name="hgrn",
    )(hg, lb, gn)


def _mlstm_kernel(ml_ref, gTa_ref, gTb_ref, cw_ref, cb_ref, bcol_ref, brow_ref, gn_ref, o_ref,
                  prev_scr, c_scr, n_scr, m64_scr, m128_scr):
    @pl.when(pl.program_id(1) == 0)
    def _():
        prev_scr[...] = jnp.zeros(prev_scr.shape, F32)
        c_scr[...] = jnp.zeros(c_scr.shape, F32)
        n_scr[...] = jnp.zeros(n_scr.shape, F32)
        m64_scr[...] = jnp.full(m64_scr.shape, M_INIT, F32)
        m128_scr[...] = jnp.full(m128_scr.shape, M_INIT, F32)

    seqs = [_mlstm_sequence(ml_ref.at[bi], gT_ref, cw_ref, cb_ref, bcol_ref, brow_ref, gn_ref, o_ref.at[bi],
                            prev_scr.at[bi], c_scr.at[bi], n_scr.at[bi], m64_scr.at[bi], m128_scr.at[bi])
            for bi, gT_ref in enumerate((gTa_ref, gTb_ref))]
    while seqs:
        for seq in list(seqs):
            if next(seq, "done") == "done":
                seqs.remove(seq)


def _mlstm_sequence(ml_ref, gT_ref, cw_ref, cb_ref, bcol_ref, brow_ref, gn_ref, o_ref,
                    prev_scr, c_scr, n_scr, m64_scr, m128_scr):
    L = ML_CHUNK
    w = ML_WIDTH

    xqk = ml_ref[:, 0:512]
    prev = prev_scr[...]
    rows = _iota((L, 512), 0)
    acc = cb_ref[...] + cw_ref[ML_CONV - 1:ML_CONV, :] * xqk
    for sh in range(1, ML_CONV):
        shifted = jnp.where(rows >= sh, pltpu.roll(xqk, sh, 0), pltpu.roll(prev, sh, 0))
        acc = acc + cw_ref[ML_CONV - 1 - sh:ML_CONV - sh, :] * shifted
    prev_scr[...] = xqk
    qk = _silu(acc)
    q = qk[:, 0:w]
    k = qk[:, w:2 * w] * (HEAD_DIM ** -0.5)
    v = ml_ref[:, 512:768]

    gcol = ml_ref[:, 1024:1152] + bcol_ref[...]
    lfcol = _log_sigmoid(gcol)
    srow = _iota((L, L), 0)
    scol = _iota((L, L), 1)
    tril = srow >= scol
    tri = _mx(jnp.where(tril, 1.0, 0.0))
    triu = _mx(jnp.where(srow <= scol, 1.0, 0.0))
    bcol = _dot01_left(tri, lfcol)
    grow = gT_ref[0] + brow_ref[...]
    brow = _dot01_right(_log_sigmoid(grow), triu)
    yield

    def expand(width, lane0):
        src = _iota((LANE, ML_HEADS * width), 0)
        dst = _iota((LANE, ML_HEADS * width), 1)
        return _mx(jnp.where(src == lane0 + dst // width, 1.0, 0.0))

    b128 = _dot01_right(bcol, expand(128, ML_HEADS))
    i128 = _dot01_right(gcol, expand(128, 0))
    b64 = _dot01_right(bcol, expand(64, ML_HEADS))
    i64 = _dot01_right(gcol, expand(64, 0))
    yield

    lane = _iota((L, LANE), 1)
    first = lane < HEAD_DIM
    pair_ones = _mx(_block_ones(LANE, HEAD_DIM))
    pair_mask = _block_ones(LANE, HEAD_DIM)
    m_prev128 = m128_scr[0:1, :]
    m_prev64 = m64_scr[0:1, :]
    n_row = n_scr[0:1, :]

    houts = []
    for p in range(ML_HEADS // 2):
        ps = slice(p * LANE, (p + 1) * LANE)
        q_pair, k_pair, v_pair = q[:, ps], k[:, ps], v[:, ps]
        per_head = []
        for hh in range(2):
            hd = 2 * p + hh
            hs = slice(hd * 128, (hd + 1) * 128)
            bt = b128[:, hs]
            d = bt - brow[12 + hd:13 + hd, :] + grow[8 + hd:9 + hd, :]
            d = jnp.where(tril, d, -jnp.inf)
            inter = bt + m_prev128[:, hs]
            m_t = jnp.maximum(inter, jnp.max(d, axis=1, keepdims=True))
            wgt = jnp.exp(d - m_t)
            a = jnp.exp(inter - m_t)
            qm = jnp.where(first if hh == 0 else ~first, q_pair, 0.0)
            s = _dot_nt(qm, k_pair) * wgt
            sv = _dot(s, v_pair)
            rs = jnp.sum(s, axis=1, keepdims=True)
            per_head.append((m_t, a, sv, rs))
            yield
        sel = lambda i: jnp.where(first, per_head[0][i], per_head[1][i])
        m_t, a, sv, rs = sel(0), sel(1), sel(2), sel(3)
        qc = _dot(q_pair, c_scr[p])
        qn = _dot(q_pair * n_row[:, ps], pair_ones)
        num = a * qc + sv
        den = a * qn + rs
        houts.append(num / jnp.maximum(jnp.abs(den), jnp.exp(-m_t)))
        yield
    hout = jnp.concatenate(houts, axis=1)

    def new_m(bx, ix, m_prev):
        blast = bx[L - 1:L, :]
        wl = blast - bx + ix
        m_new = jnp.maximum(blast + m_prev, jnp.max(wl, axis=0, keepdims=True))
        return blast, wl, m_new

    blast, wl, m_new64 = new_m(b64, i64, m_prev64)
    dec = jnp.exp(blast + m_prev64 - m_new64)
    kw = k * jnp.exp(wl - m_new64)
    for p in range(ML_HEADS // 2):
        ps = slice(p * LANE, (p + 1) * LANE)
        upd = jnp.dot(_mx(kw[:, ps].T), _mx(v[:, ps]), preferred_element_type=F32)
        c_scr[p] = c_scr[p] * dec[:, ps] + upd * pair_mask
    n_scr[...] = jnp.broadcast_to(dec * n_row + jnp.sum(kw, axis=0, keepdims=True), n_scr.shape)
    m64_scr[...] = jnp.broadcast_to(m_new64, m64_scr.shape)
    _, _, m_new128 = new_m(b128, i128, m_prev128)
    m128_scr[...] = jnp.broadcast_to(m_new128, m128_scr.shape)

    head_ones = _mx(_block_ones(w, HEAD_DIM))
    ms = _dot01_right(hout * hout, head_ones) * (1.0 / HEAD_DIM)
    y = hout * lax.rsqrt(ms + RMS_EPS) * gn_ref[...]
    o_ref[...] = (y * _sigmoid(ml_ref[:, 768:1024])).astype(o_ref.dtype)


def _mlstm(ml, wgT3, cw, cb, bcol, brow, gn, batch, seq):
    L = ML_CHUNK
    nc = seq // L
    n = batch * seq
    per_tile = DSA_TILE // L
    nb = 2
    assert batch % nb == 0

    def gate_spec(off):
        def index(b, s):
            chunk = (nb * b + off) * nc + s
            return chunk // per_tile, 0, chunk % per_tile
        return pl.BlockSpec((1, 16, L), index)

    out = pl.pallas_call(
        _mlstm_kernel,
        grid=(batch // nb, nc),
        in_specs=[pl.BlockSpec((nb, L, 1152), lambda b, s: (b, s, 0)), gate_spec(0), gate_spec(1),
                  _const_spec((ML_CONV, 512)), _const_spec((1, 512)),
                  _const_spec((1, LANE)), _const_spec((16, L)), _const_spec((1, ML_WIDTH))],
        out_specs=pl.BlockSpec((nb, L, ML_WIDTH), lambda b, s: (b, s, 0)),
        out_shape=jax.ShapeDtypeStruct((batch, seq, ML_WIDTH), MXU_DTYPE),
        scratch_shapes=[pltpu.VMEM((nb, L, 512), F32), pltpu.VMEM((nb, 2, LANE, LANE), F32),
                        pltpu.VMEM((nb, 8, ML_WIDTH), F32), pltpu.VMEM((nb, 8, ML_WIDTH), F32),
                        pltpu.VMEM((nb, 8, 512), F32)],
        compiler_params=_params("parallel", "arbitrary"),
        name="mlstm",
    )(ml.reshape(batch, seq, 1152), wgT3, wgT3, cw, cb, bcol, brow, gn)
    return out.reshape(n, ML_WIDTH)


def _rms(y, g):
    ms = jnp.mean(y * y, axis=-1, keepdims=True)
    return y * lax.rsqrt(ms + RMS_EPS) * g


def _out_proj_kernel(a_ref, b_ref, c_ref, w_ref, x_ref, g_ref, o_ref):
    y = (jnp.dot(a_ref[...], w_ref[0:512, :], preferred_element_type=F32)
         + jnp.dot(b_ref[...], w_ref[512:768, :], preferred_element_type=F32)
         + jnp.dot(c_ref[...], w_ref[768:1024, :], preferred_element_type=F32))
    o_ref[...] = x_ref[...] + _rms(y, g_ref[...])


def _out_proj(a, b, c, w, x2, g):
    n = x2.shape[0]
    tm = ROW_TILE
    row = lambda wd: pl.BlockSpec((tm, wd), lambda i: (i, 0))
    return pl.pallas_call(
        _out_proj_kernel,
        grid=(n // tm,),
        in_specs=[row(512), row(256), row(256), _const_spec((D_MODEL, D_MODEL)), row(D_MODEL),
                  _const_spec((1, D_MODEL))],
        out_specs=row(D_MODEL),
        out_shape=jax.ShapeDtypeStruct((n, D_MODEL), F32),
        compiler_params=_params("parallel"),
        name="out_proj",
    )(a, b, c, w, x2, g)


def _mem_kv_kernel(m_ref, g_ref, w_ref, k_ref, v_ref):
    h = _mx(_rms(m_ref[...], g_ref[...]))
    kv = jnp.dot(h, w_ref[...], preferred_element_type=F32)
    k_ref[...] = kv[:, 0:D_MODEL].astype(k_ref.dtype)
    v_ref[...] = kv[:, D_MODEL:2 * D_MODEL].astype(v_ref.dtype)


def _mem_kv(mem2, g, w):
    n = mem2.shape[0]
    tm = ROW_TILE
    row = pl.BlockSpec((tm, D_MODEL), lambda i: (i, 0))
    return pl.pallas_call(
        _mem_kv_kernel,
        grid=(n // tm,),
        in_specs=[row, _const_spec((1, D_MODEL)), _const_spec((D_MODEL, 2 * D_MODEL))],
        out_specs=[row, row],
        out_shape=[jax.ShapeDtypeStruct((n, D_MODEL), MXU_DTYPE)] * 2,
        compiler_params=_params("parallel"),
        name="mem_kv",
    )(mem2, g, w)


def _cross_kernel(x_ref, gpre_ref, wq_ref, k_ref, v_ref, wo_ref, gpost_ref, o_ref):
    x = x_ref[...]
    h = _mx(_rms(x, gpre_ref[...]))
    q = jnp.dot(h, wq_ref[...], preferred_element_type=F32) * (CROSS_HEAD_DIM ** -0.5)
    outs = []
    for hd in range(CROSS_HEADS):
        hs = slice(hd * CROSS_HEAD_DIM, (hd + 1) * CROSS_HEAD_DIM)
        logits = _dot_nt(q[:, hs], k_ref[:, hs])
        p = jnp.exp(logits - jnp.max(logits, axis=-1, keepdims=True))
        o = jnp.dot(_mx(p), v_ref[:, hs], preferred_element_type=F32)
        outs.append(o / jnp.sum(p, axis=-1, keepdims=True))
    o = _mx(jnp.concatenate(outs, axis=1))
    y = jnp.dot(o, wo_ref[...], preferred_element_type=F32)
    o_ref[...] = x + _rms(y, gpost_ref[...])


def _cross(x2, gpre, wq, kmem, vmem, wo, gpost, seq, n_mem):
    n = x2.shape[0]
    tm = ROW_TILE
    per_b = seq // tm
    row = pl.BlockSpec((tm, D_MODEL), lambda i: (i, 0))
    memb = pl.BlockSpec((n_mem, D_MODEL), lambda i: (i // per_b, 0))
    return pl.pallas_call(
        _cross_kernel,
        grid=(n // tm,),
        in_specs=[row, _const_spec((1, D_MODEL)), _const_spec((D_MODEL, D_MODEL)), memb, memb,
                  _const_spec((D_MODEL, D_MODEL)), _const_spec((1, D_MODEL))],
        out_specs=row,
        out_shape=jax.ShapeDtypeStruct((n, D_MODEL), F32),
        compiler_params=_params("parallel"),
        name="cross",
    )(x2, gpre, wq, kmem, vmem, wo, gpost)


def _mlp_kernel(x_ref, gpre_ref, wu_ref, wd_ref, gpost_ref, o_ref):
    x = x_ref[...]
    h = _mx(_rms(x, gpre_ref[...]))
    y = jnp.zeros(x.shape, F32)
    for c in range(MLP_HIDDEN // D_MODEL):
        cs = slice(c * D_MODEL, (c + 1) * D_MODEL)
        u = jnp.maximum(jnp.dot(h, wu_ref[:, cs], preferred_element_type=F32), 0.0)
        y = y + jnp.dot(_mx(u * u), wd_ref[cs, :], preferred_element_type=F32)
    o_ref[...] = x + _rms(y, gpost_ref[...])


def _mlp(x2, gpre, wu, wd, gpost):
    n = x2.shape[0]
    tm = ROW_TILE
    row = pl.BlockSpec((tm, D_MODEL), lambda i: (i, 0))
    return pl.pallas_call(
        _mlp_kernel,
        grid=(n // tm,),
        in_specs=[row, _const_spec((1, D_MODEL)), _const_spec((D_MODEL, MLP_HIDDEN)),
                  _const_spec((MLP_HIDDEN, D_MODEL)), _const_spec((1, D_MODEL))],
        out_specs=row,
        out_shape=jax.ShapeDtypeStruct((n, D_MODEL), F32),
        compiler_params=_params("parallel"),
        name="mlp",
    )(x2, gpre, wu, wd, gpost)


def _rope_tables(seq):
    half = HEAD_DIM // 2
    inv = ROPE_THETA ** (-jnp.arange(0, HEAD_DIM, 2, dtype=F32) / HEAD_DIM)
    ang = jnp.arange(seq).astype(F32)[:, None] * inv[None, :]
    cos, sin = jnp.cos(ang), jnp.sin(ang)
    lane = jnp.arange(LANE)
    freq = (lane % HEAD_DIM) % half
    first = (lane % HEAD_DIM) < half
    cosn = cos[:, freq]
    sinn = sin[:, freq]
    sina = jnp.where(first[None, :], -sinn, 0.0)
    sinb = jnp.where(first[None, :], 0.0, sinn)
    return cosn, sina, sinb, cos.T, sin.T


def _layout_w_in(w_in):
    depth = w_in.shape[0]
    z = lambda wd: jnp.zeros((depth, D_MODEL, wd), w_in.dtype)
    col = lambda a, b: w_in[:, :, a:b]
    wn = jnp.concatenate([col(_O_AK, _O_AV), col(_O_XK, _O_XW), z(64),
                          col(_O_HQ, _O_MQK), col(_O_MQK, _O_MG), col(_O_MG, _IN_WIDTH), z(120)], axis=-1)
    wt = jnp.concatenate([col(_O_AQ, _O_AK), col(_O_XQ, _O_XK), col(_O_AV, _O_XQ),
                          col(_O_XW, _O_HQ), col(_O_MG, _IN_WIDTH)], axis=-1)
    return _mx(wn), _mx(jnp.swapaxes(wt, 1, 2))


def kernel(x, mem, mix_pre_g, w_in, ml_conv_w, ml_conv_b, ml_gate_b, hg_lb, hg_norm_g, ml_norm_g,
           w_out, mix_post_g, cross_pre_g, mem_norm_g, w_cq, w_ckv, w_co, cross_post_g, mlp_pre_g,
           w_up, w_down, mlp_post_g):
    batch, seq, _ = x.shape
    n_mem = mem.shape[1]
    depth = w_in.shape[0]
    n = batch * seq
    assert seq % max(ROW_TILE, HG_ROWS, ML_CHUNK, DSA_TILE) == 0 and (batch * n_mem) % ROW_TILE == 0
    assert ROW_TILE % DSA_TILE == 0 and DSA_TILE % ML_CHUNK == 0

    lbs = jnp.cumsum(jax.nn.softmax(hg_lb.astype(F32), axis=0), axis=0)
    lbs = lbs - lbs[0:1]

    tabs = _rope_tables(seq)
    wn_all, wt_all = _layout_w_in(w_in)
    row = lambda a: a.astype(F32)[None, :]
    bcol = jnp.pad(ml_gate_b.astype(F32), ((0, 0), (0, LANE - 2 * ML_HEADS)))
    brow = jnp.pad(ml_gate_b.astype(F32), ((0, 0), (8, 0)))

    x2 = x.reshape(n, D_MODEL)
    mem2 = mem.reshape(batch * n_mem, D_MODEL)
    nl = n // DSA_TILE
    for l in range(depth):
        knat, hg, ml, qT3, qiT3, vT3, wgT3 = _in_proj(x2, row(mix_pre_g[l]), wn_all[l], wt_all[l], tabs, seq)
        a_out = _dsa(qiT3, wgT3, knat.reshape(nl, DSA_TILE, 256), qT3, vT3, batch, seq)
        b_out = _hgrn(hg, row(lbs[l]), row(jnp.tile(hg_norm_g[l], HG_HEADS)), batch, seq)
        c_out = _mlstm(ml, wgT3, ml_conv_w[l].astype(F32), row(ml_conv_b[l]), bcol[l][None, :],
                       jnp.broadcast_to(brow[l][:, None], (16, ML_CHUNK)),
                       row(jnp.tile(ml_norm_g[l], ML_HEADS)), batch, seq)
        x2 = _out_proj(a_out, b_out, c_out, _mx(w_out[l]), x2, row(mix_post_g[l]))
        kmem, vmem = _mem_kv(mem2, row(mem_norm_g[l]), _mx(w_ckv[l]))
        x2 = _cross(x2, row(cross_pre_g[l]), _mx(w_cq[l]), kmem, vmem, _mx(w_co[l]),
                    row(cross_post_g[l]), seq, n_mem)
        x2 = _mlp(x2, row(mlp_pre_g[l]), _mx(w_up[l]), _mx(w_down[l]), row(mlp_post_g[l]))
    return x2.reshape(batch, seq, D_MODEL)
```

```python
import functools

import jax
import jax.numpy as jnp
from jax import lax
from jax.experimental import pallas as pl
from jax.experimental.pallas import tpu as pltpu

F32 = jnp.float32
I32 = jnp.int32
MXU_DTYPE = jnp.bfloat16

D_MODEL = 1024
RMS_EPS = 1e-6
ROPE_THETA = 10000.0
NEG_BIG = -1e30
M_INIT = -1e30
LOG2E = 1.4426950408889634
F32_MIN_NORMAL = 1.1754943508222875e-38
HEAD_DIM = 64
ATT_HEADS = 8
ATT_KV_HEADS = 2
ATT_GROUP = ATT_HEADS // ATT_KV_HEADS
IDX_HEADS = 8
INDEX_TOPK_MAX = 256
HG_HEADS = 4
HG_WIDTH = HG_HEADS * HEAD_DIM
ML_HEADS = 4
ML_WIDTH = ML_HEADS * HEAD_DIM
ML_CONV = 4
CROSS_HEADS = 4
CROSS_HEAD_DIM = D_MODEL // CROSS_HEADS
MLP_HIDDEN = 4 * D_MODEL

_O_AQ, _O_AK, _O_AV, _O_XQ, _O_XK, _O_XW = 0, 512, 640, 768, 1280, 1344
_O_HQ, _O_MQK, _O_MV, _O_MO, _O_MG, _IN_WIDTH = 1352, 2376, 2888, 3144, 3400, 3408

NAT_WIDTH = 128 + 128 + 1024 + 1024
TR_WIDTH = 512 + 512 + 128 + 16

LANE = 128
DSA_TILE = 256
ROW_TILE = 256
POST_TILE = 512
HG_SUB = 16
HG_ROWS = 256
ML_CHUNK = 128
ML_SEQS = 4
VMEM_LIMIT = 56 * 1024 * 1024

_NT = (((1,), (1,)), ((), ()))


def _mx(a):
    return a.astype(MXU_DTYPE)


def _dot(a, b):
    return jnp.dot(_mx(a), _mx(b), preferred_element_type=F32)


def _dot_nt(a, b):
    return lax.dot_general(_mx(a), _mx(b), _NT, preferred_element_type=F32)


def _split3(x):
    hi = _mx(x)
    r1 = x - hi.astype(F32)
    mid = _mx(r1)
    lo = _mx(r1 - mid.astype(F32))
    return hi, mid, lo


def _dot01_left(m01, x):
    hi, mid, lo = _split3(x)
    f = lambda p: jnp.dot(m01, p, preferred_element_type=F32)
    return f(hi) + f(mid) + f(lo)


def _dot01_right(x, m01):
    hi, mid, lo = _split3(x)
    f = lambda p: jnp.dot(p, m01, preferred_element_type=F32)
    return f(hi) + f(mid) + f(lo)


def _sigmoid(x):
    return 1.0 / (1.0 + jnp.exp(-x))


def _silu(x):
    return x * _sigmoid(x)


def _log_sigmoid(x):
    return jnp.minimum(x, 0.0) - jnp.log(1.0 + jnp.exp(-jnp.abs(x)))


def _iota(shape, dim):
    return lax.broadcasted_iota(I32, shape, dim)


def _block_ones(n, blk):
    same = (_iota((n, n), 0) // blk) == (_iota((n, n), 1) // blk)
    return jnp.where(same, 1.0, 0.0)


def _params(*sem):
    return pltpu.CompilerParams(dimension_semantics=sem, vmem_limit_bytes=VMEM_LIMIT)


def _const_spec(shape):
    nd = len(shape)
    return pl.BlockSpec(shape, lambda *_: (0,) * nd, pipeline_mode=pl.Buffered(1))


def _in_proj_kernel(x_ref, g_ref, wn_ref, wt_ref, cosn_ref, sina_ref, sinb_ref, cost_ref, sint_ref,
                    knat_ref, hg_ref, ml_ref, qT_ref, qiT_ref, vT_ref, wgT_ref, *, idx_scale):
    tm = x_ref.shape[0]
    x = x_ref[...]
    ms = jnp.mean(x * x, axis=-1, keepdims=True)
    h = _mx(x * lax.rsqrt(ms + RMS_EPS) * g_ref[...])
    nat = jnp.dot(h, wn_ref[...], preferred_element_type=F32)
    tr = lax.dot_general(wt_ref[...], h, _NT, preferred_element_type=F32)

    cosn, sina, sinb = cosn_ref[...], sina_ref[...], sinb_ref[...]

    def rope_nat(z):
        return z * cosn + pltpu.roll(z, 96, 1) * sina + pltpu.roll(z, 32, 1) * sinb

    knat_ref[:, 0:128] = rope_nat(nat[:, 0:128]).astype(knat_ref.dtype)
    knat_ref[:, 128:256] = rope_nat(nat[:, 128:256]).astype(knat_ref.dtype)
    hg_ref[...] = nat[:, 256:1280]
    ml_ref[...] = nat[:, 1280:NAT_WIDTH]

    cost, sint = cost_ref[...], sint_ref[...]

    def rope_t(z, scale):
        outs = []
        for hd in range(8):
            x1 = z[hd * 64:hd * 64 + 32]
            x2 = z[hd * 64 + 32:hd * 64 + 64]
            outs.append((x1 * cost - x2 * sint) * scale)
            outs.append((x2 * cost + x1 * sint) * scale)
        return jnp.concatenate(outs, axis=0)

    qT = rope_t(tr[0:512], (HEAD_DIM ** -0.5) * LOG2E).astype(qT_ref.dtype)
    qiT = rope_t(tr[512:1024], 1.0).astype(qiT_ref.dtype)
    vT = tr[1024:1152].astype(vT_ref.dtype)
    wg = tr[1152:1168]
    wg = jnp.where(_iota(wg.shape, 0) < 8, wg * idx_scale, wg)
    for c in range(tm // DSA_TILE):
        sl = slice(c * DSA_TILE, (c + 1) * DSA_TILE)
        qT_ref[c] = qT[:, sl]
        qiT_ref[c] = qiT[:, sl]
        vT_ref[c] = vT[:, sl]
        wgT_ref[c] = wg[:, sl]


def _in_proj(x2, g, wn, wt, tabs, seq):
    n = x2.shape[0]
    tm = ROW_TILE
    nt_seq = seq // tm
    cosn, sina, sinb, cost, sint = tabs
    idx_scale = (IDX_HEADS ** -0.5) * (HEAD_DIM ** -0.5)
    row = lambda w: pl.BlockSpec((tm, w), lambda i: (i, 0))
    tabn = pl.BlockSpec((tm, LANE), lambda i: (i % nt_seq, 0))
    tabt = pl.BlockSpec((32, tm), lambda i: (0, i % nt_seq))
    t3 = lambda r: pl.BlockSpec((tm // DSA_TILE, r, DSA_TILE), lambda i: (i, 0, 0))
    nl = n // DSA_TILE
    return pl.pallas_call(
        functools.partial(_in_proj_kernel, idx_scale=idx_scale),
        grid=(n // tm,),
        in_specs=[row(D_MODEL), _const_spec((1, D_MODEL)), _const_spec((D_MODEL, NAT_WIDTH)),
                  _const_spec((TR_WIDTH, D_MODEL)), tabn, tabn, tabn, tabt, tabt],
        out_specs=[row(256), row(1024), row(1024), t3(512), t3(512), t3(128), t3(16)],
        out_shape=[jax.ShapeDtypeStruct((n, 256), MXU_DTYPE),
                   jax.ShapeDtypeStruct((n, 1024), F32),
                   jax.ShapeDtypeStruct((n, 1024), F32),
                   jax.ShapeDtypeStruct((nl, 512, DSA_TILE), MXU_DTYPE),
                   jax.ShapeDtypeStruct((nl, 512, DSA_TILE), MXU_DTYPE),
                   jax.ShapeDtypeStruct((nl, 128, DSA_TILE), MXU_DTYPE),
                   jax.ShapeDtypeStruct((nl, 16, DSA_TILE), F32)],
        compiler_params=_params("parallel"),
        name="in_proj",
    )(x2, g, wn, wt, cosn, sina, sinb, cost, sint)


def _dsa_kernel(qiT_ref, wgT_ref, knat_ref, qT_ref, vT_ref, o_ref,
                key_scr, hi_scr, bias_scr, acc_scr, thr_scr, need_scr, *, topk):
    tq = DSA_TILE
    j = pl.program_id(1)
    nk = j + 1
    int_min = jnp.int32(-2 ** 31)
    srow = _iota((tq, tq), 0)
    tcol = _iota((tq, tq), 1)
    causal = srow <= tcol

    w = wgT_ref[0]
    qi = qiT_ref[0]
    qi_wide = jnp.concatenate([qi[hd * 64:(hd + 1) * 64] for hd in range(IDX_HEADS)], axis=1)

    def score_tile(kt, carry):
        kblk = knat_ref[kt][:, 128:192]
        acc = jnp.zeros((tq, tq), F32)
        ys = jnp.dot(kblk, qi_wide, preferred_element_type=F32)
        for hd in range(IDX_HEADS):
            acc = acc + w[hd:hd + 1, :] * jnp.maximum(ys[:, hd * tq:(hd + 1) * tq], 0.0)
        acc = jnp.where(jnp.abs(acc) < F32_MIN_NORMAL, 0.0, acc)
        acc = jnp.where(jnp.logical_or(kt < j, causal), acc, -jnp.inf)
        bits = pltpu.bitcast(acc, I32)
        key_scr[kt] = bits ^ ((bits >> 31) & jnp.int32(0x7FFFFFFF))
        hi_scr[kt] = pltpu.bitcast(bits & jnp.int32(-65536), F32).astype(jnp.bfloat16)
        return carry

    lax.fori_loop(0, nk, score_tile, 0)

    key16_neg_inf = (0xFF80 ^ 0x7FFF) - 65536
    one_h = jnp.ones((16, tq), jnp.bfloat16)
    zero_h = jnp.zeros((16, tq), jnp.bfloat16)

    def select_threshold(nks):
        def count16(cand16):
            b16 = cand16 ^ ((cand16 >> 15) & 0x7FFF)
            b16 = jnp.where((cand16 > 0) & (cand16 < 0x80), 0x80, b16)
            cb =jnp.broadcast_to(pltpu.bitcast(b16 << 16, F32).astype(jnp.bfloat16), (16, tq))
            accs = [zero_h] * 4
            n = 0
            for kt in range(nks):
                for r in range(tq // 16):
                    ind = jnp.where(hi_scr[kt, r * 16:(r + 1) * 16, :] >= cb, one_h, zero_h)
                    accs[n % 4] = accs[n % 4] + ind
                    n += 1
            tot = (accs[0] + accs[1]).astype(F32) + (accs[2] + accs[3]).astype(F32)
            return jnp.sum(tot, axis=0, keepdims=True)

        def count32(pred_fn):
            accs = [jnp.zeros((8, tq), I32)] * 4
            n = 0
            for kt in range(nks):
                for r in range(tq // 8):
                    ind = jnp.where(pred_fn(key_scr[kt, r * 8:(r + 1) * 8, :]), 1, 0).astype(I32)
                    accs[n % 4] = accs[n % 4] + ind
                    n += 1
            return jnp.sum((accs[0] + accs[1]) + (accs[2] + accs[3]), axis=0, keepdims=True)

        def stage16(i, lo):
            cand = lo + (jnp.int32(1) << (15 - i))
            ok = (count16(cand) >= topk) | (cand <= key16_neg_inf)
            return jnp.where(ok, cand, lo)

        lo16 = lax.fori_loop(0, 16, stage16, jnp.full((1, tq), -32768, I32))

        def stage32(i, lo):
            cand = lo + (jnp.int32(1) << (15 - i))
            return jnp.where(count32(lambda k: k >= cand) >= topk, cand, lo)

        thr = lax.fori_loop(0, 16, stage32, lo16 << 16)
        n_gt = count32(lambda k: k > thr)
        thr_scr[...] = jnp.broadcast_to(thr, thr_scr.shape)
        need_scr[...] = jnp.broadcast_to((topk - n_gt).astype(F32), need_scr.shape)

    for jj in range(key_scr.shape[0]):
        @pl.when(j == jj)
        def _(jj=jj):
            if (jj + 1) * tq <= topk:
                thr_scr[...] = jnp.full(thr_scr.shape, int_min, I32)
                need_scr[...] = jnp.zeros(need_scr.shape, F32)
            else:
                select_threshold(jj + 1)

    thr = thr_scr[0:1, :]
    need = need_scr[0:1, :]

    tri = _mx(jnp.where(srow >= tcol, 1.0, 0.0))

    def mask_tile(kt, off):
        k = key_scr[kt]
        eq = k == thr
        eqf = jnp.where(eq, 1.0, 0.0)
        rank = jnp.dot(tri, _mx(eqf), preferred_element_type=F32) + off
        tie_bias = jnp.where(rank <= need, 0.0, NEG_BIG)
        bias_scr[kt] = jnp.where(k > thr, 0.0, jnp.where(eq, tie_bias, NEG_BIG))
        return off + jnp.sum(eqf, axis=0, keepdims=True)

    lax.fori_loop(0, nk, mask_tile, jnp.zeros((1, tq), F32))
    bias_scr[j] = jnp.where(causal, bias_scr[j], NEG_BIG)

    acc_scr[...] = jnp.zeros(acc_scr.shape, F32)
    q = qT_ref[0]
    zeros64 = jnp.zeros((64, tq), q.dtype)
    qwide = []
    for g in range(ATT_KV_HEADS):
        cols = []
        for hh in range(ATT_GROUP):
            hd = g * ATT_GROUP + hh
            qh = q[hd * 64:(hd + 1) * 64]
            cols.append(jnp.concatenate([qh, zeros64] if g == 0 else [zeros64, qh], axis=0))
        qwide.append(jnp.concatenate(cols, axis=1))

    ones_rows = jnp.ones((16, tq), q.dtype)

    def attend(kt, carry):
        ms, ls = carry
        kb = knat_ref[kt][:, 0:128]
        vtb = vT_ref[kt]
        bias = bias_scr[kt]
        lts = [jnp.dot(kb, qwide[g], preferred_element_type=F32) for g in range(ATT_KV_HEADS)]
        new_m, new_l = [], []
        for g in range(ATT_KV_HEADS):
            lt = jnp.concatenate([lts[g][:, hh * tq:(hh + 1) * tq] + bias for hh in range(ATT_GROUP)], axis=1)
            m_new = jnp.maximum(ms[g], jnp.max(lt, axis=0, keepdims=True))
            alpha = jnp.exp2(ms[g] - m_new)
            p = _mx(jnp.exp2(lt - m_new))
            v_aug = jnp.concatenate([vtb[g * 64:(g + 1) * 64], ones_rows], axis=0)
            pv = jnp.dot(v_aug, p, preferred_element_type=F32)
            new_l.append(alpha * ls[g] + pv[HEAD_DIM:HEAD_DIM + 1, :])
            acc_scr[g] = alpha * acc_scr[g] + pv[0:HEAD_DIM, :]
            new_m.append(m_new)
        return tuple(new_m), tuple(new_l)

    wide = ATT_GROUP * tq
    init = (tuple(jnp.full((1, wide), NEG_BIG, F32) for _ in range(ATT_KV_HEADS)),
            tuple(jnp.zeros((1, wide), F32) for _ in range(ATT_KV_HEADS)))
    _, ls = lax.fori_loop(0, nk, attend, init)
    outs = []
    for g in range(ATT_KV_HEADS):
        og = acc_scr[g] / ls[g]
        outs += [og[:, hh * tq:(hh + 1) * tq] for hh in range(ATT_GROUP)]
    o_ref[...] = jnp.concatenate(outs, axis=0).T.astype(o_ref.dtype)


def _dsa(qiT3, wgT3, knat3, qT3, vT3, batch, seq):
    tq = DSA_TILE
    nq = seq // tq
    n = batch * seq
    topk = min(INDEX_TOPK_MAX, seq // 4)
    qtile = lambda r: pl.BlockSpec((1, r, tq), lambda b, j: (b * nq + j, 0, 0))
    return pl.pallas_call(
        functools.partial(_dsa_kernel, topk=topk),
        grid=(batch, nq),
        in_specs=[qtile(512), qtile(16),
                  pl.BlockSpec((nq, tq, 256), lambda b, j: (b, 0, 0)),
                  qtile(512),
                  pl.BlockSpec((nq, 128, tq), lambda b, j: (b, 0, 0))],
        out_specs=pl.BlockSpec((tq, 512), lambda b, j: (b * nq + j, 0)),
        out_shape=jax.ShapeDtypeStruct((n, 512), MXU_DTYPE),
        scratch_shapes=[pltpu.VMEM((nq, tq, tq), I32), pltpu.VMEM((nq, tq, tq), jnp.bfloat16),
                        pltpu.VMEM((nq, tq, tq), F32),
                        pltpu.VMEM((ATT_KV_HEADS, HEAD_DIM, ATT_GROUP * tq), F32),
                        pltpu.VMEM((8, tq), I32), pltpu.VMEM((8, tq), F32)],
        compiler_params=_params("parallel", "arbitrary"),
        name="dsa",
    )(qiT3, wgT3, knat3, qT3, vT3)


def _hgrn_kernel(hg_ref, lb_ref, gn_ref, o_ref, st_scr, q_scr, k_scr, g_scr, qg_scr, kd_scr, o_scr, z_scr,
                 snap_scr):
    rows = hg_ref.shape[0]
    c = HG_SUB
    w = HG_WIDTH

    @pl.when(pl.program_id(1) == 0)
    def _():
        st_scr[...] = jnp.zeros(st_scr.shape, F32)
        z_scr[...] = jnp.zeros(z_scr.shape, z_scr.dtype)

    lb = lb_ref[...]
    q = _silu(hg_ref[:, 0:w]) * (HEAD_DIM ** -0.5)
    forget = lb + (1.0 - lb) * _sigmoid(hg_ref[:, w:2 * w])
    logf = jnp.log2(forget)
    kk = 1.0 - forget
    v = hg_ref[:, 2 * w:3 * w]

    ri = _iota((rows, rows), 0)
    ci = _iota((rows, rows), 1)
    same = (ri // c) == (ci // c)
    tri = _mx(jnp.where(same & (ci <= ri), 1.0, 0.0))
    ones_blk = _mx(jnp.where(same, 1.0, 0.0))
    gl = _dot01_left(tri, logf)
    glast = _dot01_left(ones_blk, logf)
    q_scr[...] = q
    k_scr[...] = gl - jnp.log2(kk)
    g_scr[...] = gl
    qg_scr[...] = q * jnp.exp2(gl)
    kd_scr[...] = kk * jnp.exp2(glast - gl)
    dec = jnp.exp2(glast)
    vT = _mx(v.T)
    head_ones = _mx(_block_ones(w, HEAD_DIM))
    pair_mask = _block_ones(LANE, HEAD_DIM)
    half = c // 2
    rhalf = _iota((half, w), 0)
    nsc = rows // c

    for sc in range(nsc):
        z_scr[sc, sc * c:(sc + 1) * c, :] = kd_scr[sc * c:(sc + 1) * c, :].astype(z_scr.dtype)
    for p in range(HG_HEADS // 2):
        ps = slice(p * LANE, (p + 1) * LANE)
        st = st_scr[p]
        for sc in range(nsc):
            snap_scr[sc, p] = st.astype(snap_scr.dtype)
            upd = jnp.dot(vT[ps, :], z_scr[sc, :, ps], preferred_element_type=F32)
            st = st * dec[sc * c:sc * c + 1, ps] + upd * pair_mask
        st_scr[p] = st

    for sc in range(nsc):
        r0 = sc * c
        q_top, q_bot = q_scr[r0:r0 + half, :], q_scr[r0 + half:r0 + c, :]
        g_top, g_bot = g_scr[r0:r0 + half, :], g_scr[r0 + half:r0 + c, :]
        bcast = lambda row: jnp.broadcast_to(row, (half, w))
        xs_top, xs_bot = [], []
        for s in range(c):
            hsr = bcast(k_scr[r0 + s:r0 + s + 1, :])
            if s < half:
                xs_top.append(q_top * jnp.exp2(jnp.where(rhalf >= s, g_top - hsr, -jnp.inf)))
                xs_bot.append(q_bot * jnp.exp2(g_bot - hsr))
            else:
                xs_bot.append(q_bot * jnp.exp2(jnp.where(rhalf >= s - half, g_bot - hsr, -jnp.inf)))
        y = jnp.dot(_mx(jnp.concatenate(xs_top + xs_bot, axis=0)), head_ones, preferred_element_type=F32)
        o_top = jnp.zeros((half, w), F32)
        o_bot = jnp.zeros((half, w), F32)
        for s in range(c):
            vsr = bcast(hg_ref[r0 + s:r0 + s + 1, 2 * w:3 * w])
            if s < half:
                o_top = o_top + y[s * half:(s + 1) * half, :] * vsr
            o_bot = o_bot + y[(half + s) * half:(half + s + 1) * half, :] * vsr
        o_diag = jnp.concatenate([o_top, o_bot], axis=0)
        o_pairs = [_dot_nt(qg_scr[r0:r0 + c, p * LANE:(p + 1) * LANE], snap_scr[sc, p])
                   for p in range(HG_HEADS // 2)]
        o_scr[r0:r0 + c, :] = jnp.concatenate(o_pairs, axis=1) + o_diag

    o = o_scr[...]
    ms = _dot01_right(o * o, head_ones) * (1.0 / HEAD_DIM)
    y = o * lax.rsqrt(ms + RMS_EPS) * gn_ref[...]
    o_ref[...] = (y * _silu(hg_ref[:, 3 * w:4 * w])).astype(o_ref.dtype)


def _hgrn(hg, lb, gn, batch, seq):
    rows = HG_ROWS
    ns = seq // rows
    n = batch * seq
    vm = lambda: pltpu.VMEM((rows, HG_WIDTH), F32)
    return pl.pallas_call(
        _hgrn_kernel,
        grid=(batch, ns),
        in_specs=[pl.BlockSpec((rows, 1024), lambda b, s: (b * ns + s, 0)),
                  _const_spec((1, HG_WIDTH)), _const_spec((1, HG_WIDTH))],
        out_specs=pl.BlockSpec((rows, HG_WIDTH), lambda b, s: (b * ns + s, 0)),
        out_shape=jax.ShapeDtypeStruct((n, HG_WIDTH), MXU_DTYPE),
        scratch_shapes=[pltpu.VMEM((HG_HEADS // 2, LANE, LANE), F32), vm(), vm(), vm(), vm(), vm(), vm(),
                        pltpu.VMEM((rows // HG_SUB, rows, HG_WIDTH), MXU_DTYPE),
                        pltpu.VMEM((rows // HG_SUB, HG_HEADS // 2, LANE, LANE), MXU_DTYPE)],
        compiler_params=_params("parallel", "arbitrary"),
        name="hgrn",
    )(hg, lb, gn)


def _mlstm_kernel(ml_ref, *refs):
    gT_refs = refs[:ML_SEQS]
    cw_ref, cb_ref, brow_ref, gn_ref, o_ref, prev_scr, c_scr, n_scr, m64_scr, m128_scr = refs[ML_SEQS:]

    @pl.when(pl.program_id(1) == 0)
    def _():
        prev_scr[...] = jnp.zeros(prev_scr.shape, F32)
        c_scr[...] = jnp.zeros(c_scr.shape, F32)
        n_scr[...] = jnp.zeros(n_scr.shape, F32)
        m64_scr[...] = jnp.full(m64_scr.shape, M_INIT, F32)
        m128_scr[...] = jnp.full(m128_scr.shape, M_INIT, F32)

    seqs = [_mlstm_sequence(ml_ref.at[bi], gT_ref, cw_ref, cb_ref, brow_ref, gn_ref, o_ref.at[bi],
                            prev_scr.at[bi], c_scr.at[bi], n_scr.at[bi], m64_scr.at[bi], m128_scr.at[bi])
            for bi, gT_ref in enumerate(gT_refs)]
    while seqs:
        for seq in list(seqs):
            if next(seq, "done") == "done":
                seqs.remove(seq)


def _mlstm_sequence(ml_ref, gT_ref, cw_ref, cb_ref, brow_ref, gn_ref, o_ref,
                    prev_scr, c_scr, n_scr, m64_scr, m128_scr):
    L = ML_CHUNK
    w = ML_WIDTH

    xqk = ml_ref[:, 0:512]
    prev = prev_scr[...]
    rows = _iota((L, 512), 0)
    acc = cb_ref[...] + cw_ref[ML_CONV - 1:ML_CONV, :] * xqk
    for sh in range(1, ML_CONV):
        shifted = jnp.where(rows >= sh, pltpu.roll(xqk, sh, 0), pltpu.roll(prev, sh, 0))
        acc = acc + cw_ref[ML_CONV - 1 - sh:ML_CONV - sh, :] * shifted
    prev_scr[...] = xqk
    qk = _silu(acc)
    q = qk[:, 0:w]
    k = qk[:, w:2 * w] * (HEAD_DIM ** -0.5)
    v = ml_ref[:, 512:768]

    srow = _iota((L, L), 0)
    scol = _iota((L, L), 1)
    tril = srow >= scol
    triu = _mx(jnp.where(srow <= scol, 1.0, 0.0))
    grow = gT_ref[0] + brow_ref[...]
    brow = _dot01_right(_log_sigmoid(grow), triu)
    yield

    def expand(x, width, row0):
        src = _iota((16, ML_HEADS * width), 0)
        dst = _iota((16, ML_HEADS * width), 1)
        e01 = _mx(jnp.where(src == row0 + dst // width, 1.0, 0.0))
        f = lambda p: lax.dot_general(p, e01, (((0,), (0,)), ((), ())), preferred_element_type=F32)
        hi, mid, lo = _split3(x)
        return f(hi) + f(mid) + f(lo)

    b128 = expand(brow, 128, 12)
    i128 = expand(grow, 128, 8)
    b64 = expand(brow, 64, 12)
    i64 = expand(grow, 64, 8)
    yield

    lane = _iota((L, LANE), 1)
    first = lane < HEAD_DIM
    pair_ones = _mx(_block_ones(LANE, HEAD_DIM))
    pair_mask = _block_ones(LANE, HEAD_DIM)
    m_prev128 = m128_scr[0:1, :]
    m_prev64 = m64_scr[0:1, :]
    n_row = n_scr[0:1, :]

    houts = []
    for p in range(ML_HEADS // 2):
        ps = slice(p * LANE, (p + 1) * LANE)
        q_pair, k_pair, v_pair = q[:, ps], k[:, ps], v[:, ps]
        per_head = []
        for hh in range(2):
            hd = 2 * p + hh
            hs = slice(hd * 128, (hd + 1) * 128)
            bt = b128[:, hs]
            d = bt - brow[12 + hd:13 + hd, :] + grow[8 + hd:9 + hd, :]
            d = jnp.where(tril, d, -jnp.inf)
            inter = bt + m_prev128[:, hs]
            m_t = jnp.maximum(inter, jnp.max(d, axis=1, keepdims=True))
            wgt = jnp.exp(d - m_t)
            a = jnp.exp(inter - m_t)
            qm = jnp.where(first if hh == 0 else ~first, q_pair, 0.0)
            s = _dot_nt(qm, k_pair) * wgt
            sv = _dot(s, v_pair)
            rs = jnp.sum(s, axis=1, keepdims=True)
            per_head.append((m_t, a, sv, rs))
            yield
        sel = lambda i: jnp.where(first, per_head[0][i], per_head[1][i])
        m_t, a, sv, rs = sel(0), sel(1), sel(2), sel(3)
        qc = _dot(q_pair, c_scr[p])
        qn = _dot(q_pair * n_row[:, ps], pair_ones)
        num = a * qc + sv
        den = a * qn + rs
        houts.append(num / jnp.maximum(jnp.abs(den), jnp.exp(-m_t)))
        yield
    hout = jnp.concatenate(houts, axis=1)

    def new_m(bx, ix, m_prev):
        blast = bx[L - 1:L, :]
        wl = blast - bx + ix
        m_new = jnp.maximum(blast + m_prev, jnp.max(wl, axis=0, keepdims=True))
        return blast, wl, m_new

    blast, wl, m_new64 = new_m(b64, i64, m_prev64)
    dec = jnp.exp(blast + m_prev64 - m_new64)
    kw = k * jnp.exp(wl - m_new64)
    for p in range(ML_HEADS // 2):
        ps = slice(p * LANE, (p + 1) * LANE)
        upd = jnp.dot(_mx(kw[:, ps].T), _mx(v[:, ps]), preferred_element_type=F32)
        c_scr[p] = c_scr[p] * dec[:, ps] + upd * pair_mask
    n_scr[...] = jnp.broadcast_to(dec * n_row + jnp.sum(kw, axis=0, keepdims=True), n_scr.shape)
    m64_scr[...] = jnp.broadcast_to(m_new64, m64_scr.shape)
    _, _, m_new128 = new_m(b128, i128, m_prev128)
    m128_scr[...] = jnp.broadcast_to(m_new128, m128_scr.shape)

    head_ones = _mx(_block_ones(w, HEAD_DIM))
    ms = _dot01_right(hout * hout, head_ones) * (1.0 / HEAD_DIM)
    y = hout * lax.rsqrt(ms + RMS_EPS) * gn_ref[...]
    o_ref[...] = (y * _sigmoid(ml_ref[:, 768:1024])).astype(o_ref.dtype)


def _mlstm(ml, wgT3, cw, cb, brow, gn, batch, seq):
    L = ML_CHUNK
    nc = seq // L
    n = batch * seq
    per_tile = DSA_TILE // L
    nb = ML_SEQS
    assert batch % nb == 0

    def gate_spec(off):
        def index(b, s):
            chunk = (nb * b + off) * nc + s
            return chunk // per_tile, 0, chunk % per_tile
        return pl.BlockSpec((1, 16, L), index)

    out = pl.pallas_call(
        _mlstm_kernel,
        grid=(batch // nb, nc),
        in_specs=[pl.BlockSpec((nb, L, 1024), lambda b, s: (b, s, 0))] + [gate_spec(o) for o in range(nb)]
                 + [_const_spec((ML_CONV, 512)), _const_spec((1, 512)),
                  _const_spec((16, L)), _const_spec((1, ML_WIDTH))],
        out_specs=pl.BlockSpec((nb, L, ML_WIDTH), lambda b, s: (b, s, 0)),
        out_shape=jax.ShapeDtypeStruct((batch, seq, ML_WIDTH), MXU_DTYPE),
        scratch_shapes=[pltpu.VMEM((nb, L, 512), F32), pltpu.VMEM((nb, 2, LANE, LANE), F32),
                        pltpu.VMEM((nb, 8, ML_WIDTH), F32), pltpu.VMEM((nb, 8, ML_WIDTH), F32),
                        pltpu.VMEM((nb, 8, 512), F32)],
        compiler_params=_params("parallel", "arbitrary"),
        name="mlstm",
    )(ml.reshape(batch, seq, 1024), *([wgT3] * nb), cw, cb, brow, gn)
    return out.reshape(n, ML_WIDTH)


def _rms(y, g):
    ms = jnp.mean(y * y, axis=-1, keepdims=True)
    return y * lax.rsqrt(ms + RMS_EPS) * g


def _mem_kv_kernel(m_ref, g_ref, w_ref, k_ref, v_ref):
    h = _mx(_rms(m_ref[...], g_ref[...]))
    kv = jnp.dot(h, w_ref[...], preferred_element_type=F32)
    k_ref[...] = kv[:, 0:D_MODEL].astype(k_ref.dtype)
    v_ref[...] = kv[:, D_MODEL:2 * D_MODEL].astype(v_ref.dtype)


def _mem_kv(mem2, g, w):
    n = mem2.shape[0]
    tm = ROW_TILE
    row = pl.BlockSpec((tm, D_MODEL), lambda i: (i, 0))
    return pl.pallas_call(
        _mem_kv_kernel,
        grid=(n // tm,),
        in_specs=[row, _const_spec((1, D_MODEL)), _const_spec((D_MODEL, 2 * D_MODEL))],
        out_specs=[row, row],
        out_shape=[jax.ShapeDtypeStruct((n, D_MODEL), MXU_DTYPE)] * 2,
        compiler_params=_params("parallel"),
        name="mem_kv",
    )(mem2, g, w)


def _post_mix_kernel(a_ref, b_ref, c_ref, x_ref, k_ref, v_ref, wout_ref, wq_ref, wo_ref, wu_ref, wd_ref,
                     g_ref, o_ref):
    g = g_ref[...]
    y = (jnp.dot(a_ref[...], wout_ref[0:512, :], preferred_element_type=F32)
         + jnp.dot(b_ref[...], wout_ref[512:768, :], preferred_element_type=F32)
         + jnp.dot(c_ref[...], wout_ref[768:1024, :], preferred_element_type=F32))
    x = x_ref[...] + _rms(y, g[0:1])
    h = _mx(_rms(x, g[1:2]))
    q = jnp.dot(h, wq_ref[...], preferred_element_type=F32) * (CROSS_HEAD_DIM ** -0.5)
    outs = []
    for hd in range(CROSS_HEADS):
        hs = slice(hd * CROSS_HEAD_DIM, (hd + 1) * CROSS_HEAD_DIM)
        logits = _dot_nt(q[:, hs], k_ref[:, hs])
        p = jnp.exp(logits - jnp.max(logits, axis=-1, keepdims=True))
        o = jnp.dot(_mx(p), v_ref[:, hs], preferred_element_type=F32)
        outs.append(o / jnp.sum(p, axis=-1, keepdims=True))
    y = jnp.dot(_mx(jnp.concatenate(outs, axis=1)), wo_ref[...], preferred_element_type=F32)
    x = x + _rms(y, g[2:3])
    h = _mx(_rms(x, g[3:4]))
    y = jnp.zeros(x.shape, F32)
    for c in range(MLP_HIDDEN // D_MODEL):
        cs = slice(c * D_MODEL, (c + 1) * D_MODEL)
        u = jnp.maximum(jnp.dot(h, wu_ref[:, cs], preferred_element_type=F32), 0.0)
        y = y + jnp.dot(_mx(u * u), wd_ref[cs, :], preferred_element_type=F32)
    o_ref[...] = x + _rms(y, g[4:5])


def _post_mix(a, b, c, x2, kmem, vmem, wout, wq, wo, wu, wd, gains, seq, n_mem):
    n = x2.shape[0]
    tm = POST_TILE
    per_b = seq // tm
    row = lambda width: pl.BlockSpec((tm, width), lambda i: (i, 0))
    memb =pl.BlockSpec((n_mem, D_MODEL), lambda i: (i // per_b, 0))
    sq = _const_spec((D_MODEL, D_MODEL))
    return pl.pallas_call(
        _post_mix_kernel,
        grid=(n // tm,),
        in_specs=[row(512), row(256), row(256), row(D_MODEL), memb, memb, sq, sq, sq,
                  _const_spec((D_MODEL, MLP_HIDDEN)), _const_spec((MLP_HIDDEN, D_MODEL)),
                  _const_spec((8, D_MODEL))],
        out_specs=row(D_MODEL),
        out_shape=jax.ShapeDtypeStruct((n, D_MODEL), F32),
        compiler_params=_params("parallel"),
        name="post_mix",
    )(a, b, c, x2, kmem, vmem, wout, wq, wo, wu, wd, gains)


def _rope_tables(seq):
    half = HEAD_DIM // 2
    inv = ROPE_THETA ** (-jnp.arange(0, HEAD_DIM, 2, dtype=F32) / HEAD_DIM)
    ang = jnp.arange(seq).astype(F32)[:, None] * inv[None, :]
    cos, sin = jnp.cos(ang), jnp.sin(ang)
    lane = jnp.arange(LANE)
    freq = (lane % HEAD_DIM) % half
    first = (lane % HEAD_DIM) < half
    cosn = cos[:, freq]
    sinn = sin[:, freq]
    sina = jnp.where(first[None, :], -sinn, 0.0)
    sinb = jnp.where(first[None, :], 0.0, sinn)
    return cosn, sina, sinb, cos.T, sin.T


def _layout_w_in(w_in):
    depth = w_in.shape[0]
    z = lambda wd: jnp.zeros((depth, D_MODEL, wd), w_in.dtype)
    col = lambda a, b: w_in[:, :, a:b]
    wn = jnp.concatenate([col(_O_AK, _O_AV), col(_O_XK, _O_XW), z(64),
                          col(_O_HQ, _O_MQK), col(_O_MQK, _O_MG)], axis=-1)
    wt = jnp.concatenate([col(_O_AQ, _O_AK), col(_O_XQ, _O_XK), col(_O_AV, _O_XQ),
                          col(_O_XW, _O_HQ), col(_O_MG, _IN_WIDTH)], axis=-1)
    return _mx(wn), _mx(jnp.swapaxes(wt, 1, 2))


def kernel(x, mem, mix_pre_g, w_in, ml_conv_w, ml_conv_b, ml_gate_b, hg_lb, hg_norm_g, ml_norm_g,
           w_out, mix_post_g, cross_pre_g, mem_norm_g, w_cq, w_ckv, w_co, cross_post_g, mlp_pre_g,
           w_up, w_down, mlp_post_g):
    batch, seq, _ = x.shape
    n_mem = mem.shape[1]
    depth = w_in.shape[0]
    n = batch * seq
    assert seq % max(ROW_TILE, POST_TILE, HG_ROWS, ML_CHUNK, DSA_TILE) == 0
    assert (batch * n_mem) % ROW_TILE == 0
    assert ROW_TILE % DSA_TILE == 0 and DSA_TILE % ML_CHUNK == 0

    lbs = jnp.cumsum(jax.nn.softmax(hg_lb.astype(F32), axis=0), axis=0)
    lbs = lbs - lbs[0:1]

    tabs = _rope_tables(seq)
    wn_all, wt_all = _layout_w_in(w_in)
    row = lambda a: a.astype(F32)[None, :]
    brow =jnp.pad(ml_gate_b.astype(F32), ((0, 0), (8, 0)))

    x2 = x.reshape(n, D_MODEL)
    mem2 = mem.reshape(batch * n_mem, D_MODEL)
    nl = n // DSA_TILE
    for l in range(depth):
        knat, hg, ml, qT3, qiT3, vT3, wgT3 = _in_proj(x2, row(mix_pre_g[l]), wn_all[l], wt_all[l], tabs, seq)
        a_out = _dsa(qiT3, wgT3, knat.reshape(nl, DSA_TILE, 256), qT3, vT3, batch, seq)
        b_out = _hgrn(hg, row(lbs[l]), row(jnp.tile(hg_norm_g[l], HG_HEADS)), batch, seq)
        c_out = _mlstm(ml, wgT3, ml_conv_w[l].astype(F32), row(ml_conv_b[l]),
                       jnp.broadcast_to(brow[l][:, None], (16, ML_CHUNK)),
                       row(jnp.tile(ml_norm_g[l], ML_HEADS)), batch, seq)
        kmem, vmem = _mem_kv(mem2, row(mem_norm_g[l]), _mx(w_ckv[l]))
        gains = jnp.stack([mix_post_g[l], cross_pre_g[l], cross_post_g[l], mlp_pre_g[l], mlp_post_g[l]]
                          + [jnp.zeros_like(mix_post_g[l])] * 3).astype(F32)
        x2 = _post_mix(a_out, b_out, c_out, x2, kmem, vmem, _mx(w_out[l]), _mx(w_cq[l]), _mx(w_co[l]),
                       _mx(w_up[l]), _mx(w_down[l]), gains, seq, n_mem)
    return x2.reshape(batch, seq, D_MODEL)
```

```python
import functools

import jax
import jax.numpy as jnp
from jax import lax
from jax.experimental import pallas as pl
from jax.experimental.pallas import tpu as pltpu

F32 = jnp.float32
I32 = jnp.int32
MXU_DTYPE = jnp.bfloat16

D_MODEL = 1024
RMS_EPS = 1e-6
ROPE_THETA = 10000.0
NEG_BIG = -1e30
M_INIT = -1e30
LOG2E = 1.4426950408889634
F32_MIN_NORMAL = 1.1754943508222875e-38
HEAD_DIM = 64
ATT_HEADS = 8
ATT_KV_HEADS = 2
ATT_GROUP = ATT_HEADS // ATT_KV_HEADS
IDX_HEADS = 8
INDEX_TOPK_MAX = 256
HG_HEADS = 4
HG_WIDTH = HG_HEADS * HEAD_DIM
ML_HEADS = 4
ML_WIDTH = ML_HEADS * HEAD_DIM
ML_CONV = 4
CROSS_HEADS = 4
CROSS_HEAD_DIM = D_MODEL // CROSS_HEADS
MLP_HIDDEN = 4 * D_MODEL

_O_AQ, _O_AK, _O_AV, _O_XQ, _O_XK, _O_XW = 0, 512, 640, 768, 1280, 1344
_O_HQ, _O_MQK, _O_MV, _O_MO, _O_MG, _IN_WIDTH = 1352, 2376, 2888, 3144, 3400, 3408

NAT_WIDTH = 128 + 128 + 1024 + 1024
TR_WIDTH = 512 + 512 + 128 + 16

LANE = 128
DSA_TILE = 256
ROW_TILE = 256
POST_TILE = 512
HG_SUB = 16
HG_ROWS = 256
ML_CHUNK = 128
ML_SEQS = 4
VMEM_LIMIT = 56 * 1024 * 1024

_NT = (((1,), (1,)), ((), ()))


def _mx(a):
    return a.astype(MXU_DTYPE)


def _dot(a, b):
    return jnp.dot(_mx(a), _mx(b), preferred_element_type=F32)


def _dot_nt(a, b):
    return lax.dot_general(_mx(a), _mx(b), _NT, preferred_element_type=F32)


def _split3(x):
    hi = _mx(x)
    r1 = x - hi.astype(F32)
    mid = _mx(r1)
    lo = _mx(r1 - mid.astype(F32))
    return hi, mid, lo


def _dot01_left(m01, x):
    hi, mid, lo = _split3(x)
    f = lambda p: jnp.dot(m01, p, preferred_element_type=F32)
    return f(hi) + f(mid) + f(lo)


def _dot01_right(x, m01):
    hi, mid, lo = _split3(x)
    f = lambda p: jnp.dot(p, m01, preferred_element_type=F32)
    return f(hi) + f(mid) + f(lo)


def _sigmoid(x):
    return 1.0 / (1.0 + jnp.exp(-x))


def _silu(x):
    return x * _sigmoid(x)


def _log_sigmoid(x):
    return jnp.minimum(x, 0.0) - jnp.log(1.0 + jnp.exp(-jnp.abs(x)))


def _iota(shape, dim):
    return lax.broadcasted_iota(I32, shape, dim)


def _block_ones(n, blk):
    same = (_iota((n, n), 0) // blk) == (_iota((n, n), 1) // blk)
    return jnp.where(same, 1.0, 0.0)


def _params(*sem):
    return pltpu.CompilerParams(dimension_semantics=sem, vmem_limit_bytes=VMEM_LIMIT)


def _const_spec(shape):
    nd = len(shape)
    return pl.BlockSpec(shape, lambda *_: (0,) * nd, pipeline_mode=pl.Buffered(1))


def _in_proj_kernel(x_ref, g_ref, wn_ref, wt_ref, cosn_ref, sina_ref, sinb_ref, cost_ref, sint_ref,
                    knat_ref, hg_ref, ml_ref, qT_ref, qiT_ref, vT_ref, wgT_ref, *, idx_scale):
    tm = x_ref.shape[0]
    x = x_ref[...]
    ms = jnp.mean(x * x, axis=-1, keepdims=True)
    h = _mx(x * lax.rsqrt(ms + RMS_EPS) * g_ref[...])
    nat = jnp.dot(h, wn_ref[...], preferred_element_type=F32)
    tr = lax.dot_general(wt_ref[...], h, _NT, preferred_element_type=F32)

    cosn, sina, sinb = cosn_ref[...], sina_ref[...], sinb_ref[...]

    def rope_nat(z):
        return z * cosn + pltpu.roll(z, 96, 1) * sina + pltpu.roll(z, 32, 1) * sinb

    knat_ref[:, 0:128] = rope_nat(nat[:, 0:128]).astype(knat_ref.dtype)
    knat_ref[:, 128:256] = rope_nat(nat[:, 128:256]).astype(knat_ref.dtype)
    hg_ref[...] = nat[:, 256:1280]
    ml_ref[...] = nat[:, 1280:NAT_WIDTH]

    cost, sint = cost_ref[...], sint_ref[...]

    def rope_t(z, scale):
        outs = []
        for hd in range(8):
            x1 = z[hd * 64:hd * 64 + 32]
            x2 = z[hd * 64 + 32:hd * 64 + 64]
            outs.append((x1 * cost - x2 * sint) * scale)
            outs.append((x2 * cost + x1 * sint) * scale)
        return jnp.concatenate(outs, axis=0)

    qT = rope_t(tr[0:512], (HEAD_DIM ** -0.5) * LOG2E).astype(qT_ref.dtype)
    qiT = rope_t(tr[512:1024], 1.0).astype(qiT_ref.dtype)
    vT = tr[1024:1152].astype(vT_ref.dtype)
    wg = tr[1152:1168]
    wg = jnp.where(_iota(wg.shape, 0) < 8, wg * idx_scale, wg)
    for c in range(tm // DSA_TILE):
        sl = slice(c * DSA_TILE, (c + 1) * DSA_TILE)
        qT_ref[c] = qT[:, sl]
        qiT_ref[c] = qiT[:, sl]
        vT_ref[c] = vT[:, sl]
        wgT_ref[c] = wg[:, sl]


def _in_proj(x2, g, wn, wt, tabs, seq):
    n = x2.shape[0]
    tm = ROW_TILE
    nt_seq = seq // tm
    cosn, sina, sinb, cost, sint = tabs
    idx_scale = (IDX_HEADS ** -0.5) * (HEAD_DIM ** -0.5)
    row = lambda w: pl.BlockSpec((tm, w), lambda i: (i, 0))
    tabn = pl.BlockSpec((tm, LANE), lambda i: (i % nt_seq, 0))
    tabt = pl.BlockSpec((32, tm), lambda i: (0, i % nt_seq))
    t3 = lambda r: pl.BlockSpec((tm // DSA_TILE, r, DSA_TILE), lambda i: (i, 0, 0))
    nl = n // DSA_TILE
    return pl.pallas_call(
        functools.partial(_in_proj_kernel, idx_scale=idx_scale),
        grid=(n // tm,),
        in_specs=[row(D_MODEL), _const_spec((1, D_MODEL)), _const_spec((D_MODEL, NAT_WIDTH)),
                  _const_spec((TR_WIDTH, D_MODEL)), tabn, tabn, tabn, tabt, tabt],
        out_specs=[row(256), row(1024), row(1024), t3(512), t3(512), t3(128), t3(16)],
        out_shape=[jax.ShapeDtypeStruct((n, 256), MXU_DTYPE),
                   jax.ShapeDtypeStruct((n, 1024), F32),
                   jax.ShapeDtypeStruct((n, 1024), F32),
                   jax.ShapeDtypeStruct((nl, 512, DSA_TILE), MXU_DTYPE),
                   jax.ShapeDtypeStruct((nl, 512, DSA_TILE), MXU_DTYPE),
                   jax.ShapeDtypeStruct((nl, 128, DSA_TILE), MXU_DTYPE),
                   jax.ShapeDtypeStruct((nl, 16, DSA_TILE), F32)],
        compiler_params=_params("parallel"),
        name="in_proj",
    )(x2, g, wn, wt, cosn, sina, sinb, cost, sint)


def _dsa_kernel(qiT_ref, wgT_ref, knat_ref, qT_ref, vT_ref, o_ref,
                key_scr, hi_scr, bias_scr, acc_scr, thr_scr, need_scr, ys_scr, lt_scr, *, topk):
    tq = DSA_TILE
    j = pl.program_id(1)
    nk = j + 1
    int_min = jnp.int32(-2 ** 31)
    srow = _iota((tq, tq), 0)
    tcol = _iota((tq, tq), 1)
    causal = srow <= tcol

    w = wgT_ref[0]
    qi = qiT_ref[0]
    qi_wide = jnp.concatenate([qi[hd * 64:(hd + 1) * 64] for hd in range(IDX_HEADS)], axis=1)

    last = nk - 1

    def score_matmul(kt, slot):
        kblk = knat_ref[kt][:, 128:192]
        ys_scr[slot] = jnp.dot(kblk, qi_wide, preferred_element_type=F32)

    def score_finish(kt, slot):
        acc = jnp.zeros((tq, tq), F32)
        for hd in range(IDX_HEADS):
            acc = acc + w[hd:hd + 1, :] * jnp.maximum(ys_scr[slot, :, hd * tq:(hd + 1) * tq], 0.0)
        acc = jnp.where(jnp.abs(acc) < F32_MIN_NORMAL, 0.0, acc)
        acc = jnp.where(jnp.logical_or(kt < j, causal), acc, -jnp.inf)
        bits = pltpu.bitcast(acc, I32)
        key_scr[kt] = bits ^ ((bits >> 31) & jnp.int32(0x7FFFFFFF))
        hi_scr[kt] = pltpu.bitcast(bits & jnp.int32(-65536), F32).astype(jnp.bfloat16)

    def score_pair(i, carry):
        score_matmul(2 * i + 1, 1)
        score_finish(2 * i, 0)
        score_matmul(jnp.minimum(2 * i + 2, last), 0)
        score_finish(2 * i + 1, 1)
        return carry

    score_matmul(0, 0)
    lax.fori_loop(0, nk // 2, score_pair, 0)

    @pl.when(nk % 2 == 1)
    def _():
        score_finish(last, 0)

    key16_neg_inf = (0xFF80 ^ 0x7FFF) - 65536
    one_h = jnp.ones((16, LANE), jnp.bfloat16)
    zero_h = jnp.zeros((16, LANE), jnp.bfloat16)

    def select_threshold(nks):
        halves = [slice(hf * LANE, (hf + 1) * LANE) for hf in range(tq // LANE)]

        def count16(cand16, ls):
            b16 = cand16 ^ ((cand16 >> 15) & 0x7FFF)
            b16 = jnp.where((cand16 > 0) & (cand16 < 0x80), 0x80, b16)
            cb = jnp.broadcast_to(pltpu.bitcast(b16 << 16, F32).astype(jnp.bfloat16), (16, LANE))
            accs = [zero_h] * 4
            n = 0
            for kt in range(nks):
                for r in range(tq // 16):
                    ind = jnp.where(hi_scr[kt, r * 16:(r + 1) * 16, ls] >= cb, one_h, zero_h)
                    accs[n % 4] = accs[n % 4] + ind
                    n += 1
            tot = (accs[0] + accs[1]).astype(F32) + (accs[2] + accs[3]).astype(F32)
            return jnp.sum(tot, axis=0, keepdims=True)

        def count32(pred_fn, ls):
            accs = [jnp.zeros((8, LANE), I32)] * 4
            n = 0
            for kt in range(nks):
                for r in range(tq // 8):
                    ind = jnp.where(pred_fn(key_scr[kt, r * 8:(r + 1) * 8, ls]), 1, 0).astype(I32)
                    accs[n % 4] = accs[n % 4] + ind
                    n += 1
            return jnp.sum((accs[0] + accs[1]) + (accs[2] + accs[3]), axis=0, keepdims=True)

        def stage16(i, los):
            out = []
            for lo, ls in zip(los, halves):
                cand = lo + (jnp.int32(1) << (15 - i))
                ok = (count16(cand, ls) >= topk) | (cand <= key16_neg_inf)
                out.append(jnp.where(ok, cand, lo))
            return tuple(out)

        lo16 = lax.fori_loop(0, 16, stage16, tuple(jnp.full((1, LANE), -32768, I32) for _ in halves))

        def stage32(i, los):
            out = []
            for lo, ls in zip(los, halves):
                cand = lo + (jnp.int32(1) << (15 - i))
                out.append(jnp.where(count32(lambda k: k >= cand, ls) >= topk, cand, lo))
            return tuple(out)

        thrs = lax.fori_loop(0, 16, stage32, tuple(lo << 16 for lo in lo16))
        for thr, ls in zip(thrs, halves):
            n_gt = count32(lambda k: k > thr, ls)
            thr_scr[:, ls] = jnp.broadcast_to(thr, (8, LANE))
            need_scr[:, ls] = jnp.broadcast_to((topk - n_gt).astype(F32), (8, LANE))

    for jj in range(key_scr.shape[0]):
        @pl.when(j == jj)
        def _(jj=jj):
            if (jj + 1) * tq <= topk:
                thr_scr[...] = jnp.full(thr_scr.shape, int_min, I32)
                need_scr[...] = jnp.zeros(need_scr.shape, F32)
            else:
                select_threshold(jj + 1)

    thr = thr_scr[0:1, :]
    need = need_scr[0:1, :]

    tri = _mx(jnp.where(srow >= tcol, 1.0, 0.0))

    def mask_tile(kt, off):
        k = key_scr[kt]
        eq = k == thr
        eqf = jnp.where(eq, 1.0, 0.0)
        rank = jnp.dot(tri, _mx(eqf), preferred_element_type=F32) + off
        tie_bias = jnp.where(rank <= need, 0.0, NEG_BIG)
        bias_scr[kt] = jnp.where(k > thr, 0.0, jnp.where(eq, tie_bias, NEG_BIG))
        return off + jnp.sum(eqf, axis=0, keepdims=True)

    lax.fori_loop(0, nk, mask_tile, jnp.zeros((1, tq), F32))
    bias_scr[j] = jnp.where(causal, bias_scr[j], NEG_BIG)

    acc_scr[...] = jnp.zeros(acc_scr.shape, F32)
    q = qT_ref[0]
    zeros64 = jnp.zeros((64, tq), q.dtype)
    qwide = []
    for g in range(ATT_KV_HEADS):
        cols = []
        for hh in range(ATT_GROUP):
            hd = g * ATT_GROUP + hh
            qh = q[hd * 64:(hd + 1) * 64]
            cols.append(jnp.concatenate([qh, zeros64] if g == 0 else [zeros64, qh], axis=0))
        qwide.append(jnp.concatenate(cols, axis=1))

    ones_rows = jnp.ones((16, tq), q.dtype)

    def logits_matmul(kt, slot):
        kb = knat_ref[kt][:, 0:128]
        for g in range(ATT_KV_HEADS):
            lt_scr[slot, g] = jnp.dot(kb, qwide[g], preferred_element_type=F32)

    def attend(kt, slot, carry):
        ms, ls = carry
        vtb = vT_ref[kt]
        bias = bias_scr[kt]
        new_m, new_l = [], []
        for g in range(ATT_KV_HEADS):
            lt = jnp.concatenate([lt_scr[slot, g, :, hh * tq:(hh + 1) * tq] + bias
                                  for hh in range(ATT_GROUP)], axis=1)
            m_new = jnp.maximum(ms[g], jnp.max(lt, axis=0, keepdims=True))
            alpha = jnp.exp2(ms[g] - m_new)
            p = _mx(jnp.exp2(lt - m_new))
            v_aug = jnp.concatenate([vtb[g * 64:(g + 1) * 64], ones_rows], axis=0)
            pv = jnp.dot(v_aug, p, preferred_element_type=F32)
            new_l.append(alpha * ls[g] + pv[HEAD_DIM:HEAD_DIM + 1, :])
            acc_scr[g] = alpha * acc_scr[g] + pv[0:HEAD_DIM, :]
            new_m.append(m_new)
        return tuple(new_m), tuple(new_l)

    wide = ATT_GROUP * tq
    init = (tuple(jnp.full((1, wide), NEG_BIG, F32) for _ in range(ATT_KV_HEADS)),
            tuple(jnp.zeros((1, wide), F32) for _ in range(ATT_KV_HEADS)))
    def attend_pair(i, carry):
        logits_matmul(2 * i + 1, 1)
        carry = attend(2 * i, 0, carry)
        logits_matmul(jnp.minimum(2 * i + 2, last), 0)
        return attend(2 * i + 1, 1, carry)

    logits_matmul(0, 0)
    carry = lax.fori_loop(0, nk // 2, attend_pair, init)
    _, ls = lax.cond(nk % 2 == 1, lambda c: attend(last, 0, c), lambda c: c, carry)
    outs = []
    for g in range(ATT_KV_HEADS):
        og = acc_scr[g] / ls[g]
        outs += [og[:, hh * tq:(hh + 1) * tq] for hh in range(ATT_GROUP)]
    o_ref[...] = jnp.concatenate(outs, axis=0).T.astype(o_ref.dtype)


def _dsa(qiT3, wgT3, knat3, qT3, vT3, batch, seq):
    tq = DSA_TILE
    nq = seq // tq
    n = batch * seq
    topk = min(INDEX_TOPK_MAX, seq // 4)
    qtile = lambda r: pl.BlockSpec((1, r, tq), lambda b, j: (b * nq + j, 0, 0))
    return pl.pallas_call(
        functools.partial(_dsa_kernel, topk=topk),
        grid=(batch, nq),
        in_specs=[qtile(512), qtile(16),
                  pl.BlockSpec((nq, tq, 256), lambda b, j: (b, 0, 0)),
                  qtile(512),
                  pl.BlockSpec((nq, 128, tq), lambda b, j: (b, 0, 0))],
        out_specs=pl.BlockSpec((tq, 512), lambda b, j: (b * nq + j, 0)),
        out_shape=jax.ShapeDtypeStruct((n, 512), MXU_DTYPE),
        scratch_shapes=[pltpu.VMEM((nq, tq, tq), I32), pltpu.VMEM((nq, tq, tq), jnp.bfloat16),
                        pltpu.VMEM((nq, tq, tq), F32),
                        pltpu.VMEM((ATT_KV_HEADS, HEAD_DIM, ATT_GROUP * tq), F32),
                        pltpu.VMEM((8, tq), I32), pltpu.VMEM((8, tq), F32),
                        pltpu.VMEM((2, tq, IDX_HEADS * tq), F32),
                        pltpu.VMEM((2, ATT_KV_HEADS, tq, ATT_GROUP * tq), F32)],
        compiler_params=_params("parallel", "arbitrary"),
        name="dsa",
    )(qiT3, wgT3, knat3, qT3, vT3)


def _hgrn_kernel(hg_ref, lb_ref, gn_ref, o_ref, st_scr, q_scr, k_scr, g_scr, qg_scr, kd_scr, o_scr, z_scr,
                 snap_scr):
    rows = hg_ref.shape[0]
    c = HG_SUB
    w = HG_WIDTH

    @pl.when(pl.program_id(1) == 0)
    def _():
        st_scr[...] = jnp.zeros(st_scr.shape, F32)
        z_scr[...] = jnp.zeros(z_scr.shape, z_scr.dtype)

    lb = lb_ref[...]
    q = _silu(hg_ref[:, 0:w]) * (HEAD_DIM ** -0.5)
    forget = lb + (1.0 - lb) * _sigmoid(hg_ref[:, w:2 * w])
    logf = jnp.log2(forget)
    kk = 1.0 - forget
    v = hg_ref[:, 2 * w:3 * w]

    ri = _iota((rows, rows), 0)
    ci = _iota((rows, rows), 1)
    same = (ri // c) == (ci // c)
    tri = _mx(jnp.where(same & (ci <= ri), 1.0, 0.0))
    ones_blk = _mx(jnp.where(same, 1.0, 0.0))
    gl = _dot01_left(tri, logf)
    glast = _dot01_left(ones_blk, logf)
    q_scr[...] = q
    k_scr[...] = gl - jnp.log2(kk)
    g_scr[...] = gl
    qg_scr[...] = q * jnp.exp2(gl)
    kd_scr[...] = kk * jnp.exp2(glast - gl)
    dec = jnp.exp2(glast)
    vT = _mx(v.T)
    head_ones = _mx(_block_ones(w, HEAD_DIM))
    pair_mask = _block_ones(LANE, HEAD_DIM)
    half = c // 2
    rhalf = _iota((half, w), 0)
    nsc = rows // c

    for sc in range(nsc):
        z_scr[sc, sc * c:(sc + 1) * c, :] = kd_scr[sc * c:(sc + 1) * c, :].astype(z_scr.dtype)
    for p in range(HG_HEADS // 2):
        ps = slice(p * LANE, (p + 1) * LANE)
        st = st_scr[p]
        for sc in range(nsc):
            snap_scr[sc, p] = st.astype(snap_scr.dtype)
            upd = jnp.dot(vT[ps, :], z_scr[sc, :, ps], preferred_element_type=F32)
            st = st * dec[sc * c:sc * c + 1, ps] + upd * pair_mask
        st_scr[p] = st

    for sc in range(nsc):
        r0 = sc * c
        q_top, q_bot = q_scr[r0:r0 + half, :], q_scr[r0 + half:r0 + c, :]
        g_top, g_bot = g_scr[r0:r0 + half, :], g_scr[r0 + half:r0 + c, :]
        bcast = lambda row: jnp.broadcast_to(row, (half, w))
        xs_top, xs_bot = [], []
        for s in range(c):
            hsr = bcast(k_scr[r0 + s:r0 + s + 1, :])
            if s < half:
                xs_top.append(q_top * jnp.exp2(jnp.where(rhalf >= s, g_top - hsr, -jnp.inf)))
                xs_bot.append(q_bot * jnp.exp2(g_bot - hsr))
            else:
                xs_bot.append(q_bot * jnp.exp2(jnp.where(rhalf >= s - half, g_bot - hsr, -jnp.inf)))
        y = jnp.dot(_mx(jnp.concatenate(xs_top + xs_bot, axis=0)), head_ones, preferred_element_type=F32)
        o_top = jnp.zeros((half, w), F32)
        o_bot = jnp.zeros((half, w), F32)
        for s in range(c):
            vsr = bcast(hg_ref[r0 + s:r0 + s + 1, 2 * w:3 * w])
            if s < half:
                o_top = o_top + y[s * half:(s + 1) * half, :] * vsr
            o_bot = o_bot + y[(half + s) * half:(half + s + 1) * half, :] * vsr
        o_diag = jnp.concatenate([o_top, o_bot], axis=0)
        o_pairs = [_dot_nt(qg_scr[r0:r0 + c, p * LANE:(p + 1) * LANE], snap_scr[sc, p])
                   for p in range(HG_HEADS // 2)]
        o_scr[r0:r0 + c, :] = jnp.concatenate(o_pairs, axis=1) + o_diag

    o = o_scr[...]
    ms = _dot01_right(o * o, head_ones) * (1.0 / HEAD_DIM)
    y = o * lax.rsqrt(ms + RMS_EPS) * gn_ref[...]
    o_ref[...] = (y * _silu(hg_ref[:, 3 * w:4 * w])).astype(o_ref.dtype)


def _hgrn(hg, lb, gn, batch, seq):
    rows = HG_ROWS
    ns = seq // rows
    n = batch * seq
    vm = lambda: pltpu.VMEM((rows, HG_WIDTH), F32)
    return pl.pallas_call(
        _hgrn_kernel,
        grid=(batch, ns),
        in_specs=[pl.BlockSpec((rows, 1024), lambda b, s: (b * ns + s, 0)),
                  _const_spec((1, HG_WIDTH)), _const_spec((1, HG_WIDTH))],
        out_specs=pl.BlockSpec((rows, HG_WIDTH), lambda b, s: (b * ns + s, 0)),
        out_shape=jax.ShapeDtypeStruct((n, HG_WIDTH), MXU_DTYPE),
        scratch_shapes=[pltpu.VMEM((HG_HEADS // 2, LANE, LANE), F32), vm(), vm(), vm(), vm(), vm(), vm(),
                        pltpu.VMEM((rows // HG_SUB, rows, HG_WIDTH), MXU_DTYPE),
                        pltpu.VMEM((rows // HG_SUB, HG_HEADS // 2, LANE, LANE), MXU_DTYPE)],
        compiler_params=_params("parallel", "arbitrary"),
        name="hgrn",
    )(hg, lb, gn)


def _mlstm_kernel(ml_ref, *refs):
    gT_refs = refs[:ML_SEQS]
    cw_ref, cb_ref, brow_ref, gn_ref, o_ref, prev_scr, c_scr, n_scr, m64_scr, m128_scr = refs[ML_SEQS:]

    @pl.when(pl.program_id(1) == 0)
    def _():
        prev_scr[...] = jnp.zeros(prev_scr.shape, F32)
        c_scr[...] = jnp.zeros(c_scr.shape, F32)
        n_scr[...] = jnp.zeros(n_scr.shape, F32)
        m64_scr[...] = jnp.full(m64_scr.shape, M_INIT, F32)
        m128_scr[...] = jnp.full(m128_scr.shape, M_INIT, F32)

    seqs = [_mlstm_sequence(ml_ref.at[bi], gT_ref, cw_ref, cb_ref, brow_ref, gn_ref, o_ref.at[bi],
                            prev_scr.at[bi], c_scr.at[bi], n_scr.at[bi], m64_scr.at[bi], m128_scr.at[bi])
            for bi, gT_ref in enumerate(gT_refs)]
    while seqs:
        for seq in list(seqs):
            if next(seq, "done") == "done":
                seqs.remove(seq)


def _mlstm_sequence(ml_ref, gT_ref, cw_ref, cb_ref, brow_ref, gn_ref, o_ref,
                    prev_scr, c_scr, n_scr, m64_scr, m128_scr):
    L = ML_CHUNK
    w = ML_WIDTH

    xqk = ml_ref[:, 0:512]
    prev = prev_scr[...]
    rows = _iota((L, 512), 0)
    acc = cb_ref[...] + cw_ref[ML_CONV - 1:ML_CONV, :] * xqk
    for sh in range(1, ML_CONV):
        shifted = jnp.where(rows >= sh, pltpu.roll(xqk, sh, 0), pltpu.roll(prev, sh, 0))
        acc = acc + cw_ref[ML_CONV - 1 - sh:ML_CONV - sh, :] * shifted
    prev_scr[...] = xqk
    qk = _silu(acc)
    q = qk[:, 0:w]
    k = qk[:, w:2 * w] * (HEAD_DIM ** -0.5)
    v = ml_ref[:, 512:768]

    srow = _iota((L, L), 0)
    scol = _iota((L, L), 1)
    tril = srow >= scol
    triu = _mx(jnp.where(srow <= scol, 1.0, 0.0))
    grow = gT_ref[0] + brow_ref[...]
    brow = _dot01_right(_log_sigmoid(grow), triu)
    yield

    def expand(x, width, row0):
        src = _iota((16, ML_HEADS * width), 0)
        dst = _iota((16, ML_HEADS * width), 1)
        e01 = _mx(jnp.where(src == row0 + dst // width, 1.0, 0.0))
        f = lambda p: lax.dot_general(p, e01, (((0,), (0,)), ((), ())), preferred_element_type=F32)
        hi, mid, lo = _split3(x)
        return f(hi) + f(mid) + f(lo)

    b128 = expand(brow, 128, 12)
    i128 = expand(grow, 128, 8)
    b64 = expand(brow, 64, 12)
    i64 = expand(grow, 64, 8)
    yield

    lane = _iota((L, LANE), 1)
    first = lane < HEAD_DIM
    pair_ones = _mx(_block_ones(LANE, HEAD_DIM))
    pair_mask = _block_ones(LANE, HEAD_DIM)
    m_prev128 = m128_scr[0:1, :]
    m_prev64 = m64_scr[0:1, :]
    n_row = n_scr[0:1, :]

    houts = []
    for p in range(ML_HEADS // 2):
        ps = slice(p * LANE, (p + 1) * LANE)
        q_pair, k_pair, v_pair = q[:, ps], k[:, ps], v[:, ps]
        per_head = []
        for hh in range(2):
            hd = 2 * p + hh
            hs = slice(hd * 128, (hd + 1) * 128)
            bt = b128[:, hs]
            d = bt - brow[12 + hd:13 + hd, :] + grow[8 + hd:9 + hd, :]
            d = jnp.where(tril, d, -jnp.inf)
            inter = bt + m_prev128[:, hs]
            m_t = jnp.maximum(inter, jnp.max(d, axis=1, keepdims=True))
            wgt = jnp.exp(d - m_t)
            a = jnp.exp(inter - m_t)
            qm = jnp.where(first if hh == 0 else ~first, q_pair, 0.0)
            s = _dot_nt(qm, k_pair) * wgt
            sv = _dot(s, v_pair)
            rs = jnp.sum(s, axis=1, keepdims=True)
            per_head.append((m_t, a, sv, rs))
            yield
        sel = lambda i: jnp.where(first, per_head[0][i], per_head[1][i])
        m_t, a, sv, rs = sel(0), sel(1), sel(2), sel(3)
        qc = _dot(q_pair, c_scr[p])
        qn = _dot(q_pair * n_row[:, ps], pair_ones)
        num = a * qc + sv
        den = a * qn + rs
        houts.append(num / jnp.maximum(jnp.abs(den), jnp.exp(-m_t)))
        yield
    hout = jnp.concatenate(houts, axis=1)

    def new_m(bx, ix, m_prev):
        blast = bx[L - 1:L, :]
        wl = blast - bx + ix
        m_new = jnp.maximum(blast + m_prev, jnp.max(wl, axis=0, keepdims=True))
        return blast, wl, m_new

    blast, wl, m_new64 = new_m(b64, i64, m_prev64)
    dec = jnp.exp(blast + m_prev64 - m_new64)
    kw = k * jnp.exp(wl - m_new64)
    for p in range(ML_HEADS // 2):
        ps = slice(p * LANE, (p + 1) * LANE)
        upd = jnp.dot(_mx(kw[:, ps].T), _mx(v[:, ps]), preferred_element_type=F32)
        c_scr[p] = c_scr[p] * dec[:, ps] + upd * pair_mask
    n_scr[...] = jnp.broadcast_to(dec * n_row + jnp.sum(kw, axis=0, keepdims=True), n_scr.shape)
    m64_scr[...] = jnp.broadcast_to(m_new64, m64_scr.shape)
    _, _, m_new128 = new_m(b128, i128, m_prev128)
    m128_scr[...] = jnp.broadcast_to(m_new128, m128_scr.shape)

    head_ones = _mx(_block_ones(w, HEAD_DIM))
    ms = _dot01_right(hout * hout, head_ones) * (1.0 / HEAD_DIM)
    y = hout * lax.rsqrt(ms + RMS_EPS) * gn_ref[...]
    o_ref[...] = (y * _sigmoid(ml_ref[:, 768:1024])).astype(o_ref.dtype)


def _mlstm(ml, wgT3, cw, cb, brow, gn, batch, seq):
    L = ML_CHUNK
    nc = seq // L
    n = batch * seq
    per_tile = DSA_TILE // L
    nb = ML_SEQS
    assert batch % nb == 0

    def gate_spec(off):
        def index(b, s):
            chunk = (nb * b + off) * nc + s
            return chunk // per_tile, 0, chunk % per_tile
        return pl.BlockSpec((1, 16, L), index)

    out = pl.pallas_call(
        _mlstm_kernel,
        grid=(batch // nb, nc),
        in_specs=[pl.BlockSpec((nb, L, 1024), lambda b, s: (b, s, 0))] + [gate_spec(o) for o in range(nb)]
                 + [_const_spec((ML_CONV, 512)), _const_spec((1, 512)),
                  _const_spec((16, L)), _const_spec((1, ML_WIDTH))],
        out_specs=pl.BlockSpec((nb, L, ML_WIDTH), lambda b, s: (b, s, 0)),
        out_shape=jax.ShapeDtypeStruct((batch, seq, ML_WIDTH), MXU_DTYPE),
        scratch_shapes=[pltpu.VMEM((nb, L, 512), F32), pltpu.VMEM((nb, 2, LANE, LANE), F32),
                        pltpu.VMEM((nb, 8, ML_WIDTH), F32), pltpu.VMEM((nb, 8, ML_WIDTH), F32),
                        pltpu.VMEM((nb, 8, 512), F32)],
        compiler_params=_params("parallel", "arbitrary"),
        name="mlstm",
    )(ml.reshape(batch, seq, 1024), *([wgT3] * nb), cw, cb, brow, gn)
    return out.reshape(n, ML_WIDTH)


def _rms(y, g):
    ms = jnp.mean(y * y, axis=-1, keepdims=True)
    return y * lax.rsqrt(ms + RMS_EPS) * g


def _mem_kv_kernel(m_ref, g_ref, w_ref, k_ref, v_ref):
    h = _mx(_rms(m_ref[...], g_ref[...]))
    kv = jnp.dot(h, w_ref[...], preferred_element_type=F32)
    k_ref[...] = kv[:, 0:D_MODEL].astype(k_ref.dtype)
    v_ref[...] = kv[:, D_MODEL:2 * D_MODEL].astype(v_ref.dtype)


def _mem_kv(mem2, g, w):
    n = mem2.shape[0]
    tm = ROW_TILE
    row = pl.BlockSpec((tm, D_MODEL), lambda i: (i, 0))
    return pl.pallas_call(
        _mem_kv_kernel,
        grid=(n // tm,),
        in_specs=[row, _const_spec((1, D_MODEL)), _const_spec((D_MODEL, 2 * D_MODEL))],
        out_specs=[row, row],
        out_shape=[jax.ShapeDtypeStruct((n, D_MODEL), MXU_DTYPE)] * 2,
        compiler_params=_params("parallel"),
        name="mem_kv",
    )(mem2, g, w)


def _post_mix_kernel(a_ref, b_ref, c_ref, x_ref, k_ref, v_ref, wout_ref, wq_ref, wo_ref, wu_ref, wd_ref,
                     g_ref, o_ref):
    g = g_ref[...]
    y = (jnp.dot(a_ref[...], wout_ref[0:512, :], preferred_element_type=F32)
         + jnp.dot(b_ref[...], wout_ref[512:768, :], preferred_element_type=F32)
         + jnp.dot(c_ref[...], wout_ref[768:1024, :], preferred_element_type=F32))
    x = x_ref[...] + _rms(y, g[0:1])
    h = _mx(_rms(x, g[1:2]))
    q = jnp.dot(h, wq_ref[...], preferred_element_type=F32) * (CROSS_HEAD_DIM ** -0.5)
    outs = []
    for hd in range(CROSS_HEADS):
        hs = slice(hd * CROSS_HEAD_DIM, (hd + 1) * CROSS_HEAD_DIM)
        logits = _dot_nt(q[:, hs], k_ref[:, hs])
        p = jnp.exp(logits - jnp.max(logits, axis=-1, keepdims=True))
        o = jnp.dot(_mx(p), v_ref[:, hs], preferred_element_type=F32)
        outs.append(o / jnp.sum(p, axis=-1, keepdims=True))
    y = jnp.dot(_mx(jnp.concatenate(outs, axis=1)), wo_ref[...], preferred_element_type=F32)
    x = x + _rms(y, g[2:3])
    h = _mx(_rms(x, g[3:4]))
    y = jnp.zeros(x.shape, F32)
    for c in range(MLP_HIDDEN // D_MODEL):
        cs = slice(c * D_MODEL, (c + 1) * D_MODEL)
        u = jnp.maximum(jnp.dot(h, wu_ref[:, cs], preferred_element_type=F32), 0.0)
        y = y + jnp.dot(_mx(u * u), wd_ref[cs, :], preferred_element_type=F32)
    o_ref[...] = x + _rms(y, g[4:5])


def _post_mix(a, b, c, x2, kmem, vmem, wout, wq, wo, wu, wd, gains, seq, n_mem):
    n = x2.shape[0]
    tm = POST_TILE
    per_b = seq // tm
    row = lambda width: pl.BlockSpec((tm, width), lambda i: (i, 0))
    memb =pl.BlockSpec((n_mem, D_MODEL), lambda i: (i // per_b, 0))
    sq = _const_spec((D_MODEL, D_MODEL))
    return pl.pallas_call(
        _post_mix_kernel,
        grid=(n // tm,),
        in_specs=[row(512), row(256), row(256), row(D_MODEL), memb, memb, sq, sq, sq,
                  _const_spec((D_MODEL, MLP_HIDDEN)), _const_spec((MLP_HIDDEN, D_MODEL)),
                  _const_spec((8, D_MODEL))],
        out_specs=row(D_MODEL),
        out_shape=jax.ShapeDtypeStruct((n, D_MODEL), F32),
        compiler_params=_params("parallel"),
        name="post_mix",
    )(a, b, c, x2, kmem, vmem, wout, wq, wo, wu, wd, gains)


def _rope_tables(seq):
    half = HEAD_DIM // 2
    inv = ROPE_THETA ** (-jnp.arange(0, HEAD_DIM, 2, dtype=F32) / HEAD_DIM)
    ang = jnp.arange(seq).astype(F32)[:, None] * inv[None, :]
    cos, sin = jnp.cos(ang), jnp.sin(ang)
    lane = jnp.arange(LANE)
    freq = (lane % HEAD_DIM) % half
    first = (lane % HEAD_DIM) < half
    cosn = cos[:, freq]
    sinn = sin[:, freq]
    sina = jnp.where(first[None, :], -sinn, 0.0)
    sinb = jnp.where(first[None, :], 0.0, sinn)
    return cosn, sina, sinb, cos.T, sin.T


def _layout_w_in(w_in):
    depth = w_in.shape[0]
    z = lambda wd: jnp.zeros((depth, D_MODEL, wd), w_in.dtype)
    col = lambda a, b: w_in[:, :, a:b]
    wn = jnp.concatenate([col(_O_AK, _O_AV), col(_O_XK, _O_XW), z(64),
                          col(_O_HQ, _O_MQK), col(_O_MQK, _O_MG)], axis=-1)
    wt = jnp.concatenate([col(_O_AQ, _O_AK), col(_O_XQ, _O_XK), col(_O_AV, _O_XQ),
                          col(_O_XW, _O_HQ), col(_O_MG, _IN_WIDTH)], axis=-1)
    return _mx(wn), _mx(jnp.swapaxes(wt, 1, 2))


def kernel(x, mem, mix_pre_g, w_in, ml_conv_w, ml_conv_b, ml_gate_b, hg_lb, hg_norm_g, ml_norm_g,
           w_out, mix_post_g, cross_pre_g, mem_norm_g, w_cq, w_ckv, w_co, cross_post_g, mlp_pre_g,
           w_up, w_down, mlp_post_g):
    batch, seq, _ = x.shape
    n_mem = mem.shape[1]
    depth = w_in.shape[0]
    n = batch * seq
    assert seq % max(ROW_TILE, POST_TILE, HG_ROWS, ML_CHUNK, DSA_TILE) == 0
    assert (batch * n_mem) % ROW_TILE == 0
    assert ROW_TILE % DSA_TILE == 0 and DSA_TILE % ML_CHUNK == 0

    lbs = jnp.cumsum(jax.nn.softmax(hg_lb.astype(F32), axis=0), axis=0)
    lbs = lbs - lbs[0:1]

    tabs = _rope_tables(seq)
    wn_all, wt_all = _layout_w_in(w_in)
    row = lambda a: a.astype(F32)[None, :]
    brow =jnp.pad(ml_gate_b.astype(F32), ((0, 0), (8, 0)))

    x2 = x.reshape(n, D_MODEL)
    mem2 = mem.reshape(batch * n_mem, D_MODEL)
    nl = n // DSA_TILE
    for l in range(depth):
        knat, hg, ml, qT3, qiT3, vT3, wgT3 = _in_proj(x2, row(mix_pre_g[l]), wn_all[l], wt_all[l], tabs, seq)
        a_out = _dsa(qiT3, wgT3, knat.reshape(nl, DSA_TILE, 256), qT3, vT3, batch, seq)
        b_out = _hgrn(hg, row(lbs[l]), row(jnp.tile(hg_norm_g[l], HG_HEADS)), batch, seq)
        c_out = _mlstm(ml, wgT3, ml_conv_w[l].astype(F32), row(ml_conv_b[l]),
                       jnp.broadcast_to(brow[l][:, None], (16, ML_CHUNK)),
                       row(jnp.tile(ml_norm_g[l], ML_HEADS)), batch, seq)
        kmem, vmem = _mem_kv(mem2, row(mem_norm_g[l]), _mx(w_ckv[l]))
        gains = jnp.stack([mix_post_g[l], cross_pre_g[l], cross_post_g[l], mlp_pre_g[l], mlp_post_g[l]]
                          + [jnp.zeros_like(mix_post_g[l])] * 3).astype(F32)
        x2 = _post_mix(a_out, b_out, c_out, x2, kmem, vmem, _mx(w_out[l]), _mx(w_cq[l]), _mx(w_co[l]),
                       _mx(w_up[l]), _mx(w_down[l]), gains, seq, n_mem)
    return x2.reshape(batch, seq, D_MODEL)
```

```python
import functools

import jax
import jax.numpy as jnp
from jax import lax
from jax.experimental import pallas as pl
from jax.experimental.pallas import tpu as pltpu

F32 = jnp.float32
I32 = jnp.int32
MXU_DTYPE = jnp.bfloat16

D_MODEL = 1024
RMS_EPS = 1e-6
ROPE_THETA = 10000.0
NEG_BIG = -1e30
M_INIT = -1e30
LOG2E = 1.4426950408889634
F32_MIN_NORMAL = 1.1754943508222875e-38
HEAD_DIM = 64
ATT_HEADS = 8
ATT_KV_HEADS = 2
ATT_GROUP = ATT_HEADS // ATT_KV_HEADS
IDX_HEADS = 8
INDEX_TOPK_MAX = 256
HG_HEADS = 4
HG_WIDTH = HG_HEADS * HEAD_DIM
ML_HEADS = 4
ML_WIDTH = ML_HEADS * HEAD_DIM
ML_CONV = 4
CROSS_HEADS = 4
CROSS_HEAD_DIM = D_MODEL // CROSS_HEADS
MLP_HIDDEN = 4 * D_MODEL

_O_AQ, _O_AK, _O_AV, _O_XQ, _O_XK, _O_XW = 0, 512, 640, 768, 1280, 1344
_O_HQ, _O_MQK, _O_MV, _O_MO, _O_MG, _IN_WIDTH = 1352, 2376, 2888, 3144, 3400, 3408

NAT_WIDTH = 128 + 128 + 1024 + 1024
TR_WIDTH = 512 + 512 + 128 + 16

LANE = 128
DSA_TILE = 256
ROW_TILE = 256
POST_TILE = 512
HG_SUB = 16
HG_ROWS = 256
ML_CHUNK = 128
ML_SEQS = 4
VMEM_LIMIT = 56 * 1024 * 1024

_NT = (((1,), (1,)), ((), ()))


def _mx(a):
    return a.astype(MXU_DTYPE)


def _dot(a, b):
    return jnp.dot(_mx(a), _mx(b), preferred_element_type=F32)


def _dot_nt(a, b):
    return lax.dot_general(_mx(a), _mx(b), _NT, preferred_element_type=F32)


def _split3(x):
    hi = _mx(x)
    r1 = x - hi.astype(F32)
    mid = _mx(r1)
    lo = _mx(r1 - mid.astype(F32))
    return hi, mid, lo


def _dot01_left(m01, x):
    hi, mid, lo = _split3(x)
    f = lambda p: jnp.dot(m01, p, preferred_element_type=F32)
    return f(hi) + f(mid) + f(lo)


def _dot01_right(x, m01):
    hi, mid, lo = _split3(x)
    f = lambda p: jnp.dot(p, m01, preferred_element_type=F32)
    return f(hi) + f(mid) + f(lo)


def _sigmoid(x):
    return 1.0 / (1.0 + jnp.exp(-x))


def _silu(x):
    return x * _sigmoid(x)


def _log_sigmoid(x):
    return jnp.minimum(x, 0.0) - jnp.log(1.0 + jnp.exp(-jnp.abs(x)))


def _iota(shape, dim):
    return lax.broadcasted_iota(I32, shape, dim)


def _block_ones(n, blk):
    same = (_iota((n, n), 0) // blk) == (_iota((n, n), 1) // blk)
    return jnp.where(same, 1.0, 0.0)


def _params(*sem):
    return pltpu.CompilerParams(dimension_semantics=sem, vmem_limit_bytes=VMEM_LIMIT)


def _const_spec(shape):
    nd = len(shape)
    return pl.BlockSpec(shape, lambda *_: (0,) * nd, pipeline_mode=pl.Buffered(1))


def _in_proj_kernel(x_ref, g_ref, wn_ref, wt_ref, cosn_ref, sina_ref, sinb_ref, cost_ref, sint_ref,
                    knat_ref, hg_ref, ml_ref, qT_ref, qiT_ref, vT_ref, wgT_ref, *, idx_scale):
    tm = x_ref.shape[0]
    x = x_ref[...]
    ms = jnp.mean(x * x, axis=-1, keepdims=True)
    h = _mx(x * lax.rsqrt(ms + RMS_EPS) * g_ref[...])
    nat = jnp.dot(h, wn_ref[...], preferred_element_type=F32)
    tr = lax.dot_general(wt_ref[...], h, _NT, preferred_element_type=F32)

    cosn, sina, sinb = cosn_ref[...], sina_ref[...], sinb_ref[...]

    def rope_nat(z):
        return z * cosn + pltpu.roll(z, 96, 1) * sina + pltpu.roll(z, 32, 1) * sinb

    knat_ref[:, 0:128] = rope_nat(nat[:, 0:128]).astype(knat_ref.dtype)
    knat_ref[:, 128:256] = rope_nat(nat[:, 128:256]).astype(knat_ref.dtype)
    hg_ref[...] = nat[:, 256:1280]
    ml_ref[...] = nat[:, 1280:NAT_WIDTH]

    cost, sint = cost_ref[...], sint_ref[...]

    def rope_t(z, scale):
        outs = []
        for hd in range(8):
            x1 = z[hd * 64:hd * 64 + 32]
            x2 = z[hd * 64 + 32:hd * 64 + 64]
            outs.append((x1 * cost - x2 * sint) * scale)
            outs.append((x2 * cost + x1 * sint) * scale)
        return jnp.concatenate(outs, axis=0)

    qT = rope_t(tr[0:512], (HEAD_DIM ** -0.5) * LOG2E).astype(qT_ref.dtype)
    qiT = rope_t(tr[512:1024], 1.0).astype(qiT_ref.dtype)
    vT = tr[1024:1152].astype(vT_ref.dtype)
    wg = tr[1152:1168]
    wg = jnp.where(_iota(wg.shape, 0) < 8, wg * idx_scale, wg)
    for c in range(tm // DSA_TILE):
        sl = slice(c * DSA_TILE, (c + 1) * DSA_TILE)
        qT_ref[c] = qT[:, sl]
        qiT_ref[c] = qiT[:, sl]
        vT_ref[c] = vT[:, sl]
        wgT_ref[c] = wg[:, sl]


def _in_proj(x2, g, wn, wt, tabs, seq):
    n = x2.shape[0]
    tm = ROW_TILE
    nt_seq = seq // tm
    cosn, sina, sinb, cost, sint = tabs
    idx_scale = (IDX_HEADS ** -0.5) * (HEAD_DIM ** -0.5)
    row = lambda w: pl.BlockSpec((tm, w), lambda i: (i, 0))
    tabn = pl.BlockSpec((tm, LANE), lambda i: (i % nt_seq, 0))
    tabt = pl.BlockSpec((32, tm), lambda i: (0, i % nt_seq))
    t3 = lambda r: pl.BlockSpec((tm // DSA_TILE, r, DSA_TILE), lambda i: (i, 0, 0))
    nl = n // DSA_TILE
    return pl.pallas_call(
        functools.partial(_in_proj_kernel, idx_scale=idx_scale),
        grid=(n // tm,),
        in_specs=[row(D_MODEL), _const_spec((1, D_MODEL)), _const_spec((D_MODEL, NAT_WIDTH)),
                  _const_spec((TR_WIDTH, D_MODEL)), tabn, tabn, tabn, tabt, tabt],
        out_specs=[row(256), row(1024), row(1024), t3(512), t3(512), t3(128), t3(16)],
        out_shape=[jax.ShapeDtypeStruct((n, 256), MXU_DTYPE),
                   jax.ShapeDtypeStruct((n, 1024), F32),
                   jax.ShapeDtypeStruct((n, 1024), F32),
                   jax.ShapeDtypeStruct((nl, 512, DSA_TILE), MXU_DTYPE),
                   jax.ShapeDtypeStruct((nl, 512, DSA_TILE), MXU_DTYPE),
                   jax.ShapeDtypeStruct((nl, 128, DSA_TILE), MXU_DTYPE),
                   jax.ShapeDtypeStruct((nl, 16, DSA_TILE), F32)],
        compiler_params=_params("parallel"),
        name="in_proj",
    )(x2, g, wn, wt, cosn, sina, sinb, cost, sint)


def _dsa_kernel(qiT_ref, wgT_ref, knat_ref, qT_ref, vT_ref, o_ref,
                key_scr, hi_scr, bias_scr, acc_scr, thr_scr, nge_scr, *, topk):
    tq = DSA_TILE
    j = pl.program_id(1)
    nk = j + 1
    int_min = jnp.int32(-2 ** 31)
    srow = _iota((tq, tq), 0)
    tcol = _iota((tq, tq), 1)
    causal = srow <= tcol

    w = wgT_ref[0]
    qi = qiT_ref[0]
    qi_wide = jnp.concatenate([qi[hd * 64:(hd + 1) * 64] for hd in range(IDX_HEADS)], axis=1)

    def score_tile(kt, carry):
        kblk = knat_ref[kt][:, 128:192]
        acc = jnp.zeros((tq, tq), F32)
        ys = jnp.dot(kblk, qi_wide, preferred_element_type=F32)
        for hd in range(IDX_HEADS):
            acc = acc + w[hd:hd + 1, :] * jnp.maximum(ys[:, hd * tq:(hd + 1) * tq], 0.0)
        acc = jnp.where(jnp.abs(acc) < F32_MIN_NORMAL, 0.0, acc)
        acc = jnp.where(jnp.logical_or(kt < j, causal), acc, -jnp.inf)
        bits = pltpu.bitcast(acc, I32)
        key_scr[kt] = bits ^ ((bits >> 31) & jnp.int32(0x7FFFFFFF))
        hi_scr[kt] = pltpu.bitcast(bits & jnp.int32(-65536), F32).astype(jnp.bfloat16)
        return carry

    lax.fori_loop(0, nk, score_tile, 0)

    key16_neg_inf = (0xFF80 ^ 0x7FFF) - 65536
    one_h = jnp.ones((16, tq), jnp.bfloat16)
    zero_h = jnp.zeros((16, tq), jnp.bfloat16)

    def select_threshold(nks):
        def count16(cand16):
            b16 = cand16 ^ ((cand16 >> 15) & 0x7FFF)
            b16 = jnp.where((cand16 > 0) & (cand16 < 0x80), 0x80, b16)
            cb = jnp.broadcast_to(pltpu.bitcast(b16 << 16, F32).astype(jnp.bfloat16), (16, tq))
            accs = [zero_h] * 4
            n = 0
            for kt in range(nks):
                for r in range(tq // 16):
                    ind = jnp.where(hi_scr[kt, r * 16:(r + 1) * 16, :] >= cb, one_h, zero_h)
                    accs[n % 4] = accs[n % 4] + ind
                    n += 1
            tot = (accs[0] + accs[1]).astype(F32) + (accs[2] + accs[3]).astype(F32)
            return jnp.sum(tot, axis=0, keepdims=True)

        def count32(pred_fn):
            accs = [jnp.zeros((8, tq), I32)] * 4
            n = 0
            for kt in range(nks):
                for r in range(tq // 8):
                    ind = jnp.where(pred_fn(key_scr[kt, r * 8:(r + 1) * 8, :]), 1, 0).astype(I32)
                    accs[n % 4] = accs[n % 4] + ind
                    n += 1
            return jnp.sum((accs[0] + accs[1]) + (accs[2] + accs[3]), axis=0, keepdims=True)

        def stage16(i, lo):
            cand = lo + (jnp.int32(1) << (15 - i))
            ok = (count16(cand) >= topk) | (cand <= key16_neg_inf)
            return jnp.where(ok, cand, lo)

        lo16 = lax.fori_loop(0, 16, stage16, jnp.full((1, tq), -32768, I32))

        def stage32(i, carry):
            lo, _ = carry
            cand = lo + (jnp.int32(1) << (15 - i))
            cnt = count32(lambda k: k >= cand)
            ok = cnt >= topk
            return jnp.where(ok, cand, lo), jnp.where(ok, cnt, carry[1])

        thr, n_ge = lax.fori_loop(0, 16, stage32, (lo16 << 16, jnp.full((1, tq), nks * tq, I32)))
        thr_scr[...] = jnp.broadcast_to(thr, thr_scr.shape)
        nge_scr[...] = jnp.broadcast_to(n_ge, nge_scr.shape)

    for jj in range(key_scr.shape[0]):
        @pl.when(j == jj)
        def _(jj=jj):
            if (jj + 1) * tq <= topk:
                thr_scr[...] = jnp.full(thr_scr.shape, int_min, I32)
                nge_scr[...] = jnp.full(nge_scr.shape, topk, I32)
            else:
                select_threshold(jj + 1)

    thr = thr_scr[0:1, :]
    n_ge = nge_scr[0:1, :]

    def mask_plain():
        def body(kt, carry):
            bias_scr[kt] = jnp.where(key_scr[kt] >= thr, 0.0, NEG_BIG)
            return carry
        lax.fori_loop(0, nk, body, 0)

    def mask_ties():
        def count_gt(kt, c):
            ind = jnp.where(key_scr[kt] > thr, 1.0, 0.0)
            return c + jnp.sum(ind, axis=0, keepdims=True)
        n_gt = lax.fori_loop(0, nk, count_gt, jnp.zeros((1, tq), F32))
        need = topk - n_gt
        tri = _mx(jnp.where(srow >= tcol, 1.0, 0.0))

        def body(kt, off):
            k = key_scr[kt]
            eq = k == thr
            eqf = jnp.where(eq, 1.0, 0.0)
            rank = jnp.dot(tri, _mx(eqf), preferred_element_type=F32) + off
            tie_bias = jnp.where(rank <= need, 0.0, NEG_BIG)
            bias_scr[kt] = jnp.where(k > thr, 0.0, jnp.where(eq, tie_bias, NEG_BIG))
            return off + jnp.sum(eqf, axis=0, keepdims=True)
        lax.fori_loop(0, nk, body, jnp.zeros((1, tq), F32))

    no_excess_ties = jnp.max(jnp.abs(n_ge - topk)) == 0
    lax.cond(no_excess_ties, mask_plain, mask_ties)
    bias_scr[j] = jnp.where(causal, bias_scr[j], NEG_BIG)

    acc_scr[...] = jnp.zeros(acc_scr.shape, F32)
    q = qT_ref[0]
    zeros64 = jnp.zeros((64, tq), q.dtype)
    qwide = []
    for g in range(ATT_KV_HEADS):
        cols = []
        for hh in range(ATT_GROUP):
            hd = g * ATT_GROUP + hh
            qh = q[hd * 64:(hd + 1) * 64]
            cols.append(jnp.concatenate([qh, zeros64] if g == 0 else [zeros64, qh], axis=0))
        qwide.append(jnp.concatenate(cols, axis=1))

    ones_rows = jnp.ones((16, tq), q.dtype)

    def attend(kt, carry):
        ms, ls = carry
        kb = knat_ref[kt][:, 0:128]
        vtb = vT_ref[kt]
        bias = bias_scr[kt]
        lts = [jnp.dot(kb, qwide[g], preferred_element_type=F32) for g in range(ATT_KV_HEADS)]
        new_m, new_l = [], []
        for g in range(ATT_KV_HEADS):
            lt = jnp.concatenate([lts[g][:, hh * tq:(hh + 1) * tq] + bias for hh in range(ATT_GROUP)], axis=1)
            m_new = jnp.maximum(ms[g], jnp.max(lt, axis=0, keepdims=True))
            alpha = jnp.exp2(ms[g] - m_new)
            p = _mx(jnp.exp2(lt - m_new))
            v_aug = jnp.concatenate([vtb[g * 64:(g + 1) * 64], ones_rows], axis=0)
            pv = jnp.dot(v_aug, p, preferred_element_type=F32)
            new_l.append(alpha * ls[g] + pv[HEAD_DIM:HEAD_DIM + 1, :])
            acc_scr[g] = alpha * acc_scr[g] + pv[0:HEAD_DIM, :]
            new_m.append(m_new)
        return tuple(new_m), tuple(new_l)

    wide = ATT_GROUP * tq
    init = (tuple(jnp.full((1, wide), NEG_BIG, F32) for _ in range(ATT_KV_HEADS)),
            tuple(jnp.zeros((1, wide), F32) for _ in range(ATT_KV_HEADS)))
    _, ls = lax.fori_loop(0, nk, attend, init)
    outs = []
    for g in range(ATT_KV_HEADS):
        og = acc_scr[g] / ls[g]
        outs += [og[:, hh * tq:(hh + 1) * tq] for hh in range(ATT_GROUP)]
    o_ref[...] = jnp.concatenate(outs, axis=0).T.astype(o_ref.dtype)


def _dsa(qiT3, wgT3, knat3, qT3, vT3, batch, seq):
    tq = DSA_TILE
    nq = seq // tq
    n = batch * seq
    topk = min(INDEX_TOPK_MAX, seq // 4)
    qtile = lambda r: pl.BlockSpec((1, r, tq), lambda b, j: (b * nq + j, 0, 0))
    return pl.pallas_call(
        functools.partial(_dsa_kernel, topk=topk),
        grid=(batch, nq),
        in_specs=[qtile(512), qtile(16),
                  pl.BlockSpec((nq, tq, 256), lambda b, j: (b, 0, 0)),
                  qtile(512),
                  pl.BlockSpec((nq, 128, tq), lambda b, j: (b, 0, 0))],
        out_specs=pl.BlockSpec((tq, 512), lambda b, j: (b * nq + j, 0)),
        out_shape=jax.ShapeDtypeStruct((n, 512), MXU_DTYPE),
        scratch_shapes=[pltpu.VMEM((nq, tq, tq), I32), pltpu.VMEM((nq, tq, tq), jnp.bfloat16),
                        pltpu.VMEM((nq, tq, tq), F32),
                        pltpu.VMEM((ATT_KV_HEADS, HEAD_DIM, ATT_GROUP * tq), F32),
                        pltpu.VMEM((8, tq), I32), pltpu.VMEM((8, tq), I32)],
        compiler_params=_params("parallel", "arbitrary"),
        name="dsa",
    )(qiT3, wgT3, knat3, qT3, vT3)


def _hgrn_kernel(hg_ref, lb_ref, gn_ref, o_ref, st_scr, q_scr, k_scr, g_scr, qg_scr, kd_scr, o_scr, z_scr,
                 snap_scr):
    rows = hg_ref.shape[0]
    c = HG_SUB
    w = HG_WIDTH

    @pl.when(pl.program_id(1) == 0)
    def _():
        st_scr[...] = jnp.zeros(st_scr.shape, F32)
        z_scr[...] = jnp.zeros(z_scr.shape, z_scr.dtype)

    lb = lb_ref[...]
    q = _silu(hg_ref[:, 0:w]) * (HEAD_DIM ** -0.5)
    forget = lb + (1.0 - lb) * _sigmoid(hg_ref[:, w:2 * w])
    logf = jnp.log2(forget)
    kk = 1.0 - forget
    v = hg_ref[:, 2 * w:3 * w]

    ri = _iota((rows, rows), 0)
    ci = _iota((rows, rows), 1)
    same = (ri // c) == (ci // c)
    tri = _mx(jnp.where(same & (ci <= ri), 1.0, 0.0))
    ones_blk = _mx(jnp.where(same, 1.0, 0.0))
    gl = _dot01_left(tri, logf)
    glast = _dot01_left(ones_blk, logf)
    q_scr[...] = q
    k_scr[...] = gl - jnp.log2(kk)
    g_scr[...] = gl
    qg_scr[...] = q * jnp.exp2(gl)
    kd_scr[...] = kk * jnp.exp2(glast - gl)
    dec = jnp.exp2(glast)
    vT = _mx(v.T)
    head_ones = _mx(_block_ones(w, HEAD_DIM))
    pair_mask = _block_ones(LANE, HEAD_DIM)
    half = c // 2
    rhalf = _iota((half, w), 0)
    nsc = rows // c

    for sc in range(nsc):
        z_scr[sc, sc * c:(sc + 1) * c, :] = kd_scr[sc * c:(sc + 1) * c, :].astype(z_scr.dtype)
    for p in range(HG_HEADS // 2):
        ps = slice(p * LANE, (p + 1) * LANE)
        st = st_scr[p]
        for sc in range(nsc):
            snap_scr[sc, p] = st.astype(snap_scr.dtype)
            upd = jnp.dot(vT[ps, :], z_scr[sc, :, ps], preferred_element_type=F32)
            st = st * dec[sc * c:sc * c + 1, ps] + upd * pair_mask
        st_scr[p] = st

    for sc in range(nsc):
        r0 = sc * c
        q_top, q_bot = q_scr[r0:r0 + half, :], q_scr[r0 + half:r0 + c, :]
        g_top, g_bot = g_scr[r0:r0 + half, :], g_scr[r0 + half:r0 + c, :]
        bcast = lambda row: jnp.broadcast_to(row, (half, w))
        xs_top, xs_bot = [], []
        for s in range(c):
            hsr = bcast(k_scr[r0 + s:r0 + s + 1, :])
            if s < half:
                xs_top.append(q_top * jnp.exp2(jnp.where(rhalf >= s, g_top - hsr, -jnp.inf)))
                xs_bot.append(q_bot * jnp.exp2(g_bot - hsr))
            else:
                xs_bot.append(q_bot * jnp.exp2(jnp.where(rhalf >= s - half, g_bot - hsr, -jnp.inf)))
        y = jnp.dot(_mx(jnp.concatenate(xs_top + xs_bot, axis=0)), head_ones, preferred_element_type=F32)
        o_top = jnp.zeros((half, w), F32)
        o_bot = jnp.zeros((half, w), F32)
        for s in range(c):
            vsr = bcast(hg_ref[r0 + s:r0 + s + 1, 2 * w:3 * w])
            if s < half:
                o_top = o_top + y[s * half:(s + 1) * half, :] * vsr
            o_bot = o_bot + y[(half + s) * half:(half + s + 1) * half, :] * vsr
        o_diag = jnp.concatenate([o_top, o_bot], axis=0)
        o_pairs = [_dot_nt(qg_scr[r0:r0 + c, p * LANE:(p + 1) * LANE], snap_scr[sc, p])
                   for p in range(HG_HEADS // 2)]
        o_scr[r0:r0 + c, :] = jnp.concatenate(o_pairs, axis=1) + o_diag

    o = o_scr[...]
    ms = _dot01_right(o * o, head_ones) * (1.0 / HEAD_DIM)
    y = o * lax.rsqrt(ms + RMS_EPS) * gn_ref[...]
    o_ref[...] = (y * _silu(hg_ref[:, 3 * w:4 * w])).astype(o_ref.dtype)


def _hgrn(hg, lb, gn, batch, seq):
    rows = HG_ROWS
    ns = seq // rows
    n = batch * seq
    vm = lambda: pltpu.VMEM((rows, HG_WIDTH), F32)
    return pl.pallas_call(
        _hgrn_kernel,
        grid=(batch, ns),
        in_specs=[pl.BlockSpec((rows, 1024), lambda b, s: (b * ns + s, 0)),
                  _const_spec((1, HG_WIDTH)), _const_spec((1, HG_WIDTH))],
        out_specs=pl.BlockSpec((rows, HG_WIDTH), lambda b, s: (b * ns + s, 0)),
        out_shape=jax.ShapeDtypeStruct((n, HG_WIDTH), MXU_DTYPE),
        scratch_shapes=[pltpu.VMEM((HG_HEADS // 2, LANE, LANE), F32), vm(), vm(), vm(), vm(), vm(), vm(),
                        pltpu.VMEM((rows // HG_SUB, rows, HG_WIDTH), MXU_DTYPE),
                        pltpu.VMEM((rows // HG_SUB, HG_HEADS // 2, LANE, LANE), MXU_DTYPE)],
        compiler_params=_params("parallel", "arbitrary"),
        name="hgrn",
    )(hg, lb, gn)


def _mlstm_kernel(ml_ref, *refs):
    gT_refs = refs[:ML_SEQS]
    cw_ref, cb_ref, brow_ref, gn_ref, o_ref, prev_scr, c_scr, n_scr, m64_scr, m128_scr = refs[ML_SEQS:]

    @pl.when(pl.program_id(1) == 0)
    def _():
        prev_scr[...] = jnp.zeros(prev_scr.shape, F32)
        c_scr[...] = jnp.zeros(c_scr.shape, F32)
        n_scr[...] = jnp.zeros(n_scr.shape, F32)
        m64_scr[...] = jnp.full(m64_scr.shape, M_INIT, F32)
        m128_scr[...] = jnp.full(m128_scr.shape, M_INIT, F32)

    seqs = [_mlstm_sequence(ml_ref.at[bi], gT_ref, cw_ref, cb_ref, brow_ref, gn_ref, o_ref.at[bi],
                            prev_scr.at[bi], c_scr.at[bi], n_scr.at[bi], m64_scr.at[bi], m128_scr.at[bi])
            for bi, gT_ref in enumerate(gT_refs)]
    while seqs:
        for seq in list(seqs):
            if next(seq, "done") == "done":
                seqs.remove(seq)


def _mlstm_sequence(ml_ref, gT_ref, cw_ref, cb_ref, brow_ref, gn_ref, o_ref,
                    prev_scr, c_scr, n_scr, m64_scr, m128_scr):
    L = ML_CHUNK
    w = ML_WIDTH

    xqk = ml_ref[:, 0:512]
    prev = prev_scr[...]
    rows = _iota((L, 512), 0)
    acc = cb_ref[...] + cw_ref[ML_CONV - 1:ML_CONV, :] * xqk
    for sh in range(1, ML_CONV):
        shifted = jnp.where(rows >= sh, pltpu.roll(xqk, sh, 0), pltpu.roll(prev, sh, 0))
        acc = acc + cw_ref[ML_CONV - 1 - sh:ML_CONV - sh, :] * shifted
    prev_scr[...] = xqk
    qk = _silu(acc)
    q = qk[:, 0:w]
    k = qk[:, w:2 * w] * (HEAD_DIM ** -0.5)
    v = ml_ref[:, 512:768]

    srow = _iota((L, L), 0)
    scol = _iota((L, L), 1)
    tril = srow >= scol
    triu = _mx(jnp.where(srow <= scol, 1.0, 0.0))
    grow = gT_ref[0] + brow_ref[...]
    brow = _dot01_right(_log_sigmoid(grow), triu)
    yield

    def expand(x, width, row0):
        src = _iota((16, ML_HEADS * width), 0)
        dst = _iota((16, ML_HEADS * width), 1)
        e01 = _mx(jnp.where(src == row0 + dst // width, 1.0, 0.0))
        f = lambda p: lax.dot_general(p, e01, (((0,), (0,)), ((), ())), preferred_element_type=F32)
        hi, mid, lo = _split3(x)
        return f(hi) + f(mid) + f(lo)

    b128 = expand(brow, 128, 12)
    i128 = expand(grow, 128, 8)
    b64 = expand(brow, 64, 12)
    i64 = expand(grow, 64, 8)
    yield

    lane = _iota((L, LANE), 1)
    first = lane < HEAD_DIM
    pair_ones = _mx(_block_ones(LANE, HEAD_DIM))
    pair_mask = _block_ones(LANE, HEAD_DIM)
    m_prev128 = m128_scr[0:1, :]
    m_prev64 = m64_scr[0:1, :]
    n_row = n_scr[0:1, :]

    houts = []
    for p in range(ML_HEADS // 2):
        ps = slice(p * LANE, (p + 1) * LANE)
        q_pair, k_pair, v_pair = q[:, ps], k[:, ps], v[:, ps]
        per_head = []
        for hh in range(2):
            hd = 2 * p + hh
            hs = slice(hd * 128, (hd + 1) * 128)
            bt = b128[:, hs]
            d = bt - brow[12 + hd:13 + hd, :] + grow[8 + hd:9 + hd, :]
            d = jnp.where(tril, d, -jnp.inf)
            inter = bt + m_prev128[:, hs]
            m_t = jnp.maximum(inter, jnp.max(d, axis=1, keepdims=True))
            wgt = jnp.exp(d - m_t)
            a = jnp.exp(inter - m_t)
            qm = jnp.where(first if hh == 0 else ~first, q_pair, 0.0)
            s = _dot_nt(qm, k_pair) * wgt
            sv = _dot(s, v_pair)
            rs = jnp.sum(s, axis=1, keepdims=True)
            per_head.append((m_t, a, sv, rs))
            yield
        sel = lambda i: jnp.where(first, per_head[0][i], per_head[1][i])
        m_t, a, sv, rs = sel(0), sel(1), sel(2), sel(3)
        qc = _dot(q_pair, c_scr[p])
        qn = _dot(q_pair * n_row[:, ps], pair_ones)
        num = a * qc + sv
        den = a * qn + rs
        houts.append(num / jnp.maximum(jnp.abs(den), jnp.exp(-m_t)))
        yield
    hout = jnp.concatenate(houts, axis=1)

    def new_m(bx, ix, m_prev):
        blast = bx[L - 1:L, :]
        wl = blast - bx + ix
        m_new = jnp.maximum(blast + m_prev, jnp.max(wl, axis=0, keepdims=True))
        return blast, wl, m_new

    blast, wl, m_new64 = new_m(b64, i64, m_prev64)
    dec = jnp.exp(blast + m_prev64 - m_new64)
    kw = k * jnp.exp(wl - m_new64)
    for p in range(ML_HEADS // 2):
        ps = slice(p * LANE, (p + 1) * LANE)
        upd = jnp.dot(_mx(kw[:, ps].T), _mx(v[:, ps]), preferred_element_type=F32)
        c_scr[p] = c_scr[p] * dec[:, ps] + upd * pair_mask
    n_scr[...] = jnp.broadcast_to(dec * n_row + jnp.sum(kw, axis=0, keepdims=True), n_scr.shape)
    m64_scr[...] = jnp.broadcast_to(m_new64, m64_scr.shape)
    _, _, m_new128 = new_m(b128, i128, m_prev128)
    m128_scr[...] = jnp.broadcast_to(m_new128, m128_scr.shape)

    head_ones = _mx(_block_ones(w, HEAD_DIM))
    ms = _dot01_right(hout * hout, head_ones) * (1.0 / HEAD_DIM)
    y = hout * lax.rsqrt(ms + RMS_EPS) * gn_ref[...]
    o_ref[...] = (y * _sigmoid(ml_ref[:, 768:1024])).astype(o_ref.dtype)


def _mlstm(ml, wgT3, cw, cb, brow, gn, batch, seq):
    L = ML_CHUNK
    nc = seq // L
    n = batch * seq
    per_tile = DSA_TILE // L
    nb = ML_SEQS
    assert batch % nb == 0

    def gate_spec(off):
        def index(b, s):
            chunk = (nb * b + off) * nc + s
            return chunk // per_tile, 0, chunk % per_tile
        return pl.BlockSpec((1, 16, L), index)

    out = pl.pallas_call(
        _mlstm_kernel,
        grid=(batch // nb, nc),
        in_specs=[pl.BlockSpec((nb, L, 1024), lambda b, s: (b, s, 0))] + [gate_spec(o) for o in range(nb)]
                 + [_const_spec((ML_CONV, 512)), _const_spec((1, 512)),
                  _const_spec((16, L)), _const_spec((1, ML_WIDTH))],
        out_specs=pl.BlockSpec((nb, L, ML_WIDTH), lambda b, s: (b, s, 0)),
        out_shape=jax.ShapeDtypeStruct((batch, seq, ML_WIDTH), MXU_DTYPE),
        scratch_shapes=[pltpu.VMEM((nb, L, 512), F32), pltpu.VMEM((nb, 2, LANE, LANE), F32),
                        pltpu.VMEM((nb, 8, ML_WIDTH), F32), pltpu.VMEM((nb, 8, ML_WIDTH), F32),
                        pltpu.VMEM((nb, 8, 512), F32)],
        compiler_params=_params("parallel", "arbitrary"),
        name="mlstm",
    )(ml.reshape(batch, seq, 1024), *([wgT3] * nb), cw, cb, brow, gn)
    return out.reshape(n, ML_WIDTH)


def _rms(y, g):
    ms = jnp.mean(y * y, axis=-1, keepdims=True)
    return y * lax.rsqrt(ms + RMS_EPS) * g


def _mem_kv_kernel(m_ref, g_ref, w_ref, k_ref, v_ref):
    h = _mx(_rms(m_ref[...], g_ref[...]))
    kv = jnp.dot(h, w_ref[...], preferred_element_type=F32)
    k_ref[...] = kv[:, 0:D_MODEL].astype(k_ref.dtype)
    v_ref[...] = kv[:, D_MODEL:2 * D_MODEL].astype(v_ref.dtype)


def _mem_kv(mem2, g, w):
    n = mem2.shape[0]
    tm = ROW_TILE
    row = pl.BlockSpec((tm, D_MODEL), lambda i: (i, 0))
    return pl.pallas_call(
        _mem_kv_kernel,
        grid=(n // tm,),
        in_specs=[row, _const_spec((1, D_MODEL)), _const_spec((D_MODEL, 2 * D_MODEL))],
        out_specs=[row, row],
        out_shape=[jax.ShapeDtypeStruct((n, D_MODEL), MXU_DTYPE)] * 2,
        compiler_params=_params("parallel"),
        name="mem_kv",
    )(mem2, g, w)


def _post_mix_kernel(a_ref, b_ref, c_ref, x_ref, k_ref, v_ref, wout_ref, wq_ref, wo_ref, wu_ref, wd_ref,
                     g_ref, o_ref):
    g = g_ref[...]
    y = (jnp.dot(a_ref[...], wout_ref[0:512, :], preferred_element_type=F32)
         + jnp.dot(b_ref[...], wout_ref[512:768, :], preferred_element_type=F32)
         + jnp.dot(c_ref[...], wout_ref[768:1024, :], preferred_element_type=F32))
    x = x_ref[...] + _rms(y, g[0:1])
    h = _mx(_rms(x, g[1:2]))
    q = jnp.dot(h, wq_ref[...], preferred_element_type=F32) * (CROSS_HEAD_DIM ** -0.5)
    outs = []
    for hd in range(CROSS_HEADS):
        hs = slice(hd * CROSS_HEAD_DIM, (hd + 1) * CROSS_HEAD_DIM)
        logits = _dot_nt(q[:, hs], k_ref[:, hs])
        p = jnp.exp(logits - jnp.max(logits, axis=-1, keepdims=True))
        o = jnp.dot(_mx(p), v_ref[:, hs], preferred_element_type=F32)
        outs.append(o / jnp.sum(p, axis=-1, keepdims=True))
    y = jnp.dot(_mx(jnp.concatenate(outs, axis=1)), wo_ref[...], preferred_element_type=F32)
    x = x + _rms(y, g[2:3])
    h = _mx(_rms(x, g[3:4]))
    y = jnp.zeros(x.shape, F32)
    for c in range(MLP_HIDDEN // D_MODEL):
        cs = slice(c * D_MODEL, (c + 1) * D_MODEL)
        u = jnp.maximum(jnp.dot(h, wu_ref[:, cs], preferred_element_type=F32), 0.0)
        y = y + jnp.dot(_mx(u * u), wd_ref[cs, :], preferred_element_type=F32)
    o_ref[...] = x + _rms(y, g[4:5])


def _post_mix(a, b, c, x2, kmem, vmem, wout, wq, wo, wu, wd, gains, seq, n_mem):
    n = x2.shape[0]
    tm = POST_TILE
    per_b = seq // tm
    row = lambda width: pl.BlockSpec((tm, width), lambda i: (i, 0))
    memb =pl.BlockSpec((n_mem, D_MODEL), lambda i: (i // per_b, 0))
    sq = _const_spec((D_MODEL, D_MODEL))
    return pl.pallas_call(
        _post_mix_kernel,
        grid=(n // tm,),
        in_specs=[row(512), row(256), row(256), row(D_MODEL), memb, memb, sq, sq, sq,
                  _const_spec((D_MODEL, MLP_HIDDEN)), _const_spec((MLP_HIDDEN, D_MODEL)),
                  _const_spec((8, D_MODEL))],
        out_specs=row(D_MODEL),
        out_shape=jax.ShapeDtypeStruct((n, D_MODEL), F32),
        compiler_params=_params("parallel"),
        name="post_mix",
    )(a, b, c, x2, kmem, vmem, wout, wq, wo, wu, wd, gains)


def _rope_tables(seq):
    half = HEAD_DIM // 2
    inv = ROPE_THETA ** (-jnp.arange(0, HEAD_DIM, 2, dtype=F32) / HEAD_DIM)
    ang = jnp.arange(seq).astype(F32)[:, None] * inv[None, :]
    cos, sin = jnp.cos(ang), jnp.sin(ang)
    lane = jnp.arange(LANE)
    freq = (lane % HEAD_DIM) % half
    first = (lane % HEAD_DIM) < half
    cosn = cos[:, freq]
    sinn = sin[:, freq]
    sina = jnp.where(first[None, :], -sinn, 0.0)
    sinb = jnp.where(first[None, :], 0.0, sinn)
    return cosn, sina, sinb, cos.T, sin.T


def _layout_w_in(w_in):
    depth = w_in.shape[0]
    z = lambda wd: jnp.zeros((depth, D_MODEL, wd), w_in.dtype)
    col = lambda a, b: w_in[:, :, a:b]
    wn = jnp.concatenate([col(_O_AK, _O_AV), col(_O_XK, _O_XW), z(64),
                          col(_O_HQ, _O_MQK), col(_O_MQK, _O_MG)], axis=-1)
    wt = jnp.concatenate([col(_O_AQ, _O_AK), col(_O_XQ, _O_XK), col(_O_AV, _O_XQ),
                          col(_O_XW, _O_HQ), col(_O_MG, _IN_WIDTH)], axis=-1)
    return _mx(wn), _mx(jnp.swapaxes(wt, 1, 2))


def kernel(x, mem, mix_pre_g, w_in, ml_conv_w, ml_conv_b, ml_gate_b, hg_lb, hg_norm_g, ml_norm_g,
           w_out, mix_post_g, cross_pre_g, mem_norm_g, w_cq, w_ckv, w_co, cross_post_g, mlp_pre_g,
           w_up, w_down, mlp_post_g):
    batch, seq, _ = x.shape
    n_mem = mem.shape[1]
    depth = w_in.shape[0]
    n = batch * seq
    assert seq % max(ROW_TILE, POST_TILE, HG_ROWS, ML_CHUNK, DSA_TILE) == 0
    assert (batch * n_mem) % ROW_TILE == 0
    assert ROW_TILE % DSA_TILE == 0 and DSA_TILE % ML_CHUNK == 0

    lbs = jnp.cumsum(jax.nn.softmax(hg_lb.astype(F32), axis=0), axis=0)
    lbs = lbs - lbs[0:1]

    tabs = _rope_tables(seq)
    wn_all, wt_all = _layout_w_in(w_in)
    row = lambda a: a.astype(F32)[None, :]
    brow =jnp.pad(ml_gate_b.astype(F32), ((0, 0), (8, 0)))

    x2 = x.reshape(n, D_MODEL)
    mem2 = mem.reshape(batch * n_mem, D_MODEL)
    nl = n // DSA_TILE
    for l in range(depth):
        knat, hg, ml, qT3, qiT3, vT3, wgT3 = _in_proj(x2, row(mix_pre_g[l]), wn_all[l], wt_all[l], tabs, seq)
        a_out = _dsa(qiT3, wgT3, knat.reshape(nl, DSA_TILE, 256), qT3, vT3, batch, seq)
        b_out = _hgrn(hg, row(lbs[l]), row(jnp.tile(hg_norm_g[l], HG_HEADS)), batch, seq)
        c_out = _mlstm(ml, wgT3, ml_conv_w[l].astype(F32), row(ml_conv_b[l]),
                       jnp.broadcast_to(brow[l][:, None], (16, ML_CHUNK)),
                       row(jnp.tile(ml_norm_g[l], ML_HEADS)), batch, seq)
        kmem, vmem = _mem_kv(mem2, row(mem_norm_g[l]), _mx(w_ckv[l]))
        gains = jnp.stack([mix_post_g[l], cross_pre_g[l], cross_post_g[l], mlp_pre_g[l], mlp_post_g[l]]
                          + [jnp.zeros_like(mix_post_g[l])] * 3).astype(F32)
        x2 = _post_mix(a_out, b_out, c_out, x2, kmem, vmem, _mx(w_out[l]), _mx(w_cq[l]), _mx(w_co[l]),
                       _mx(w_up[l]), _mx(w_down[l]), gains, seq, n_mem)
    return x2.reshape(batch, seq, D_MODEL)
```

```python
import functools

import jax
import jax.numpy as jnp
from jax import lax
from jax.experimental import pallas as pl
from jax.experimental.pallas import tpu as pltpu

F32 = jnp.float32
I32 = jnp.int32
MXU_DTYPE = jnp.bfloat16

D_MODEL = 1024
RMS_EPS = 1e-6
ROPE_THETA = 10000.0
NEG_BIG = -1e30
M_INIT = -1e30
LOG2E = 1.4426950408889634
F32_MIN_NORMAL = 1.1754943508222875e-38
BOUND_MARGIN = 1.05
SAFE_LOGIT_BOUND = 60.0
HEAD_DIM = 64
ATT_HEADS = 8
ATT_KV_HEADS = 2
ATT_GROUP = ATT_HEADS // ATT_KV_HEADS
IDX_HEADS = 8
INDEX_TOPK_MAX = 256
HG_HEADS = 4
HG_WIDTH = HG_HEADS * HEAD_DIM
ML_HEADS = 4
ML_WIDTH = ML_HEADS * HEAD_DIM
ML_CONV = 4
CROSS_HEADS = 4
CROSS_HEAD_DIM = D_MODEL // CROSS_HEADS
MLP_HIDDEN = 4 * D_MODEL

_O_AQ, _O_AK, _O_AV, _O_XQ, _O_XK, _O_XW = 0, 512, 640, 768, 1280, 1344
_O_HQ, _O_MQK, _O_MV, _O_MO, _O_MG, _IN_WIDTH = 1352, 2376, 2888, 3144, 3400, 3408

NAT_WIDTH = 128 + 128 + 1024 + 1024
TR_WIDTH = 512 + 512 + 128 + 16

LANE = 128
DSA_TILE = 256
ROW_TILE = 256
POST_TILE = 512
HG_SUB = 16
HG_ROWS = 256
ML_CHUNK = 128
ML_SEQS = 4
VMEM_LIMIT = 56 * 1024 * 1024

_NT = (((1,), (1,)), ((), ()))


def _mx(a):
    return a.astype(MXU_DTYPE)


def _dot(a, b):
    return jnp.dot(_mx(a), _mx(b), preferred_element_type=F32)


def _dot_nt(a, b):
    return lax.dot_general(_mx(a), _mx(b), _NT, preferred_element_type=F32)


def _split3(x):
    hi = _mx(x)
    r1 = x - hi.astype(F32)
    mid = _mx(r1)
    lo = _mx(r1 - mid.astype(F32))
    return hi, mid, lo


def _dot01_left(m01, x):
    hi, mid, lo = _split3(x)
    f = lambda p: jnp.dot(m01, p, preferred_element_type=F32)
    return f(hi) + f(mid) + f(lo)


def _dot01_right(x, m01):
    hi, mid, lo = _split3(x)
    f = lambda p: jnp.dot(p, m01, preferred_element_type=F32)
    return f(hi) + f(mid) + f(lo)


def _sigmoid(x):
    return 1.0 / (1.0 + jnp.exp(-x))


def _silu(x):
    return x * _sigmoid(x)


def _log_sigmoid(x):
    return jnp.minimum(x, 0.0) - jnp.log(1.0 + jnp.exp(-jnp.abs(x)))


def _iota(shape, dim):
    return lax.broadcasted_iota(I32, shape, dim)


def _block_ones(n, blk):
    same = (_iota((n, n), 0) // blk) == (_iota((n, n), 1) // blk)
    return jnp.where(same, 1.0, 0.0)


def _params(*sem):
    return pltpu.CompilerParams(dimension_semantics=sem, vmem_limit_bytes=VMEM_LIMIT)


def _const_spec(shape):
    nd = len(shape)
    return pl.BlockSpec(shape, lambda *_: (0,) * nd, pipeline_mode=pl.Buffered(1))


def _in_proj_kernel(x_ref, g_ref, wn_ref, wt_ref, cosn_ref, sina_ref, sinb_ref, cost_ref, sint_ref,
                    knat_ref, hg_ref, ml_ref, qT_ref, qiT_ref, vT_ref, wgT_ref, *, idx_scale):
    tm = x_ref.shape[0]
    x = x_ref[...]
    ms = jnp.mean(x * x, axis=-1, keepdims=True)
    h = _mx(x * lax.rsqrt(ms + RMS_EPS) * g_ref[...])
    nat = jnp.dot(h, wn_ref[...], preferred_element_type=F32)
    tr = lax.dot_general(wt_ref[...], h, _NT, preferred_element_type=F32)

    cosn, sina, sinb = cosn_ref[...], sina_ref[...], sinb_ref[...]

    def rope_nat(z):
        return z * cosn + pltpu.roll(z, 96, 1) * sina + pltpu.roll(z, 32, 1) * sinb

    knat_ref[:, 0:128] = rope_nat(nat[:, 0:128]).astype(knat_ref.dtype)
    knat_ref[:, 128:256] = rope_nat(nat[:, 128:256]).astype(knat_ref.dtype)
    hg_ref[...] = nat[:, 256:1280]
    ml_ref[...] = nat[:, 1280:NAT_WIDTH]

    cost, sint = cost_ref[...], sint_ref[...]

    def rope_t(z, scale):
        outs = []
        for hd in range(8):
            x1 = z[hd * 64:hd * 64 + 32]
            x2 = z[hd * 64 + 32:hd * 64 + 64]
            outs.append((x1 * cost - x2 * sint) * scale)
            outs.append((x2 * cost + x1 * sint) * scale)
        return jnp.concatenate(outs, axis=0)

    qT = rope_t(tr[0:512], (HEAD_DIM ** -0.5) * LOG2E).astype(qT_ref.dtype)
    qiT = rope_t(tr[512:1024], 1.0).astype(qiT_ref.dtype)
    vT = tr[1024:1152].astype(vT_ref.dtype)
    wg = tr[1152:1168]
    wg = jnp.where(_iota(wg.shape, 0) < 8, wg * idx_scale, wg)
    for c in range(tm // DSA_TILE):
        sl = slice(c * DSA_TILE, (c + 1) * DSA_TILE)
        qT_ref[c] = qT[:, sl]
        qiT_ref[c] = qiT[:, sl]
        vT_ref[c] = vT[:, sl]
        wgT_ref[c] = wg[:, sl]


def _in_proj(x2, g, wn, wt, tabs, seq):
    n = x2.shape[0]
    tm = ROW_TILE
    nt_seq = seq // tm
    cosn, sina, sinb, cost, sint = tabs
    idx_scale = (IDX_HEADS ** -0.5) * (HEAD_DIM ** -0.5)
    row = lambda w: pl.BlockSpec((tm, w), lambda i: (i, 0))
    tabn = pl.BlockSpec((tm, LANE), lambda i: (i % nt_seq, 0))
    tabt = pl.BlockSpec((32, tm), lambda i: (0, i % nt_seq))
    t3 = lambda r: pl.BlockSpec((tm // DSA_TILE, r, DSA_TILE), lambda i: (i, 0, 0))
    nl = n // DSA_TILE
    return pl.pallas_call(
        functools.partial(_in_proj_kernel, idx_scale=idx_scale),
        grid=(n // tm,),
        in_specs=[row(D_MODEL), _const_spec((1, D_MODEL)), _const_spec((D_MODEL, NAT_WIDTH)),
                  _const_spec((TR_WIDTH, D_MODEL)), tabn, tabn, tabn, tabt, tabt],
        out_specs=[row(256), row(1024), row(1024), t3(512), t3(512), t3(128), t3(16)],
        out_shape=[jax.ShapeDtypeStruct((n, 256), MXU_DTYPE),
                   jax.ShapeDtypeStruct((n, 1024), F32),
                   jax.ShapeDtypeStruct((n, 1024), F32),
                   jax.ShapeDtypeStruct((nl, 512, DSA_TILE), MXU_DTYPE),
                   jax.ShapeDtypeStruct((nl, 512, DSA_TILE), MXU_DTYPE),
                   jax.ShapeDtypeStruct((nl, 128, DSA_TILE), MXU_DTYPE),
                   jax.ShapeDtypeStruct((nl, 16, DSA_TILE), F32)],
        compiler_params=_params("parallel"),
        name="in_proj",
    )(x2, g, wn, wt, cosn, sina, sinb, cost, sint)


def _dsa_kernel(qiT_ref, wgT_ref, knat_ref, qT_ref, vT_ref, o_ref,
                key_scr, hi_scr, bias_scr, acc_scr, thr_scr, nge_scr, kmax_scr, *, topk):
    tq = DSA_TILE
    j = pl.program_id(1)
    nk = j + 1
    int_min = jnp.int32(-2 ** 31)
    srow = _iota((tq, tq), 0)
    tcol = _iota((tq, tq), 1)
    causal = srow <= tcol

    w = wgT_ref[0]
    qi = qiT_ref[0]
    qi_wide = jnp.concatenate([qi[hd * 64:(hd + 1) * 64] for hd in range(IDX_HEADS)], axis=1)

    def score_tile(kt, carry):
        kblk = knat_ref[kt][:, 128:192]
        acc = jnp.zeros((tq, tq), F32)
        ys = jnp.dot(kblk, qi_wide, preferred_element_type=F32)
        for hd in range(IDX_HEADS):
            acc = acc + w[hd:hd + 1, :] * jnp.maximum(ys[:, hd * tq:(hd + 1) * tq], 0.0)
        acc = jnp.where(jnp.abs(acc) < F32_MIN_NORMAL, 0.0, acc)
        acc = jnp.where(jnp.logical_or(kt < j, causal), acc, -jnp.inf)
        bits = pltpu.bitcast(acc, I32)
        key_scr[kt] = bits ^ ((bits >> 31) & jnp.int32(0x7FFFFFFF))
        hi_scr[kt] = pltpu.bitcast(bits & jnp.int32(-65536), F32).astype(jnp.bfloat16)
        return carry

    lax.fori_loop(0, nk, score_tile, 0)

    key16_neg_inf = (0xFF80 ^ 0x7FFF) - 65536
    one_h = jnp.ones((16, tq), jnp.bfloat16)
    zero_h = jnp.zeros((16, tq), jnp.bfloat16)

    def select_threshold(nks):
        def count16(cand16):
            b16 = cand16 ^ ((cand16 >> 15) & 0x7FFF)
            b16 = jnp.where((cand16 > 0) & (cand16 < 0x80), 0x80, b16)
            cb = jnp.broadcast_to(pltpu.bitcast(b16 << 16, F32).astype(jnp.bfloat16), (16, tq))
            accs = [zero_h] * 4
            n = 0
            for kt in range(nks):
                for r in range(tq // 16):
                    ind = jnp.where(hi_scr[kt, r * 16:(r + 1) * 16, :] >= cb, one_h, zero_h)
                    accs[n % 4] = accs[n % 4] + ind
                    n += 1
            tot = (accs[0] + accs[1]).astype(F32) + (accs[2] + accs[3]).astype(F32)
            return jnp.sum(tot, axis=0, keepdims=True)

        def count32(pred_fn):
            accs = [jnp.zeros((8, tq), I32)] * 4
            n = 0
            for kt in range(nks):
                for r in range(tq // 8):
                    ind = jnp.where(pred_fn(key_scr[kt, r * 8:(r + 1) * 8, :]), 1, 0).astype(I32)
                    accs[n % 4] = accs[n % 4] + ind
                    n += 1
            return jnp.sum((accs[0] + accs[1]) + (accs[2] + accs[3]), axis=0, keepdims=True)

        def stage16(i, lo):
            cand = lo + (jnp.int32(1) << (15 - i))
            ok = (count16(cand) >= topk) | (cand <= key16_neg_inf)
            return jnp.where(ok, cand, lo)

        lo16 = lax.fori_loop(0, 16, stage16, jnp.full((1, tq), -32768, I32))

        def stage32(i, carry):
            lo, _ = carry
            cand = lo + (jnp.int32(1) << (15 - i))
            cnt = count32(lambda k: k >= cand)
            ok = cnt >= topk
            return jnp.where(ok, cand, lo), jnp.where(ok, cnt, carry[1])

        thr, n_ge = lax.fori_loop(0, 16, stage32, (lo16 << 16, jnp.full((1, tq), nks * tq, I32)))
        thr_scr[...] = jnp.broadcast_to(thr, thr_scr.shape)
        nge_scr[...] = jnp.broadcast_to(n_ge, nge_scr.shape)

    for jj in range(key_scr.shape[0]):
        @pl.when(j == jj)
        def _(jj=jj):
            if (jj + 1) * tq <= topk:
                thr_scr[...] = jnp.full(thr_scr.shape, int_min, I32)
                nge_scr[...] = jnp.full(nge_scr.shape, topk, I32)
            else:
                select_threshold(jj + 1)

    thr = thr_scr[0:1, :]
    n_ge = nge_scr[0:1, :]

    q = qT_ref[0]

    @pl.when(j == 0)
    def _():
        group_ones = _mx(_block_ones(LANE, HEAD_DIM))

        def body(kt, m):
            k = knat_ref[kt][:, 0:128].astype(F32)
            n2 = jnp.dot(_mx(k * k), group_ones, preferred_element_type=F32)
            return jnp.maximum(m, jnp.max(n2, axis=0, keepdims=True))

        kmax2 = lax.fori_loop(0, key_scr.shape[0], body, jnp.zeros((1, LANE), F32))
        kmax_scr[...] = jnp.broadcast_to(kmax2, kmax_scr.shape)

    qf = q.astype(F32)
    bound2 = jnp.zeros((1, tq), F32)
    for hd in range(ATT_HEADS):
        g = hd // ATT_GROUP
        qn2 = jnp.sum(qf[hd * 64:(hd + 1) * 64] * qf[hd * 64:(hd + 1) * 64], axis=0, keepdims=True)
        bound2 = jnp.maximum(bound2, qn2 * kmax_scr[0:1, g * HEAD_DIM:g * HEAD_DIM + 1])
    bound = BOUND_MARGIN * jnp.sqrt(bound2)
    bound_is_small = jnp.max(bound) < SAFE_LOGIT_BOUND
    neg_bound = -bound

    def mask_plain():
        def body(kt, carry):
            bias_scr[kt] = jnp.where(key_scr[kt] >= thr, neg_bound, NEG_BIG)
            return carry
        lax.fori_loop(0, nk, body, 0)

    def mask_ties():
        def count_gt(kt, c):
            ind = jnp.where(key_scr[kt] > thr, 1.0, 0.0)
            return c + jnp.sum(ind, axis=0, keepdims=True)
        n_gt = lax.fori_loop(0, nk, count_gt, jnp.zeros((1, tq), F32))
        need = topk - n_gt
        tri = _mx(jnp.where(srow >= tcol, 1.0, 0.0))

        def body(kt, off):
            k = key_scr[kt]
            eq = k == thr
            eqf = jnp.where(eq, 1.0, 0.0)
            rank = jnp.dot(tri, _mx(eqf), preferred_element_type=F32) + off
            tie_bias = jnp.where(rank <= need, neg_bound, NEG_BIG)
            bias_scr[kt] = jnp.where(k > thr, neg_bound, jnp.where(eq, tie_bias, NEG_BIG))
            return off + jnp.sum(eqf, axis=0, keepdims=True)
        lax.fori_loop(0, nk, body, jnp.zeros((1, tq), F32))

    no_excess_ties = jnp.max(jnp.abs(n_ge - topk)) == 0
    lax.cond(no_excess_ties, mask_plain, mask_ties)
    bias_scr[j] = jnp.where(causal, bias_scr[j], NEG_BIG)

    acc_scr[...] = jnp.zeros(acc_scr.shape, F32)
    zeros64 = jnp.zeros((64, tq), q.dtype)
    qwide = []
    for g in range(ATT_KV_HEADS):
        cols = []
        for hh in range(ATT_GROUP):
            hd = g * ATT_GROUP + hh
            qh = q[hd * 64:(hd + 1) * 64]
            cols.append(jnp.concatenate([qh, zeros64] if g == 0 else [zeros64, qh], axis=0))
        qwide.append(jnp.concatenate(cols, axis=1))

    ones_rows = jnp.ones((16, tq), q.dtype)

    def attend(kt, carry):
        ms, ls = carry
        kb = knat_ref[kt][:, 0:128]
        vtb = vT_ref[kt]
        bias = bias_scr[kt]
        lts = [jnp.dot(kb, qwide[g], preferred_element_type=F32) for g in range(ATT_KV_HEADS)]
        new_m, new_l = [], []
        for g in range(ATT_KV_HEADS):
            lt = jnp.concatenate([lts[g][:, hh * tq:(hh + 1) * tq] + bias for hh in range(ATT_GROUP)], axis=1)
            m_new = jnp.maximum(ms[g], jnp.max(lt, axis=0, keepdims=True))
            alpha = jnp.exp2(ms[g] - m_new)
            p = _mx(jnp.exp2(lt - m_new))
            v_aug = jnp.concatenate([vtb[g * 64:(g + 1) * 64], ones_rows], axis=0)
            pv = jnp.dot(v_aug, p, preferred_element_type=F32)
            new_l.append(alpha * ls[g] + pv[HEAD_DIM:HEAD_DIM + 1, :])
            acc_scr[g, 0:HEAD_DIM, :] = alpha * acc_scr[g, 0:HEAD_DIM, :] + pv[0:HEAD_DIM, :]
            new_m.append(m_new)
        return tuple(new_m), tuple(new_l)

    wide = ATT_GROUP * tq

    def run_online():
        init = (tuple(jnp.full((1, wide), NEG_BIG, F32) for _ in range(ATT_KV_HEADS)),
                tuple(jnp.zeros((1, wide), F32) for _ in range(ATT_KV_HEADS)))
        _, ls = lax.fori_loop(0, nk, attend, init)
        for g in range(ATT_KV_HEADS):
            acc_scr[g, HEAD_DIM:, :] = jnp.broadcast_to(ls[g], (16, wide))

    def attend_bounded(kt, carry):
        kb = knat_ref[kt][:, 0:128]
        vtb = vT_ref[kt]
        bias = bias_scr[kt]
        for g in range(ATT_KV_HEADS):
            lt = jnp.dot(kb, qwide[g], preferred_element_type=F32)
            p = _mx(jnp.exp2(jnp.concatenate([lt[:, hh * tq:(hh + 1) * tq] + bias
                                              for hh in range(ATT_GROUP)], axis=1)))
            v_aug = jnp.concatenate([vtb[g * 64:(g + 1) * 64], ones_rows], axis=0)
            acc_scr[g] = acc_scr[g] + jnp.dot(v_aug, p, preferred_element_type=F32)
        return carry

    def run_bounded():
        def pair(i, carry):
            attend_bounded(2 * i, carry)
            return attend_bounded(2 * i + 1, carry)

        lax.fori_loop(0, nk // 2, pair, 0)

        @pl.when(nk % 2 == 1)
        def _():
            attend_bounded(nk - 1, 0)

    lax.cond(bound_is_small, run_bounded, run_online)
    outs = []
    for g in range(ATT_KV_HEADS):
        og = acc_scr[g, 0:HEAD_DIM, :] / acc_scr[g, HEAD_DIM:HEAD_DIM + 1, :]
        outs += [og[:, hh * tq:(hh + 1) * tq] for hh in range(ATT_GROUP)]
    o_ref[...] = jnp.concatenate(outs, axis=0).T.astype(o_ref.dtype)


def _dsa(qiT3, wgT3, knat3, qT3, vT3, batch, seq):
    tq = DSA_TILE
    nq = seq // tq
    n = batch * seq
    topk = min(INDEX_TOPK_MAX, seq // 4)
    qtile = lambda r: pl.BlockSpec((1, r, tq), lambda b, j: (b * nq + j, 0, 0))
    return pl.pallas_call(
        functools.partial(_dsa_kernel, topk=topk),
        grid=(batch, nq),
        in_specs=[qtile(512), qtile(16),
                  pl.BlockSpec((nq, tq, 256), lambda b, j: (b, 0, 0)),
                  qtile(512),
                  pl.BlockSpec((nq, 128, tq), lambda b, j: (b, 0, 0))],
        out_specs=pl.BlockSpec((tq, 512), lambda b, j: (b * nq + j, 0)),
        out_shape=jax.ShapeDtypeStruct((n, 512), MXU_DTYPE),
        scratch_shapes=[pltpu.VMEM((nq, tq, tq), I32), pltpu.VMEM((nq, tq, tq), jnp.bfloat16),
                        pltpu.VMEM((nq, tq, tq), F32),
                        pltpu.VMEM((ATT_KV_HEADS, HEAD_DIM + 16, ATT_GROUP * tq), F32),
                        pltpu.VMEM((8, tq), I32), pltpu.VMEM((8, tq), I32), pltpu.VMEM((8, LANE), F32)],
        compiler_params=_params("parallel", "arbitrary"),
        name="dsa",
    )(qiT3, wgT3, knat3, qT3, vT3)


def _hgrn_kernel(hg_ref, lb_ref, gn_ref, o_ref, st_scr, q_scr, k_scr, g_scr, qg_scr, kd_scr, o_scr, z_scr,
                 snap_scr):
    rows = hg_ref.shape[0]
    c = HG_SUB
    w = HG_WIDTH

    @pl.when(pl.program_id(1) == 0)
    def _():
        st_scr[...] = jnp.zeros(st_scr.shape, F32)
        z_scr[...] = jnp.zeros(z_scr.shape, z_scr.dtype)

    lb = lb_ref[...]
    q = _silu(hg_ref[:, 0:w]) * (HEAD_DIM ** -0.5)
    forget = lb + (1.0 - lb) * _sigmoid(hg_ref[:, w:2 * w])
    logf = jnp.log2(forget)
    kk = 1.0 - forget
    v = hg_ref[:, 2 * w:3 * w]

    ri = _iota((rows, rows), 0)
    ci = _iota((rows, rows), 1)
    same = (ri // c) == (ci // c)
    tri = _mx(jnp.where(same & (ci <= ri), 1.0, 0.0))
    ones_blk = _mx(jnp.where(same, 1.0, 0.0))
    gl = _dot01_left(tri, logf)
    glast = _dot01_left(ones_blk, logf)
    q_scr[...] = q
    k_scr[...] = gl - jnp.log2(kk)
    g_scr[...] = gl
    qg_scr[...] = q * jnp.exp2(gl)
    kd_scr[...] = kk * jnp.exp2(glast - gl)
    dec = jnp.exp2(glast)
    vT = _mx(v.T)
    head_ones = _mx(_block_ones(w, HEAD_DIM))
    pair_mask = _block_ones(LANE, HEAD_DIM)
    half = c // 2
    rhalf = _iota((half, w), 0)
    nsc = rows // c

    for sc in range(nsc):
        z_scr[sc, sc * c:(sc + 1) * c, :] = kd_scr[sc * c:(sc + 1) * c, :].astype(z_scr.dtype)
    for p in range(HG_HEADS // 2):
        ps = slice(p * LANE, (p + 1) * LANE)
        st = st_scr[p]
        for sc in range(nsc):
            snap_scr[sc, p] = st.astype(snap_scr.dtype)
            upd = jnp.dot(vT[ps, :], z_scr[sc, :, ps], preferred_element_type=F32)
            st = st * dec[sc * c:sc * c + 1, ps] + upd * pair_mask
        st_scr[p] = st

    for sc in range(nsc):
        r0 = sc * c
        q_top, q_bot = q_scr[r0:r0 + half, :], q_scr[r0 + half:r0 + c, :]
        g_top, g_bot = g_scr[r0:r0 + half, :], g_scr[r0 + half:r0 + c, :]
        bcast = lambda row: jnp.broadcast_to(row, (half, w))
        xs_top, xs_bot = [], []
        for s in range(c):
            hsr = bcast(k_scr[r0 + s:r0 + s + 1, :])
            if s < half:
                xs_top.append(q_top * jnp.exp2(jnp.where(rhalf >= s, g_top - hsr, -jnp.inf)))
                xs_bot.append(q_bot * jnp.exp2(g_bot - hsr))
            else:
                xs_bot.append(q_bot * jnp.exp2(jnp.where(rhalf >= s - half, g_bot - hsr, -jnp.inf)))
        y = jnp.dot(_mx(jnp.concatenate(xs_top + xs_bot, axis=0)), head_ones, preferred_element_type=F32)
        o_top = jnp.zeros((half, w), F32)
        o_bot = jnp.zeros((half, w), F32)
        for s in range(c):
            vsr = bcast(hg_ref[r0 + s:r0 + s + 1, 2 * w:3 * w])
            if s < half:
                o_top = o_top + y[s * half:(s + 1) * half, :] * vsr
            o_bot = o_bot + y[(half + s) * half:(half + s + 1) * half, :] * vsr
        o_diag = jnp.concatenate([o_top, o_bot], axis=0)
        o_pairs = [_dot_nt(qg_scr[r0:r0 + c, p * LANE:(p + 1) * LANE], snap_scr[sc, p])
                   for p in range(HG_HEADS // 2)]
        o_scr[r0:r0 + c, :] = jnp.concatenate(o_pairs, axis=1) + o_diag

    o = o_scr[...]
    ms = _dot01_right(o * o, head_ones) * (1.0 / HEAD_DIM)
    y = o * lax.rsqrt(ms + RMS_EPS) * gn_ref[...]
    o_ref[...] = (y * _silu(hg_ref[:, 3 * w:4 * w])).astype(o_ref.dtype)


def _hgrn(hg, lb, gn, batch, seq):
    rows = HG_ROWS
    ns = seq // rows
    n = batch * seq
    vm = lambda: pltpu.VMEM((rows, HG_WIDTH), F32)
    return pl.pallas_call(
        _hgrn_kernel,
        grid=(batch, ns),
        in_specs=[pl.BlockSpec((rows, 1024), lambda b, s: (b * ns + s, 0)),
                  _const_spec((1, HG_WIDTH)), _const_spec((1, HG_WIDTH))],
        out_specs=pl.BlockSpec((rows, HG_WIDTH), lambda b, s: (b * ns + s, 0)),
        out_shape=jax.ShapeDtypeStruct((n, HG_WIDTH), MXU_DTYPE),
        scratch_shapes=[pltpu.VMEM((HG_HEADS // 2, LANE, LANE), F32), vm(), vm(), vm(), vm(), vm(), vm(),
                        pltpu.VMEM((rows // HG_SUB, rows, HG_WIDTH), MXU_DTYPE),
                        pltpu.VMEM((rows // HG_SUB, HG_HEADS // 2, LANE, LANE), MXU_DTYPE)],
        compiler_params=_params("parallel", "arbitrary"),
        name="hgrn",
    )(hg, lb, gn)


def _mlstm_kernel(ml_ref, *refs):
    gT_refs = refs[:ML_SEQS]
    cw_ref, cb_ref, brow_ref, gn_ref, o_ref, prev_scr, c_scr, n_scr, m64_scr, m128_scr = refs[ML_SEQS:]

    @pl.when(pl.program_id(1) == 0)
    def _():
        prev_scr[...] = jnp.zeros(prev_scr.shape, F32)
        c_scr[...] = jnp.zeros(c_scr.shape, F32)
        n_scr[...] = jnp.zeros(n_scr.shape, F32)
        m64_scr[...] = jnp.full(m64_scr.shape, M_INIT, F32)
        m128_scr[...] = jnp.full(m128_scr.shape, M_INIT, F32)

    seqs = [_mlstm_sequence(ml_ref.at[bi], gT_ref, cw_ref, cb_ref, brow_ref, gn_ref, o_ref.at[bi],
                            prev_scr.at[bi], c_scr.at[bi], n_scr.at[bi], m64_scr.at[bi], m128_scr.at[bi])
            for bi, gT_ref in enumerate(gT_refs)]
    while seqs:
        for seq in list(seqs):
            if next(seq, "done") == "done":
                seqs.remove(seq)


def _mlstm_sequence(ml_ref, gT_ref, cw_ref, cb_ref, brow_ref, gn_ref, o_ref,
                    prev_scr, c_scr, n_scr, m64_scr, m128_scr):
    L = ML_CHUNK
    w = ML_WIDTH

    xqk = ml_ref[:, 0:512]
    prev = prev_scr[...]
    rows = _iota((L, 512), 0)
    acc = cb_ref[...] + cw_ref[ML_CONV - 1:ML_CONV, :] * xqk
    for sh in range(1, ML_CONV):
        shifted = jnp.where(rows >= sh, pltpu.roll(xqk, sh, 0), pltpu.roll(prev, sh, 0))
        acc = acc + cw_ref[ML_CONV - 1 - sh:ML_CONV - sh, :] * shifted
    prev_scr[...] = xqk
    qk = _silu(acc)
    q = qk[:, 0:w]
    k = qk[:, w:2 * w] * (HEAD_DIM ** -0.5)
    v = ml_ref[:, 512:768]

    srow = _iota((L, L), 0)
    scol = _iota((L, L), 1)
    tril = srow >= scol
    triu = _mx(jnp.where(srow <= scol, 1.0, 0.0))
    grow = gT_ref[0] + brow_ref[...]
    brow = _dot01_right(_log_sigmoid(grow), triu)
    yield

    def expand(x, width, row0):
        src = _iota((16, ML_HEADS * width), 0)
        dst = _iota((16, ML_HEADS * width), 1)
        e01 = _mx(jnp.where(src == row0 + dst // width, 1.0, 0.0))
        f = lambda p: lax.dot_general(p, e01, (((0,), (0,)), ((), ())), preferred_element_type=F32)
        hi, mid, lo = _split3(x)
        return f(hi) + f(mid) + f(lo)

    b128 = expand(brow, 128, 12)
    i128 = expand(grow, 128, 8)
    b64 = expand(brow, 64, 12)
    i64 = expand(grow, 64, 8)
    yield

    lane = _iota((L, LANE), 1)
    first = lane < HEAD_DIM
    pair_ones = _mx(_block_ones(LANE, HEAD_DIM))
    pair_mask = _block_ones(LANE, HEAD_DIM)
    m_prev128 = m128_scr[0:1, :]
    m_prev64 = m64_scr[0:1, :]
    n_row = n_scr[0:1, :]

    houts = []
    for p in range(ML_HEADS // 2):
        ps = slice(p * LANE, (p + 1) * LANE)
        q_pair, k_pair, v_pair = q[:, ps], k[:, ps], v[:, ps]
        per_head = []
        for hh in range(2):
            hd = 2 * p + hh
            hs = slice(hd * 128, (hd + 1) * 128)
            bt = b128[:, hs]
            d = bt - brow[12 + hd:13 + hd, :] + grow[8 + hd:9 + hd, :]
            d = jnp.where(tril, d, -jnp.inf)
            inter = bt + m_prev128[:, hs]
            m_t = jnp.maximum(inter, jnp.max(d, axis=1, keepdims=True))
            wgt = jnp.exp(d - m_t)
            a = jnp.exp(inter - m_t)
            qm = jnp.where(first if hh == 0 else ~first, q_pair, 0.0)
            s = _dot_nt(qm, k_pair) * wgt
            sv = _dot(s, v_pair)
            rs = jnp.sum(s, axis=1, keepdims=True)
            per_head.append((m_t, a, sv, rs))
            yield
        sel = lambda i: jnp.where(first, per_head[0][i], per_head[1][i])
        m_t, a, sv, rs = sel(0), sel(1), sel(2), sel(3)
        qc = _dot(q_pair, c_scr[p])
        qn = _dot(q_pair * n_row[:, ps], pair_ones)
        num = a * qc + sv
        den = a * qn + rs
        houts.append(num / jnp.maximum(jnp.abs(den), jnp.exp(-m_t)))
        yield
    hout = jnp.concatenate(houts, axis=1)

    def new_m(bx, ix, m_prev):
        blast = bx[L - 1:L, :]
        wl = blast - bx + ix
        m_new = jnp.maximum(blast + m_prev, jnp.max(wl, axis=0, keepdims=True))
        return blast, wl, m_new

    blast, wl, m_new64 = new_m(b64, i64, m_prev64)
    dec = jnp.exp(blast + m_prev64 - m_new64)
    kw = k * jnp.exp(wl - m_new64)
    for p in range(ML_HEADS // 2):
        ps = slice(p * LANE, (p + 1) * LANE)
        upd = jnp.dot(_mx(kw[:, ps].T), _mx(v[:, ps]), preferred_element_type=F32)
        c_scr[p] = c_scr[p] * dec[:, ps] + upd * pair_mask
    n_scr[...] = jnp.broadcast_to(dec * n_row + jnp.sum(kw, axis=0, keepdims=True), n_scr.shape)
    m64_scr[...] = jnp.broadcast_to(m_new64, m64_scr.shape)
    _, _, m_new128 = new_m(b128, i128, m_prev128)
    m128_scr[...] = jnp.broadcast_to(m_new128, m128_scr.shape)

    head_ones = _mx(_block_ones(w, HEAD_DIM))
    ms = _dot01_right(hout * hout, head_ones) * (1.0 / HEAD_DIM)
    y = hout * lax.rsqrt(ms + RMS_EPS) * gn_ref[...]
    o_ref[...] = (y * _sigmoid(ml_ref[:, 768:1024])).astype(o_ref.dtype)


def _mlstm(ml, wgT3, cw, cb, brow, gn, batch, seq):
    L = ML_CHUNK
    nc = seq // L
    n = batch * seq
    per_tile = DSA_TILE // L
    nb = ML_SEQS
    assert batch % nb == 0

    def gate_spec(off):
        def index(b, s):
            chunk = (nb * b + off) * nc + s
            return chunk // per_tile, 0, chunk % per_tile
        return pl.BlockSpec((1, 16, L), index)

    out = pl.pallas_call(
        _mlstm_kernel,
        grid=(batch // nb, nc),
        in_specs=[pl.BlockSpec((nb, L, 1024), lambda b, s: (b, s, 0))] + [gate_spec(o) for o in range(nb)]
                 + [_const_spec((ML_CONV, 512)), _const_spec((1, 512)),
                  _const_spec((16, L)), _const_spec((1, ML_WIDTH))],
        out_specs=pl.BlockSpec((nb, L, ML_WIDTH), lambda b, s: (b, s, 0)),
        out_shape=jax.ShapeDtypeStruct((batch, seq, ML_WIDTH), MXU_DTYPE),
        scratch_shapes=[pltpu.VMEM((nb, L, 512), F32), pltpu.VMEM((nb, 2, LANE, LANE), F32),
                        pltpu.VMEM((nb, 8, ML_WIDTH), F32), pltpu.VMEM((nb, 8, ML_WIDTH), F32),
                        pltpu.VMEM((nb, 8, 512), F32)],
        compiler_params=_params("parallel", "arbitrary"),
        name="mlstm",
    )(ml.reshape(batch, seq, 1024), *([wgT3] * nb), cw, cb, brow, gn)
    return out.reshape(n, ML_WIDTH)


def _rms(y, g):
    ms = jnp.mean(y * y, axis=-1, keepdims=True)
    return y * lax.rsqrt(ms + RMS_EPS) * g


def _mem_kv_kernel(m_ref, g_ref, w_ref, k_ref, v_ref):
    h = _mx(_rms(m_ref[...], g_ref[...]))
    kv = jnp.dot(h, w_ref[...], preferred_element_type=F32)
    k_ref[...] = kv[:, 0:D_MODEL].astype(k_ref.dtype)
    v_ref[...] = kv[:, D_MODEL:2 * D_MODEL].astype(v_ref.dtype)


def _mem_kv(mem2, g, w):
    n = mem2.shape[0]
    tm = ROW_TILE
    row = pl.BlockSpec((tm, D_MODEL), lambda i: (i, 0))
    return pl.pallas_call(
        _mem_kv_kernel,
        grid=(n // tm,),
        in_specs=[row, _const_spec((1, D_MODEL)), _const_spec((D_MODEL, 2 * D_MODEL))],
        out_specs=[row, row],
        out_shape=[jax.ShapeDtypeStruct((n, D_MODEL), MXU_DTYPE)] * 2,
        compiler_params=_params("parallel"),
        name="mem_kv",
    )(mem2, g, w)


def _post_mix_kernel(a_ref, b_ref, c_ref, x_ref, k_ref, v_ref, wout_ref, wq_ref, wo_ref, wu_ref, wd_ref,
                     g_ref, o_ref):
    g = g_ref[...]
    y = (jnp.dot(a_ref[...], wout_ref[0:512, :], preferred_element_type=F32)
         + jnp.dot(b_ref[...], wout_ref[512:768, :], preferred_element_type=F32)
         + jnp.dot(c_ref[...], wout_ref[768:1024, :], preferred_element_type=F32))
    x = x_ref[...] + _rms(y, g[0:1])
    h = _mx(_rms(x, g[1:2]))
    q = jnp.dot(h, wq_ref[...], preferred_element_type=F32) * (CROSS_HEAD_DIM ** -0.5)
    outs = []
    for hd in range(CROSS_HEADS):
        hs = slice(hd * CROSS_HEAD_DIM, (hd + 1) * CROSS_HEAD_DIM)
        logits = _dot_nt(q[:, hs], k_ref[:, hs])
        p = jnp.exp(logits - jnp.max(logits, axis=-1, keepdims=True))
        o = jnp.dot(_mx(p), v_ref[:, hs], preferred_element_type=F32)
        outs.append(o / jnp.sum(p, axis=-1, keepdims=True))
    y = jnp.dot(_mx(jnp.concatenate(outs, axis=1)), wo_ref[...], preferred_element_type=F32)
    x = x + _rms(y, g[2:3])
    h = _mx(_rms(x, g[3:4]))
    y = jnp.zeros(x.shape, F32)
    for c in range(MLP_HIDDEN // D_MODEL):
        cs = slice(c * D_MODEL, (c + 1) * D_MODEL)
        u = jnp.maximum(jnp.dot(h, wu_ref[:, cs], preferred_element_type=F32), 0.0)
        y = y + jnp.dot(_mx(u * u), wd_ref[cs, :], preferred_element_type=F32)
    o_ref[...] = x + _rms(y, g[4:5])


def _post_mix(a, b, c, x2, kmem, vmem, wout, wq, wo, wu, wd, gains, seq, n_mem):
    n = x2.shape[0]
    tm = POST_TILE
    per_b = seq // tm
    row = lambda width: pl.BlockSpec((tm, width), lambda i: (i, 0))
    memb =pl.BlockSpec((n_mem, D_MODEL), lambda i: (i // per_b, 0))
    sq = _const_spec((D_MODEL, D_MODEL))
    return pl.pallas_call(
        _post_mix_kernel,
        grid=(n // tm,),
        in_specs=[row(512), row(256), row(256), row(D_MODEL), memb, memb, sq, sq, sq,
                  _const_spec((D_MODEL, MLP_HIDDEN)), _const_spec((MLP_HIDDEN, D_MODEL)),
                  _const_spec((8, D_MODEL))],
        out_specs=row(D_MODEL),
        out_shape=jax.ShapeDtypeStruct((n, D_MODEL), F32),
        compiler_params=_params("parallel"),
        name="post_mix",
    )(a, b, c, x2, kmem, vmem, wout, wq, wo, wu, wd, gains)


def _rope_tables(seq):
    half = HEAD_DIM // 2
    inv = ROPE_THETA ** (-jnp.arange(0, HEAD_DIM, 2, dtype=F32) / HEAD_DIM)
    ang = jnp.arange(seq).astype(F32)[:, None] * inv[None, :]
    cos, sin = jnp.cos(ang), jnp.sin(ang)
    lane = jnp.arange(LANE)
    freq = (lane % HEAD_DIM) % half
    first = (lane % HEAD_DIM) < half
    cosn = cos[:, freq]
    sinn = sin[:, freq]
    sina = jnp.where(first[None, :], -sinn, 0.0)
    sinb = jnp.where(first[None, :], 0.0, sinn)
    return cosn, sina, sinb, cos.T, sin.T


def _layout_w_in(w_in):
    depth = w_in.shape[0]
    z = lambda wd: jnp.zeros((depth, D_MODEL, wd), w_in.dtype)
    col = lambda a, b: w_in[:, :, a:b]
    wn = jnp.concatenate([col(_O_AK, _O_AV), col(_O_XK, _O_XW), z(64),
                          col(_O_HQ, _O_MQK), col(_O_MQK, _O_MG)], axis=-1)
    wt = jnp.concatenate([col(_O_AQ, _O_AK), col(_O_XQ, _O_XK), col(_O_AV, _O_XQ),
                          col(_O_XW, _O_HQ), col(_O_MG, _IN_WIDTH)], axis=-1)
    return _mx(wn), _mx(jnp.swapaxes(wt, 1, 2))


def kernel(x, mem, mix_pre_g, w_in, ml_conv_w, ml_conv_b, ml_gate_b, hg_lb, hg_norm_g, ml_norm_g,
           w_out, mix_post_g, cross_pre_g, mem_norm_g, w_cq, w_ckv, w_co, cross_post_g, mlp_pre_g,
           w_up, w_down, mlp_post_g):
    batch, seq, _ = x.shape
    n_mem = mem.shape[1]
    depth = w_in.shape[0]
    n = batch * seq
    assert seq % max(ROW_TILE, POST_TILE, HG_ROWS, ML_CHUNK, DSA_TILE) == 0
    assert (batch * n_mem) % ROW_TILE == 0
    assert ROW_TILE % DSA_TILE == 0 and DSA_TILE % ML_CHUNK == 0

    lbs = jnp.cumsum(jax.nn.softmax(hg_lb.astype(F32), axis=0), axis=0)
    lbs = lbs - lbs[0:1]

    tabs = _rope_tables(seq)
    wn_all, wt_all = _layout_w_in(w_in)
    row = lambda a: a.astype(F32)[None, :]
    brow =jnp.pad(ml_gate_b.astype(F32), ((0, 0), (8, 0)))

    x2 = x.reshape(n, D_MODEL)
    mem2 = mem.reshape(batch * n_mem, D_MODEL)
    nl = n // DSA_TILE
    for l in range(depth):
        knat, hg, ml, qT3, qiT3, vT3, wgT3 = _in_proj(x2, row(mix_pre_g[l]), wn_all[l], wt_all[l], tabs, seq)
        a_out = _dsa(qiT3, wgT3, knat.reshape(nl, DSA_TILE, 256), qT3, vT3, batch, seq)
        b_out = _hgrn(hg, row(lbs[l]), row(jnp.tile(hg_norm_g[l], HG_HEADS)), batch, seq)
        c_out = _mlstm(ml, wgT3, ml_conv_w[l].astype(F32), row(ml_conv_b[l]),
                       jnp.broadcast_to(brow[l][:, None], (16, ML_CHUNK)),
                       row(jnp.tile(ml_norm_g[l], ML_HEADS)), batch, seq)
        kmem, vmem = _mem_kv(mem2, row(mem_norm_g[l]), _mx(w_ckv[l]))
        gains = jnp.stack([mix_post_g[l], cross_pre_g[l], cross_post_g[l], mlp_pre_g[l], mlp_post_g[l]]
                          + [jnp.zeros_like(mix_post_g[l])] * 3).astype(F32)
        x2 = _post_mix(a_out, b_out, c_out, x2, kmem, vmem, _mx(w_out[l]), _mx(w_cq[l]), _mx(w_co[l]),
                       _mx(w_up[l]), _mx(w_down[l]), gains, seq, n_mem)
    return x2.reshape(batch, seq, D_MODEL)
```

```python
import functools

import jax
import jax.numpy as jnp
from jax import lax
from jax.experimental import pallas as pl
from jax.experimental.pallas import tpu as pltpu

F32 = jnp.float32
I32 = jnp.int32
MXU_DTYPE = jnp.bfloat16

D_MODEL = 1024
RMS_EPS = 1e-6
ROPE_THETA = 10000.0
NEG_BIG = -1e30
M_INIT = -1e30
LOG2E = 1.4426950408889634
F32_MIN_NORMAL = 1.1754943508222875e-38
BOUND_MARGIN = 1.05
SAFE_LOGIT_BOUND = 60.0
HEAD_DIM = 64
ATT_HEADS = 8
ATT_KV_HEADS = 2
ATT_GROUP = ATT_HEADS // ATT_KV_HEADS
IDX_HEADS = 8
INDEX_TOPK_MAX = 256
HG_HEADS = 4
HG_WIDTH = HG_HEADS * HEAD_DIM
ML_HEADS = 4
ML_WIDTH = ML_HEADS * HEAD_DIM
ML_CONV = 4
CROSS_HEADS = 4
CROSS_HEAD_DIM = D_MODEL // CROSS_HEADS
MLP_HIDDEN = 4 * D_MODEL

_O_AQ, _O_AK, _O_AV, _O_XQ, _O_XK, _O_XW = 0, 512, 640, 768, 1280, 1344
_O_HQ, _O_MQK, _O_MV, _O_MO, _O_MG, _IN_WIDTH = 1352, 2376, 2888, 3144, 3400, 3408

NAT_WIDTH = 128 + 128 + 1024 + 1024
TR_WIDTH = 512 + 512 + 128 + 16

LANE = 128
DSA_TILE = 256
ATT_STRIPS = 4
ROW_TILE = 256
POST_TILE = 512
HG_SUB = 16
HG_ROWS = 256
ML_CHUNK = 128
ML_SEQS = 4
VMEM_LIMIT = 56 * 1024 * 1024

_NT = (((1,), (1,)), ((), ()))


def _mx(a):
    return a.astype(MXU_DTYPE)


def _dot(a, b):
    return jnp.dot(_mx(a), _mx(b), preferred_element_type=F32)


def _dot_nt(a, b):
    return lax.dot_general(_mx(a), _mx(b), _NT, preferred_element_type=F32)


def _split3(x):
    hi = _mx(x)
    r1 = x - hi.astype(F32)
    mid = _mx(r1)
    lo = _mx(r1 - mid.astype(F32))
    return hi, mid, lo


def _dot01_left(m01, x):
    hi, mid, lo = _split3(x)
    f = lambda p: jnp.dot(m01, p, preferred_element_type=F32)
    return f(hi) + f(mid) + f(lo)


def _dot01_right(x, m01):
    hi, mid, lo = _split3(x)
    f = lambda p: jnp.dot(p, m01, preferred_element_type=F32)
    return f(hi) + f(mid) + f(lo)


def _sigmoid(x):
    return 1.0 / (1.0 + jnp.exp(-x))


def _silu(x):
    return x * _sigmoid(x)


def _log_sigmoid(x):
    return jnp.minimum(x, 0.0) - jnp.log(1.0 + jnp.exp(-jnp.abs(x)))


def _iota(shape, dim):
    return lax.broadcasted_iota(I32, shape, dim)


def _block_ones(n, blk):
    same = (_iota((n, n), 0) // blk) == (_iota((n, n), 1) // blk)
    return jnp.where(same, 1.0, 0.0)


def _params(*sem):
    return pltpu.CompilerParams(dimension_semantics=sem, vmem_limit_bytes=VMEM_LIMIT)


def _const_spec(shape):
    nd = len(shape)
    return pl.BlockSpec(shape, lambda *_: (0,) * nd, pipeline_mode=pl.Buffered(1))


def _in_proj_kernel(x_ref, g_ref, wn_ref, wt_ref, cosn_ref, sina_ref, sinb_ref, cost_ref, sint_ref,
                    knat_ref, hg_ref, ml_ref, qT_ref, qiT_ref, vT_ref, wgT_ref, *, idx_scale):
    tm = x_ref.shape[0]
    x = x_ref[...]
    ms = jnp.mean(x * x, axis=-1, keepdims=True)
    h = _mx(x * lax.rsqrt(ms + RMS_EPS) * g_ref[...])
    nat = jnp.dot(h, wn_ref[...], preferred_element_type=F32)
    tr = lax.dot_general(wt_ref[...], h, _NT, preferred_element_type=F32)

    cosn, sina, sinb = cosn_ref[...], sina_ref[...], sinb_ref[...]

    def rope_nat(z):
        return z * cosn + pltpu.roll(z, 96, 1) * sina + pltpu.roll(z, 32, 1) * sinb

    knat_ref[:, 0:128] = rope_nat(nat[:, 0:128]).astype(knat_ref.dtype)
    knat_ref[:, 128:256] = rope_nat(nat[:, 128:256]).astype(knat_ref.dtype)
    hg_ref[...] = nat[:, 256:1280]
    ml_ref[...] = nat[:, 1280:NAT_WIDTH]

    cost, sint = cost_ref[...], sint_ref[...]

    def rope_t(z, scale):
        outs = []
        for hd in range(8):
            x1 = z[hd * 64:hd * 64 + 32]
            x2 = z[hd * 64 + 32:hd * 64 + 64]
            outs.append((x1 * cost - x2 * sint) * scale)
            outs.append((x2 * cost + x1 * sint) * scale)
        return jnp.concatenate(outs, axis=0)

    qT = rope_t(tr[0:512], (HEAD_DIM ** -0.5) * LOG2E).astype(qT_ref.dtype)
    qiT = rope_t(tr[512:1024], 1.0).astype(qiT_ref.dtype)
    vT = tr[1024:1152].astype(vT_ref.dtype)
    wg = tr[1152:1168]
    wg = jnp.where(_iota(wg.shape, 0) < 8, wg * idx_scale, wg)
    for c in range(tm // DSA_TILE):
        sl = slice(c * DSA_TILE, (c + 1) * DSA_TILE)
        qT_ref[c] = qT[:, sl]
        qiT_ref[c] = qiT[:, sl]
        vT_ref[c] = vT[:, sl]
        wgT_ref[c] = wg[:, sl]


def _in_proj(x2, g, wn, wt, tabs, seq):
    n = x2.shape[0]
    tm = ROW_TILE
    nt_seq = seq // tm
    cosn, sina, sinb, cost, sint = tabs
    idx_scale = (IDX_HEADS ** -0.5) * (HEAD_DIM ** -0.5)
    row = lambda w: pl.BlockSpec((tm, w), lambda i: (i, 0))
    tabn = pl.BlockSpec((tm, LANE), lambda i: (i % nt_seq, 0))
    tabt = pl.BlockSpec((32, tm), lambda i: (0, i % nt_seq))
    t3 = lambda r: pl.BlockSpec((tm // DSA_TILE, r, DSA_TILE), lambda i: (i, 0, 0))
    nl = n // DSA_TILE
    return pl.pallas_call(
        functools.partial(_in_proj_kernel, idx_scale=idx_scale),
        grid=(n // tm,),
        in_specs=[row(D_MODEL), _const_spec((1, D_MODEL)), _const_spec((D_MODEL, NAT_WIDTH)),
                  _const_spec((TR_WIDTH, D_MODEL)), tabn, tabn, tabn, tabt, tabt],
        out_specs=[row(256), row(1024), row(1024), t3(512), t3(512), t3(128), t3(16)],
        out_shape=[jax.ShapeDtypeStruct((n, 256), MXU_DTYPE),
                   jax.ShapeDtypeStruct((n, 1024), F32),
                   jax.ShapeDtypeStruct((n, 1024), F32),
                   jax.ShapeDtypeStruct((nl, 512, DSA_TILE), MXU_DTYPE),
                   jax.ShapeDtypeStruct((nl, 512, DSA_TILE), MXU_DTYPE),
                   jax.ShapeDtypeStruct((nl, 128, DSA_TILE), MXU_DTYPE),
                   jax.ShapeDtypeStruct((nl, 16, DSA_TILE), F32)],
        compiler_params=_params("parallel"),
        name="in_proj",
    )(x2, g, wn, wt, cosn, sina, sinb, cost, sint)


def _dsa_kernel(qiT_ref, wgT_ref, knat_ref, qTn_ref, qTp_ref, vT_ref, o_ref,
                key_scr, hi_scr, bias_scr, acc_scr, thr_scr, nge_scr, kmax_scr, *, topk):
    tq = DSA_TILE
    nq = key_scr.shape[0]
    j = pl.program_id(1)
    nk = j + 1
    has_cur = j < nq
    has_prev = j >= 1
    cslot = j % 2
    pslot = (j + 1) % 2
    int_min = jnp.int32(-2 ** 31)
    srow = _iota((tq, tq), 0)
    tcol = _iota((tq, tq), 1)
    causal = srow <= tcol

    @pl.when(j == 0)
    def _():
        group_ones = _mx(_block_ones(LANE, HEAD_DIM))

        def body(kt, m):
            k = knat_ref[kt][:, 0:128].astype(F32)
            n2 = jnp.dot(_mx(k * k), group_ones, preferred_element_type=F32)
            return jnp.maximum(m, jnp.max(n2, axis=0, keepdims=True))

        kmax2 = lax.fori_loop(0, nq, body, jnp.zeros((1, LANE), F32))
        kmax_scr[...] = jnp.broadcast_to(kmax2, kmax_scr.shape)

    def logit_bound(q):
        qf = q.astype(F32)
        bound2 = jnp.zeros((1, tq), F32)
        for hd in range(ATT_HEADS):
            g = hd // ATT_GROUP
            qh = qf[hd * 64:(hd + 1) * 64]
            qn2 = jnp.sum(qh * qh, axis=0, keepdims=True)
            bound2 = jnp.maximum(bound2, qn2 * kmax_scr[0:1, g * HEAD_DIM:g * HEAD_DIM + 1])
        return BOUND_MARGIN * jnp.sqrt(bound2)

    @pl.when(has_cur)
    def _():
        w = wgT_ref[0]
        qi = qiT_ref[0]
        qi_wide = jnp.concatenate([qi[hd * 64:(hd + 1) * 64] for hd in range(IDX_HEADS)], axis=1)

        def score_tile(kt, carry):
            kblk = knat_ref[kt][:, 128:192]
            acc = jnp.zeros((tq, tq), F32)
            ys = jnp.dot(kblk, qi_wide, preferred_element_type=F32)
            for hd in range(IDX_HEADS):
                acc = acc + w[hd:hd + 1, :] * jnp.maximum(ys[:, hd * tq:(hd + 1) * tq], 0.0)
            acc = jnp.where(jnp.abs(acc) < F32_MIN_NORMAL, 0.0, acc)
            acc = jnp.where(jnp.logical_or(kt < j, causal), acc, -jnp.inf)
            bits = pltpu.bitcast(acc, I32)
            key_scr[kt] = bits ^ ((bits >> 31) & jnp.int32(0x7FFFFFFF))
            hi_scr[kt] = pltpu.bitcast(bits & jnp.int32(-65536), F32).astype(jnp.bfloat16)
            return carry

        lax.fori_loop(0, nk, score_tile, 0)

    acc_scr[...] = jnp.zeros(acc_scr.shape, F32)
    q = qTp_ref[0]
    prev_small = jnp.max(logit_bound(q)) < SAFE_LOGIT_BOUND
    zeros64 = jnp.zeros((64, tq), q.dtype)
    qwide = []
    for g in range(ATT_KV_HEADS):
        cols = []
        for hh in range(ATT_GROUP):
            hd = g * ATT_GROUP + hh
            qh = q[hd * 64:(hd + 1) * 64]
            cols.append(jnp.concatenate([qh, zeros64] if g == 0 else [zeros64, qh], axis=0))
        qwide.append(jnp.concatenate(cols, axis=1))
    ones_rows = jnp.ones((16, tq), q.dtype)
    wide = ATT_GROUP * tq
    penalty = jnp.where(prev_small, 0.0, NEG_BIG)

    def attend_strip(kt, strip):
        g, part = divmod(strip, ATT_STRIPS // ATT_KV_HEADS)
        heads = ATT_GROUP // (ATT_STRIPS // ATT_KV_HEADS)
        cols = slice(part * heads * tq, (part + 1) * heads * tq)
        kb = knat_ref[kt][:, 0:128]
        vtb = vT_ref[kt]
        bias = bias_scr[pslot, kt] + penalty
        lt = jnp.dot(kb, qwide[g][:, cols], preferred_element_type=F32)
        p = _mx(jnp.exp2(jnp.concatenate([lt[:, hh * tq:(hh + 1) * tq] + bias for hh in range(heads)], axis=1)))
        v_aug = jnp.concatenate([vtb[g * 64:(g + 1) * 64], ones_rows], axis=0)
        acc_scr[g, :, cols] = acc_scr[g, :, cols] + jnp.dot(v_aug, p, preferred_element_type=F32)

    def attend_bounded(kt):
        for strip in range(ATT_STRIPS):
            attend_strip(kt, strip)

    def attend_online(kt, carry):
        ms, ls = carry
        kb = knat_ref[kt][:, 0:128]
        vtb = vT_ref[kt]
        bias = bias_scr[pslot, kt]
        lts = [jnp.dot(kb, qwide[g], preferred_element_type=F32) for g in range(ATT_KV_HEADS)]
        new_m, new_l = [], []
        for g in range(ATT_KV_HEADS):
            lt = jnp.concatenate([lts[g][:, hh * tq:(hh + 1) * tq] + bias for hh in range(ATT_GROUP)], axis=1)
            m_new = jnp.maximum(ms[g], jnp.max(lt, axis=0, keepdims=True))
            alpha = jnp.exp2(ms[g] - m_new)
            p = _mx(jnp.exp2(lt - m_new))
            v_aug = jnp.concatenate([vtb[g * 64:(g + 1) * 64], ones_rows], axis=0)
            pv = jnp.dot(v_aug, p, preferred_element_type=F32)
            new_l.append(alpha * ls[g] + pv[HEAD_DIM:HEAD_DIM + 1, :])
            acc_scr[g, 0:HEAD_DIM, :] = alpha * acc_scr[g, 0:HEAD_DIM, :] + pv[0:HEAD_DIM, :]
            new_m.append(m_new)
        return tuple(new_m), tuple(new_l)

    def attend_all_bounded(n_tiles):
        def body(kt, carry):
            attend_bounded(kt)
            return carry
        lax.fori_loop(0, n_tiles, body, 0)

    key16_neg_inf = (0xFF80 ^ 0x7FFF) - 65536
    one_h = jnp.ones((16, tq), jnp.bfloat16)
    zero_h = jnp.zeros((16, tq), jnp.bfloat16)

    def select_threshold(nks, n_prev):
        def count16(cand16):
            b16 = cand16 ^ ((cand16 >> 15) & 0x7FFF)
            b16 = jnp.where((cand16 > 0) & (cand16 < 0x80), 0x80, b16)
            cb = jnp.broadcast_to(pltpu.bitcast(b16 << 16, F32).astype(jnp.bfloat16), (16, tq))
            accs = [zero_h] * 4
            n = 0
            for kt in range(nks):
                for r in range(tq // 16):
                    ind = jnp.where(hi_scr[kt, r * 16:(r + 1) * 16, :] >= cb, one_h, zero_h)
                    accs[n % 4] = accs[n % 4] + ind
                    n += 1
            tot = (accs[0] + accs[1]).astype(F32) + (accs[2] + accs[3]).astype(F32)
            return jnp.sum(tot, axis=0, keepdims=True)

        def count32(pred_fn):
            accs = [jnp.zeros((8, tq), I32)] * 4
            n = 0
            for kt in range(nks):
                for r in range(tq // 8):
                    ind = jnp.where(pred_fn(key_scr[kt, r * 8:(r + 1) * 8, :]), 1, 0).astype(I32)
                    accs[n % 4] = accs[n % 4] + ind
                    n += 1
            return jnp.sum((accs[0] + accs[1]) + (accs[2] + accs[3]), axis=0, keepdims=True)

        def stage16(i, lo):
            cand = lo + (jnp.int32(1) << (15 - i))
            ok = (count16(cand) >= topk) | (cand <= key16_neg_inf)
            return jnp.where(ok, cand, lo)

        def stage32(i, carry):
            lo, _ = carry
            cand = lo + (jnp.int32(1) << (15 - i))
            cnt = count32(lambda k: k >= cand)
            ok = cnt >= topk
            return jnp.where(ok, cand, lo), jnp.where(ok, cnt, carry[1])

        per_stage = 16 // ATT_STRIPS
        n32 = min(n_prev, per_stage)
        n16 = n_prev - n32
        assert n16 <= per_stage

        def with_attention(stage, first_tile):
            def trip(t, carry):
                for u in range(ATT_STRIPS):
                    attend_strip(first_tile + t, u)
                    carry = stage(ATT_STRIPS * t + u, carry)
                return carry
            return trip

        lo16 = lax.fori_loop(0, n16, with_attention(stage16, n32), jnp.full((1, tq), -32768, I32))
        lo16 = lax.fori_loop(ATT_STRIPS * n16, 16, stage16, lo16)

        carry = (lo16 << 16, jnp.full((1, tq), nks * tq, I32))
        carry = lax.fori_loop(0, n32, with_attention(stage32, 0), carry)
        thr, n_ge = lax.fori_loop(ATT_STRIPS * n32, 16, stage32, carry)
        thr_scr[...] = jnp.broadcast_to(thr, thr_scr.shape)
        nge_scr[...] = jnp.broadcast_to(n_ge, nge_scr.shape)

    for jj in range(nq + 1):
        @pl.when(j == jj)
        def _(jj=jj):
            if jj == nq:
                attend_all_bounded(nq)
            elif (jj + 1) * tq <= topk:
                thr_scr[...] = jnp.full(thr_scr.shape, int_min, I32)
                nge_scr[...] = jnp.full(nge_scr.shape, topk, I32)
                if jj > 0:
                    attend_all_bounded(jj)
            else:
                select_threshold(jj + 1, jj)

    @pl.when(has_cur)
    def _():
        thr = thr_scr[0:1, :]
        n_ge = nge_scr[0:1, :]
        neg_bound = -logit_bound(qTn_ref[0])

        def mask_plain():
            def body(kt, carry):
                bias_scr[cslot, kt] = jnp.where(key_scr[kt] >= thr, neg_bound, NEG_BIG)
                return carry
            lax.fori_loop(0, nk, body, 0)

        def mask_ties():
            def count_gt(kt, c):
                ind = jnp.where(key_scr[kt] > thr, 1.0, 0.0)
                return c + jnp.sum(ind, axis=0, keepdims=True)
            n_gt = lax.fori_loop(0, nk, count_gt, jnp.zeros((1, tq), F32))
            need = topk - n_gt
            tri = _mx(jnp.where(srow >= tcol, 1.0, 0.0))

            def body(kt, off):
                k = key_scr[kt]
                eq = k == thr
                eqf = jnp.where(eq, 1.0, 0.0)
                rank = jnp.dot(tri, _mx(eqf), preferred_element_type=F32) + off
                tie_bias = jnp.where(rank <= need, neg_bound, NEG_BIG)
                bias_scr[cslot, kt] = jnp.where(k > thr, neg_bound, jnp.where(eq, tie_bias, NEG_BIG))
                return off + jnp.sum(eqf, axis=0, keepdims=True)
            lax.fori_loop(0, nk, body, jnp.zeros((1, tq), F32))

        no_excess_ties = jnp.max(jnp.abs(n_ge - topk)) == 0
        lax.cond(no_excess_ties, mask_plain, mask_ties)
        bias_scr[cslot, j] = jnp.where(causal, bias_scr[cslot, j], NEG_BIG)

    @pl.when(has_prev)
    def _():
        @pl.when(jnp.logical_not(prev_small))
        def _():
            init = (tuple(jnp.full((1, wide), NEG_BIG, F32) for _ in range(ATT_KV_HEADS)),
                    tuple(jnp.zeros((1, wide), F32) for _ in range(ATT_KV_HEADS)))
            _, ls = lax.fori_loop(0, j, attend_online, init)
            for g in range(ATT_KV_HEADS):
                acc_scr[g, HEAD_DIM:, :] = jnp.broadcast_to(ls[g], (16, wide))

        outs = []
        for g in range(ATT_KV_HEADS):
            og = acc_scr[g, 0:HEAD_DIM, :] / acc_scr[g, HEAD_DIM:HEAD_DIM + 1, :]
            outs += [og[:, hh * tq:(hh + 1) * tq] for hh in range(ATT_GROUP)]
        o_ref[...] = jnp.concatenate(outs, axis=0).T.astype(o_ref.dtype)


def _dsa(qiT3, wgT3, knat3, qT3, vT3, batch, seq):
    tq = DSA_TILE
    nq = seq // tq
    n = batch * seq
    topk = min(INDEX_TOPK_MAX, seq // 4)
    cur = lambda r: pl.BlockSpec((1, r, tq), lambda b, j: (b * nq + jnp.minimum(j, nq - 1), 0, 0))
    prev = lambda r: pl.BlockSpec((1, r, tq), lambda b, j: (b * nq + jnp.maximum(j - 1, 0), 0, 0))
    return pl.pallas_call(
        functools.partial(_dsa_kernel, topk=topk),
        grid=(batch, nq + 1),
        in_specs=[cur(512), cur(16),
                  pl.BlockSpec((nq, tq, 256), lambda b, j: (b, 0, 0)),
                  cur(512), prev(512),
                  pl.BlockSpec((nq, 128, tq), lambda b, j: (b, 0, 0))],
        out_specs=pl.BlockSpec((tq, 512), lambda b, j: (b * nq + jnp.maximum(j - 1, 0), 0)),
        out_shape=jax.ShapeDtypeStruct((n, 512), MXU_DTYPE),
        scratch_shapes=[pltpu.VMEM((nq, tq, tq), I32), pltpu.VMEM((nq, tq, tq), jnp.bfloat16),
                        pltpu.VMEM((2, nq, tq, tq), F32),
                        pltpu.VMEM((ATT_KV_HEADS, HEAD_DIM + 16, ATT_GROUP * tq), F32),
                        pltpu.VMEM((8, tq), I32), pltpu.VMEM((8, tq), I32), pltpu.VMEM((8, LANE), F32)],
        compiler_params=_params("parallel", "arbitrary"),
        name="dsa",
    )(qiT3, wgT3, knat3, qT3, qT3, vT3)


def _hgrn_kernel(hg_ref, lb_ref, gn_ref, o_ref, st_scr, q_scr, k_scr, g_scr, qg_scr, kd_scr, o_scr, z_scr,
                 snap_scr):
    rows = hg_ref.shape[0]
    c = HG_SUB
    w = HG_WIDTH

    @pl.when(pl.program_id(1) == 0)
    def _():
        st_scr[...] = jnp.zeros(st_scr.shape, F32)
        z_scr[...] = jnp.zeros(z_scr.shape, z_scr.dtype)

    lb = lb_ref[...]
    q = _silu(hg_ref[:, 0:w]) * (HEAD_DIM ** -0.5)
    forget = lb + (1.0 - lb) * _sigmoid(hg_ref[:, w:2 * w])
    logf = jnp.log2(forget)
    kk = 1.0 - forget
    v = hg_ref[:, 2 * w:3 * w]

    ri = _iota((rows, rows), 0)
    ci = _iota((rows, rows), 1)
    same = (ri // c) == (ci // c)
    tri = _mx(jnp.where(same & (ci <= ri), 1.0, 0.0))
    ones_blk = _mx(jnp.where(same, 1.0, 0.0))
    gl = _dot01_left(tri, logf)
    glast = _dot01_left(ones_blk, logf)
    q_scr[...] = q
    k_scr[...] = gl - jnp.log2(kk)
    g_scr[...] = gl
    qg_scr[...] = q * jnp.exp2(gl)
    kd_scr[...] = kk * jnp.exp2(glast - gl)
    dec = jnp.exp2(glast)
    vT = _mx(v.T)
    head_ones = _mx(_block_ones(w, HEAD_DIM))
    pair_mask = _block_ones(LANE, HEAD_DIM)
    half = c // 2
    rhalf = _iota((half, w), 0)
    nsc = rows // c

    for sc in range(nsc):
        z_scr[sc, sc * c:(sc + 1) * c, :] = kd_scr[sc * c:(sc + 1) * c, :].astype(z_scr.dtype)
    for p in range(HG_HEADS // 2):
        ps = slice(p * LANE, (p + 1) * LANE)
        st = st_scr[p]
        for sc in range(nsc):
            snap_scr[sc, p] = st.astype(snap_scr.dtype)
            upd = jnp.dot(vT[ps, :], z_scr[sc, :, ps], preferred_element_type=F32)
            st = st * dec[sc * c:sc * c + 1, ps] + upd * pair_mask
        st_scr[p] = st

    for sc in range(nsc):
        r0 = sc * c
        q_top, q_bot = q_scr[r0:r0 + half, :], q_scr[r0 + half:r0 + c, :]
        g_top, g_bot = g_scr[r0:r0 + half, :], g_scr[r0 + half:r0 + c, :]
        bcast = lambda row: jnp.broadcast_to(row, (half, w))
        xs_top, xs_bot = [], []
        for s in range(c):
            hsr = bcast(k_scr[r0 + s:r0 + s + 1, :])
            if s < half:
                xs_top.append(q_top * jnp.exp2(jnp.where(rhalf >= s, g_top - hsr, -jnp.inf)))
                xs_bot.append(q_bot * jnp.exp2(g_bot - hsr))
            else:
                xs_bot.append(q_bot * jnp.exp2(jnp.where(rhalf >= s - half, g_bot - hsr, -jnp.inf)))
        y = jnp.dot(_mx(jnp.concatenate(xs_top + xs_bot, axis=0)), head_ones, preferred_element_type=F32)
        o_top = jnp.zeros((half, w), F32)
        o_bot = jnp.zeros((half, w), F32)
        for s in range(c):
            vsr = bcast(hg_ref[r0 + s:r0 + s + 1, 2 * w:3 * w])
            if s < half:
                o_top = o_top + y[s * half:(s + 1) * half, :] * vsr
            o_bot = o_bot + y[(half + s) * half:(half + s + 1) * half, :] * vsr
        o_diag = jnp.concatenate([o_top, o_bot], axis=0)
        o_pairs = [_dot_nt(qg_scr[r0:r0 + c, p * LANE:(p + 1) * LANE], snap_scr[sc, p])
                   for p in range(HG_HEADS // 2)]
        o_scr[r0:r0 + c, :] = jnp.concatenate(o_pairs, axis=1) + o_diag

    o = o_scr[...]
    ms = _dot01_right(o * o, head_ones) * (1.0 / HEAD_DIM)
    y = o * lax.rsqrt(ms + RMS_EPS) * gn_ref[...]
    o_ref[...] = (y * _silu(hg_ref[:, 3 * w:4 * w])).astype(o_ref.dtype)


def _hgrn(hg, lb, gn, batch, seq):
    rows = HG_ROWS
    ns = seq // rows
    n = batch * seq
    vm = lambda: pltpu.VMEM((rows, HG_WIDTH), F32)
    return pl.pallas_call(
        _hgrn_kernel,
        grid=(batch, ns),
        in_specs=[pl.BlockSpec((rows, 1024), lambda b, s: (b * ns + s, 0)),
                  _const_spec((1, HG_WIDTH)), _const_spec((1, HG_WIDTH))],
        out_specs=pl.BlockSpec((rows, HG_WIDTH), lambda b, s: (b * ns + s, 0)),
        out_shape=jax.ShapeDtypeStruct((n, HG_WIDTH), MXU_DTYPE),
        scratch_shapes=[pltpu.VMEM((HG_HEADS // 2, LANE, LANE), F32), vm(), vm(), vm(), vm(), vm(), vm(),
                        pltpu.VMEM((rows // HG_SUB, rows, HG_WIDTH), MXU_DTYPE),
                        pltpu.VMEM((rows // HG_SUB, HG_HEADS // 2, LANE, LANE), MXU_DTYPE)],
        compiler_params=_params("parallel", "arbitrary"),
        name="hgrn",
    )(hg, lb, gn)


def _mlstm_kernel(ml_ref, *refs):
    gT_refs = refs[:ML_SEQS]
    cw_ref, cb_ref, brow_ref, gn_ref, o_ref, prev_scr, c_scr, n_scr, m64_scr, m128_scr = refs[ML_SEQS:]

    @pl.when(pl.program_id(1) == 0)
    def _():
        prev_scr[...] = jnp.zeros(prev_scr.shape, F32)
        c_scr[...] = jnp.zeros(c_scr.shape, F32)
        n_scr[...] = jnp.zeros(n_scr.shape, F32)
        m64_scr[...] = jnp.full(m64_scr.shape, M_INIT, F32)
        m128_scr[...] = jnp.full(m128_scr.shape, M_INIT, F32)

    seqs = [_mlstm_sequence(ml_ref.at[bi], gT_ref, cw_ref, cb_ref, brow_ref, gn_ref, o_ref.at[bi],
                            prev_scr.at[bi], c_scr.at[bi], n_scr.at[bi], m64_scr.at[bi], m128_scr.at[bi])
            for bi, gT_ref in enumerate(gT_refs)]
    while seqs:
        for seq in list(seqs):
            if next(seq, "done") == "done":
                seqs.remove(seq)


def _mlstm_sequence(ml_ref, gT_ref, cw_ref, cb_ref, brow_ref, gn_ref, o_ref,
                    prev_scr, c_scr, n_scr, m64_scr, m128_scr):
    L = ML_CHUNK
    w = ML_WIDTH

    xqk = ml_ref[:, 0:512]
    prev = prev_scr[...]
    rows = _iota((L, 512), 0)
    acc = cb_ref[...] + cw_ref[ML_CONV - 1:ML_CONV, :] * xqk
    for sh in range(1, ML_CONV):
        shifted = jnp.where(rows >= sh, pltpu.roll(xqk, sh, 0), pltpu.roll(prev, sh, 0))
        acc = acc + cw_ref[ML_CONV - 1 - sh:ML_CONV - sh, :] * shifted
    prev_scr[...] = xqk
    qk = _silu(acc)
    q = qk[:, 0:w]
    k = qk[:, w:2 * w] * (HEAD_DIM ** -0.5)
    v = ml_ref[:, 512:768]

    srow = _iota((L, L), 0)
    scol = _iota((L, L), 1)
    tril = srow >= scol
    triu = _mx(jnp.where(srow <= scol, 1.0, 0.0))
    grow = gT_ref[0] + brow_ref[...]
    brow = _dot01_right(_log_sigmoid(grow), triu)
    yield

    def expand(x, width, row0):
        src = _iota((16, ML_HEADS * width), 0)
        dst = _iota((16, ML_HEADS * width), 1)
        e01 = _mx(jnp.where(src == row0 + dst // width, 1.0, 0.0))
        f = lambda p: lax.dot_general(p, e01, (((0,), (0,)), ((), ())), preferred_element_type=F32)
        hi, mid, lo = _split3(x)
        return f(hi) + f(mid) + f(lo)

    b128 = expand(brow, 128, 12)
    i128 = expand(grow, 128, 8)
    b64 = expand(brow, 64, 12)
    i64 = expand(grow, 64, 8)
    yield

    lane = _iota((L, LANE), 1)
    first = lane < HEAD_DIM
    pair_ones = _mx(_block_ones(LANE, HEAD_DIM))
    pair_mask = _block_ones(LANE, HEAD_DIM)
    m_prev128 = m128_scr[0:1, :]
    m_prev64 = m64_scr[0:1, :]
    n_row = n_scr[0:1, :]

    houts = []
    for p in range(ML_HEADS // 2):
        ps = slice(p * LANE, (p + 1) * LANE)
        q_pair, k_pair, v_pair = q[:, ps], k[:, ps], v[:, ps]
        per_head = []
        for hh in range(2):
            hd = 2 * p + hh
            hs = slice(hd * 128, (hd + 1) * 128)
            bt = b128[:, hs]
            d = bt - brow[12 + hd:13 + hd, :] + grow[8 + hd:9 + hd, :]
            d = jnp.where(tril, d, -jnp.inf)
            inter = bt + m_prev128[:, hs]
            m_t = jnp.maximum(inter, jnp.max(d, axis=1, keepdims=True))
            wgt = jnp.exp(d - m_t)
            a = jnp.exp(inter - m_t)
            qm = jnp.where(first if hh == 0 else ~first, q_pair, 0.0)
            s = _dot_nt(qm, k_pair) * wgt
            sv = _dot(s, v_pair)
            rs = jnp.sum(s, axis=1, keepdims=True)
            per_head.append((m_t, a, sv, rs))
            yield
        sel = lambda i: jnp.where(first, per_head[0][i], per_head[1][i])
        m_t, a, sv, rs = sel(0), sel(1), sel(2), sel(3)
        qc = _dot(q_pair, c_scr[p])
        qn = _dot(q_pair * n_row[:, ps], pair_ones)
        num = a * qc + sv
        den = a * qn + rs
        houts.append(num / jnp.maximum(jnp.abs(den), jnp.exp(-m_t)))
        yield
    hout = jnp.concatenate(houts, axis=1)

    def new_m(bx, ix, m_prev):
        blast = bx[L - 1:L, :]
        wl = blast - bx + ix
        m_new = jnp.maximum(blast + m_prev, jnp.max(wl, axis=0, keepdims=True))
        return blast, wl, m_new

    blast, wl, m_new64 = new_m(b64, i64, m_prev64)
    dec = jnp.exp(blast + m_prev64 - m_new64)
    kw = k * jnp.exp(wl - m_new64)
    for p in range(ML_HEADS // 2):
        ps = slice(p * LANE, (p + 1) * LANE)
        upd = jnp.dot(_mx(kw[:, ps].T), _mx(v[:, ps]), preferred_element_type=F32)
        c_scr[p] = c_scr[p] * dec[:, ps] + upd * pair_mask
    n_scr[...] = jnp.broadcast_to(dec * n_row + jnp.sum(kw, axis=0, keepdims=True), n_scr.shape)
    m64_scr[...] = jnp.broadcast_to(m_new64, m64_scr.shape)
    _, _, m_new128 = new_m(b128, i128, m_prev128)
    m128_scr[...] = jnp.broadcast_to(m_new128, m128_scr.shape)

    head_ones = _mx(_block_ones(w, HEAD_DIM))
    ms = _dot01_right(hout * hout, head_ones) * (1.0 / HEAD_DIM)
    y = hout * lax.rsqrt(ms + RMS_EPS) * gn_ref[...]
    o_ref[...] = (y * _sigmoid(ml_ref[:, 768:1024])).astype(o_ref.dtype)


def _mlstm(ml, wgT3, cw, cb, brow, gn, batch, seq):
    L = ML_CHUNK
    nc = seq // L
    n = batch * seq
    per_tile = DSA_TILE // L
    nb = ML_SEQS
    assert batch % nb == 0

    def gate_spec(off):
        def index(b, s):
            chunk = (nb * b + off) * nc + s
            return chunk // per_tile, 0, chunk % per_tile
        return pl.BlockSpec((1, 16, L), index)

    out = pl.pallas_call(
        _mlstm_kernel,
        grid=(batch // nb, nc),
        in_specs=[pl.BlockSpec((nb, L, 1024), lambda b, s: (b, s, 0))] + [gate_spec(o) for o in range(nb)]
                 + [_const_spec((ML_CONV, 512)), _const_spec((1, 512)),
                    _const_spec((16, L)), _const_spec((1, ML_WIDTH))],
        out_specs=pl.BlockSpec((nb, L, ML_WIDTH), lambda b, s: (b, s, 0)),
        out_shape=jax.ShapeDtypeStruct((batch, seq, ML_WIDTH), MXU_DTYPE),
        scratch_shapes=[pltpu.VMEM((nb, L, 512), F32), pltpu.VMEM((nb, 2, LANE, LANE), F32),
                        pltpu.VMEM((nb, 8, ML_WIDTH), F32), pltpu.VMEM((nb, 8, ML_WIDTH), F32),
                        pltpu.VMEM((nb, 8, 512), F32)],
        compiler_params=_params("parallel", "arbitrary"),
        name="mlstm",
    )(ml.reshape(batch, seq, 1024), *([wgT3] * nb), cw, cb, brow, gn)
    return out.reshape(n, ML_WIDTH)


def _rms(y, g):
    ms = jnp.mean(y * y, axis=-1, keepdims=True)
    return y * lax.rsqrt(ms + RMS_EPS) * g


def _mem_kv_kernel(m_ref, g_ref, w_ref, k_ref, v_ref):
    h = _mx(_rms(m_ref[...], g_ref[...]))
    kv = jnp.dot(h, w_ref[...], preferred_element_type=F32)
    k_ref[...] = kv[:, 0:D_MODEL].astype(k_ref.dtype)
    v_ref[...] = kv[:, D_MODEL:2 * D_MODEL].astype(v_ref.dtype)


def _mem_kv(mem2, g, w):
    n = mem2.shape[0]
    tm = ROW_TILE
    row = pl.BlockSpec((tm, D_MODEL), lambda i: (i, 0))
    return pl.pallas_call(
        _mem_kv_kernel,
        grid=(n // tm,),
        in_specs=[row, _const_spec((1, D_MODEL)), _const_spec((D_MODEL, 2 * D_MODEL))],
        out_specs=[row, row],
        out_shape=[jax.ShapeDtypeStruct((n, D_MODEL), MXU_DTYPE)] * 2,
        compiler_params=_params("parallel"),
        name="mem_kv",
    )(mem2, g, w)


def _post_mix_kernel(a_ref, b_ref, c_ref, x_ref, k_ref, v_ref, wout_ref, wq_ref, wo_ref, wu_ref, wd_ref,
                     g_ref, o_ref):
    g = g_ref[...]
    y = (jnp.dot(a_ref[...], wout_ref[0:512, :], preferred_element_type=F32)
         + jnp.dot(b_ref[...], wout_ref[512:768, :], preferred_element_type=F32)
         + jnp.dot(c_ref[...], wout_ref[768:1024, :], preferred_element_type=F32))
    x = x_ref[...] + _rms(y, g[0:1])
    h = _mx(_rms(x, g[1:2]))
    q = jnp.dot(h, wq_ref[...], preferred_element_type=F32) * (CROSS_HEAD_DIM ** -0.5)
    outs = []
    for hd in range(CROSS_HEADS):
        hs = slice(hd * CROSS_HEAD_DIM, (hd + 1) * CROSS_HEAD_DIM)
        logits = _dot_nt(q[:, hs], k_ref[:, hs])
        p = jnp.exp(logits - jnp.max(logits, axis=-1, keepdims=True))
        o = jnp.dot(_mx(p), v_ref[:, hs], preferred_element_type=F32)
        outs.append(o / jnp.sum(p, axis=-1, keepdims=True))
    y = jnp.dot(_mx(jnp.concatenate(outs, axis=1)), wo_ref[...], preferred_element_type=F32)
    x = x + _rms(y, g[2:3])
    h = _mx(_rms(x, g[3:4]))
    y = jnp.zeros(x.shape, F32)
    for c in range(MLP_HIDDEN // D_MODEL):
        cs = slice(c * D_MODEL, (c + 1) * D_MODEL)
        u = jnp.maximum(jnp.dot(h, wu_ref[:, cs], preferred_element_type=F32), 0.0)
        y = y + jnp.dot(_mx(u * u), wd_ref[cs, :], preferred_element_type=F32)
    o_ref[...] = x + _rms(y, g[4:5])


def _post_mix(a, b, c, x2, kmem, vmem, wout, wq, wo, wu, wd, gains, seq, n_mem):
    n = x2.shape[0]
    tm = POST_TILE
    per_b = seq // tm
    row = lambda width: pl.BlockSpec((tm, width), lambda i: (i, 0))
    memb = pl.BlockSpec((n_mem, D_MODEL), lambda i: (i // per_b, 0))
    sq = _const_spec((D_MODEL, D_MODEL))
    return pl.pallas_call(
        _post_mix_kernel,
        grid=(n // tm,),
        in_specs=[row(512), row(256), row(256), row(D_MODEL), memb, memb, sq, sq, sq,
                  _const_spec((D_MODEL, MLP_HIDDEN)), _const_spec((MLP_HIDDEN, D_MODEL)),
                  _const_spec((8, D_MODEL))],
        out_specs=row(D_MODEL),
        out_shape=jax.ShapeDtypeStruct((n, D_MODEL), F32),
        compiler_params=_params("parallel"),
        name="post_mix",
    )(a, b, c, x2, kmem, vmem, wout, wq, wo, wu, wd, gains)


def _rope_tables(seq):
    half = HEAD_DIM // 2
    inv = ROPE_THETA ** (-jnp.arange(0, HEAD_DIM, 2, dtype=F32) / HEAD_DIM)
    ang = jnp.arange(seq).astype(F32)[:, None] * inv[None, :]
    cos, sin = jnp.cos(ang), jnp.sin(ang)
    lane = jnp.arange(LANE)
    freq = (lane % HEAD_DIM) % half
    first = (lane % HEAD_DIM) < half
    cosn = cos[:, freq]
    sinn = sin[:, freq]
    sina = jnp.where(first[None, :], -sinn, 0.0)
    sinb = jnp.where(first[None, :], 0.0, sinn)
    return cosn, sina, sinb, cos.T, sin.T


def _layout_w_in(w_in):
    depth = w_in.shape[0]
    z = lambda wd: jnp.zeros((depth, D_MODEL, wd), w_in.dtype)
    col = lambda a, b: w_in[:, :, a:b]
    wn = jnp.concatenate([col(_O_AK, _O_AV), col(_O_XK, _O_XW), z(64),
                          col(_O_HQ, _O_MQK), col(_O_MQK, _O_MG)], axis=-1)
    wt = jnp.concatenate([col(_O_AQ, _O_AK), col(_O_XQ, _O_XK), col(_O_AV, _O_XQ),
                          col(_O_XW, _O_HQ), col(_O_MG, _IN_WIDTH)], axis=-1)
    return _mx(wn), _mx(jnp.swapaxes(wt, 1, 2))


def kernel(x, mem, mix_pre_g, w_in, ml_conv_w, ml_conv_b, ml_gate_b, hg_lb, hg_norm_g, ml_norm_g,
           w_out, mix_post_g, cross_pre_g, mem_norm_g, w_cq, w_ckv, w_co, cross_post_g, mlp_pre_g,
           w_up, w_down, mlp_post_g):
    batch, seq, _ = x.shape
    n_mem = mem.shape[1]
    depth = w_in.shape[0]
    n = batch * seq
    assert seq % max(ROW_TILE, POST_TILE, HG_ROWS, ML_CHUNK, DSA_TILE) == 0
    assert (batch * n_mem) % ROW_TILE == 0
    assert ROW_TILE % DSA_TILE == 0 and DSA_TILE % ML_CHUNK == 0

    lbs = jnp.cumsum(jax.nn.softmax(hg_lb.astype(F32), axis=0), axis=0)
    lbs = lbs - lbs[0:1]

    tabs = _rope_tables(seq)
    wn_all, wt_all = _layout_w_in(w_in)
    row = lambda a: a.astype(F32)[None, :]
    brow = jnp.pad(ml_gate_b.astype(F32), ((0, 0), (8, 0)))

    x2 = x.reshape(n, D_MODEL)
    mem2 = mem.reshape(batch * n_mem, D_MODEL)
    nl = n // DSA_TILE
    for l in range(depth):
        knat, hg, ml, qT3, qiT3, vT3, wgT3 = _in_proj(x2, row(mix_pre_g[l]), wn_all[l], wt_all[l], tabs, seq)
        a_out = _dsa(qiT3, wgT3, knat.reshape(nl, DSA_TILE, 256), qT3, vT3, batch, seq)
        b_out = _hgrn(hg, row(lbs[l]), row(jnp.tile(hg_norm_g[l], HG_HEADS)), batch, seq)
        c_out = _mlstm(ml, wgT3, ml_conv_w[l].astype(F32), row(ml_conv_b[l]),
                       jnp.broadcast_to(brow[l][:, None], (16, ML_CHUNK)),
                       row(jnp.tile(ml_norm_g[l], ML_HEADS)), batch, seq)
        kmem, vmem = _mem_kv(mem2, row(mem_norm_g[l]), _mx(w_ckv[l]))
        gains = jnp.stack([mix_post_g[l], cross_pre_g[l], cross_post_g[l], mlp_pre_g[l], mlp_post_g[l]]
                          + [jnp.zeros_like(mix_post_g[l])] * 3).astype(F32)
        x2 = _post_mix(a_out, b_out, c_out, x2, kmem, vmem, _mx(w_out[l]), _mx(w_cq[l]), _mx(w_co[l]),
                       _mx(w_up[l]), _mx(w_down[l]), gains, seq, n_mem)
    return x2.reshape(batch, seq, D_MODEL)
```

```python
import functools

import jax
import jax.numpy as jnp
from jax import lax
from jax.experimental import pallas as pl
from jax.experimental.pallas import tpu as pltpu

F32 = jnp.float32
I32 = jnp.int32
MXU_DTYPE = jnp.bfloat16

D_MODEL = 1024
RMS_EPS = 1e-6
ROPE_THETA = 10000.0
NEG_BIG = -1e30
M_INIT = -1e30
LOG2E = 1.4426950408889634
F32_MIN_NORMAL = 1.1754943508222875e-38
BOUND_MARGIN = 1.05
SAFE_LOGIT_BOUND = 60.0
HEAD_DIM = 64
ATT_HEADS = 8
ATT_KV_HEADS = 2
ATT_GROUP = ATT_HEADS // ATT_KV_HEADS
IDX_HEADS = 8
INDEX_TOPK_MAX = 256
HG_HEADS = 4
HG_WIDTH = HG_HEADS * HEAD_DIM
ML_HEADS = 4
ML_WIDTH = ML_HEADS * HEAD_DIM
ML_CONV = 4
CROSS_HEADS = 4
CROSS_HEAD_DIM = D_MODEL // CROSS_HEADS
MLP_HIDDEN = 4 * D_MODEL

_O_AQ, _O_AK, _O_AV, _O_XQ, _O_XK, _O_XW = 0, 512, 640, 768, 1280, 1344
_O_HQ, _O_MQK, _O_MV, _O_MO, _O_MG, _IN_WIDTH = 1352, 2376, 2888, 3144, 3400, 3408

NAT_WIDTH = 128 + 128 + 1024 + 1024
TR_WIDTH = 512 + 512 + 128 + 16

LANE = 128
DSA_TILE = 256
ROW_TILE = 256
POST_TILE = 512
HG_SUB = 16
HG_ROWS = 256
HG_SEQS = 4
ML_CHUNK = 128
ML_SEQS = 4
VMEM_LIMIT = 56 * 1024 * 1024

_NT = (((1,), (1,)), ((), ()))


def _mx(a):
    return a.astype(MXU_DTYPE)


def _dot(a, b):
    return jnp.dot(_mx(a), _mx(b), preferred_element_type=F32)


def _dot_nt(a, b):
    return lax.dot_general(_mx(a), _mx(b), _NT, preferred_element_type=F32)


def _split3(x):
    hi = _mx(x)
    r1 = x - hi.astype(F32)
    mid = _mx(r1)
    lo = _mx(r1 - mid.astype(F32))
    return hi, mid, lo


def _dot01_left(m01, x):
    hi, mid, lo = _split3(x)
    f = lambda p: jnp.dot(m01, p, preferred_element_type=F32)
    return f(hi) + f(mid) + f(lo)


def _dot01_right(x, m01):
    hi, mid, lo = _split3(x)
    f = lambda p: jnp.dot(p, m01, preferred_element_type=F32)
    return f(hi) + f(mid) + f(lo)


def _sigmoid(x):
    return 1.0 / (1.0 + jnp.exp(-x))


def _silu(x):
    return x * _sigmoid(x)


def _log_sigmoid(x):
    return jnp.minimum(x, 0.0) - jnp.log(1.0 + jnp.exp(-jnp.abs(x)))


def _iota(shape, dim):
    return lax.broadcasted_iota(I32, shape, dim)


def _block_ones(n, blk):
    same = (_iota((n, n), 0) // blk) == (_iota((n, n), 1) // blk)
    return jnp.where(same, 1.0, 0.0)


def _params(*sem):
    return pltpu.CompilerParams(dimension_semantics=sem, vmem_limit_bytes=VMEM_LIMIT)


def _const_spec(shape):
    nd = len(shape)
    return pl.BlockSpec(shape, lambda *_: (0,) * nd, pipeline_mode=pl.Buffered(1))


def _in_proj_kernel(x_ref, g_ref, wn_ref, wt_ref, cosn_ref, sina_ref, sinb_ref, cost_ref, sint_ref,
                    knat_ref, hg_ref, ml_ref, qT_ref, qiT_ref, vT_ref, wgT_ref, *, idx_scale):
    tm = x_ref.shape[0]
    x = x_ref[...]
    ms = jnp.mean(x * x, axis=-1, keepdims=True)
    h = _mx(x * lax.rsqrt(ms + RMS_EPS) * g_ref[...])
    nat = jnp.dot(h, wn_ref[...], preferred_element_type=F32)
    tr = lax.dot_general(wt_ref[...], h, _NT, preferred_element_type=F32)

    cosn, sina, sinb = cosn_ref[...], sina_ref[...], sinb_ref[...]

    def rope_nat(z):
        return z * cosn + pltpu.roll(z, 96, 1) * sina + pltpu.roll(z, 32, 1) * sinb

    knat_ref[:, 0:128] = rope_nat(nat[:, 0:128]).astype(knat_ref.dtype)
    knat_ref[:, 128:256] = rope_nat(nat[:, 128:256]).astype(knat_ref.dtype)
    hg_ref[...] = nat[:, 256:1280]
    ml_ref[...] = nat[:, 1280:NAT_WIDTH]

    cost, sint = cost_ref[...], sint_ref[...]

    def rope_t(z, scale):
        outs = []
        for hd in range(8):
            x1 = z[hd * 64:hd * 64 + 32]
            x2 = z[hd * 64 + 32:hd * 64 + 64]
            outs.append((x1 * cost - x2 * sint) * scale)
            outs.append((x2 * cost + x1 * sint) * scale)
        return jnp.concatenate(outs, axis=0)

    qT = rope_t(tr[0:512], (HEAD_DIM ** -0.5) * LOG2E).astype(qT_ref.dtype)
    qiT = rope_t(tr[512:1024], 1.0).astype(qiT_ref.dtype)
    vT = tr[1024:1152].astype(vT_ref.dtype)
    wg = tr[1152:1168]
    wg = jnp.where(_iota(wg.shape, 0) < 8, wg * idx_scale, wg)
    for c in range(tm // DSA_TILE):
        sl = slice(c * DSA_TILE, (c + 1) * DSA_TILE)
        qT_ref[c] = qT[:, sl]
        qiT_ref[c] = qiT[:, sl]
        vT_ref[c] = vT[:, sl]
        wgT_ref[c] = wg[:, sl]


def _in_proj(x2, g, wn, wt, tabs, seq):
    n = x2.shape[0]
    tm = ROW_TILE
    nt_seq = seq // tm
    cosn, sina, sinb, cost, sint = tabs
    idx_scale = (IDX_HEADS ** -0.5) * (HEAD_DIM ** -0.5)
    row = lambda w: pl.BlockSpec((tm, w), lambda i: (i, 0))
    tabn = pl.BlockSpec((tm, LANE), lambda i: (i % nt_seq, 0))
    tabt = pl.BlockSpec((32, tm), lambda i: (0, i % nt_seq))
    t3 = lambda r: pl.BlockSpec((tm // DSA_TILE, r, DSA_TILE), lambda i: (i, 0, 0))
    nl = n // DSA_TILE
    return pl.pallas_call(
        functools.partial(_in_proj_kernel, idx_scale=idx_scale),
        grid=(n // tm,),
        in_specs=[row(D_MODEL), _const_spec((1, D_MODEL)), _const_spec((D_MODEL, NAT_WIDTH)),
                  _const_spec((TR_WIDTH, D_MODEL)), tabn, tabn, tabn, tabt, tabt],
        out_specs=[row(256), row(1024), row(1024), t3(512), t3(512), t3(128), t3(16)],
        out_shape=[jax.ShapeDtypeStruct((n, 256), MXU_DTYPE),
                   jax.ShapeDtypeStruct((n, 1024), F32),
                   jax.ShapeDtypeStruct((n, 1024), F32),
                   jax.ShapeDtypeStruct((nl, 512, DSA_TILE), MXU_DTYPE),
                   jax.ShapeDtypeStruct((nl, 512, DSA_TILE), MXU_DTYPE),
                   jax.ShapeDtypeStruct((nl, 128, DSA_TILE), MXU_DTYPE),
                   jax.ShapeDtypeStruct((nl, 16, DSA_TILE), F32)],
        compiler_params=_params("parallel"),
        name="in_proj",
    )(x2, g, wn, wt, cosn, sina, sinb, cost, sint)


def _dsa_kernel(qiT_ref, wgT_ref, knat_ref, qT_ref, vT_ref, o_ref,
                key_scr, hi_scr, bias_scr, acc_scr, thr_scr, nge_scr, kmax_scr, *, topk):
    tq = DSA_TILE
    j = pl.program_id(1)
    nk = j + 1
    int_min = jnp.int32(-2 ** 31)
    srow = _iota((tq, tq), 0)
    tcol = _iota((tq, tq), 1)
    causal = srow <= tcol

    w = wgT_ref[0]
    qi = qiT_ref[0]
    qi_wide = jnp.concatenate([qi[hd * 64:(hd + 1) * 64] for hd in range(IDX_HEADS)], axis=1)

    def score_tile(kt, carry):
        kblk = knat_ref[kt][:, 128:192]
        acc = jnp.zeros((tq, tq), F32)
        ys = jnp.dot(kblk, qi_wide, preferred_element_type=F32)
        for hd in range(IDX_HEADS):
            acc = acc + w[hd:hd + 1, :] * jnp.maximum(ys[:, hd * tq:(hd + 1) * tq], 0.0)
        acc = jnp.where(jnp.abs(acc) < F32_MIN_NORMAL, 0.0, acc)
        acc = jnp.where(jnp.logical_or(kt < j, causal), acc, -jnp.inf)
        bits = pltpu.bitcast(acc, I32)
        key_scr[kt] = bits ^ ((bits >> 31) & jnp.int32(0x7FFFFFFF))
        hi_scr[kt] = pltpu.bitcast(bits & jnp.int32(-65536), F32).astype(jnp.bfloat16)
        return carry

    lax.fori_loop(0, nk, score_tile, 0)

    key16_neg_inf = (0xFF80 ^ 0x7FFF) - 65536
    one_h = jnp.ones((16, tq), jnp.bfloat16)
    zero_h = jnp.zeros((16, tq), jnp.bfloat16)

    def select_threshold(nks):
        def count16(cand16):
            b16 = cand16 ^ ((cand16 >> 15) & 0x7FFF)
            b16 = jnp.where((cand16 > 0) & (cand16 < 0x80), 0x80, b16)
            cb = jnp.broadcast_to(pltpu.bitcast(b16 << 16, F32).astype(jnp.bfloat16), (16, tq))
            accs = [zero_h] * 4
            n = 0
            for kt in range(nks):
                for r in range(tq // 16):
                    ind = jnp.where(hi_scr[kt, r * 16:(r + 1) * 16, :] >= cb, one_h, zero_h)
                    accs[n % 4] = accs[n % 4] + ind
                    n += 1
            tot = (accs[0] + accs[1]).astype(F32) + (accs[2] + accs[3]).astype(F32)
            return jnp.sum(tot, axis=0, keepdims=True)

        def count32(pred_fn):
            accs = [jnp.zeros((8, tq), I32)] * 4
            n = 0
            for kt in range(nks):
                for r in range(tq // 8):
                    ind = jnp.where(pred_fn(key_scr[kt, r * 8:(r + 1) * 8, :]), 1, 0).astype(I32)
                    accs[n % 4] = accs[n % 4] + ind
                    n += 1
            return jnp.sum((accs[0] + accs[1]) + (accs[2] + accs[3]), axis=0, keepdims=True)

        def stage16(i, lo):
            cand = lo + (jnp.int32(1) << (15 - i))
            ok = (count16(cand) >= topk) | (cand <= key16_neg_inf)
            return jnp.where(ok, cand, lo)

        lo16 = lax.fori_loop(0, 16, stage16, jnp.full((1, tq), -32768, I32))

        def stage32(i, carry):
            lo, _ = carry
            cand = lo + (jnp.int32(1) << (15 - i))
            cnt = count32(lambda k: k >= cand)
            ok = cnt >= topk
            return jnp.where(ok, cand, lo), jnp.where(ok, cnt, carry[1])

        thr, n_ge = lax.fori_loop(0, 16, stage32, (lo16 << 16, jnp.full((1, tq), nks * tq, I32)))
        thr_scr[...] = jnp.broadcast_to(thr, thr_scr.shape)
        nge_scr[...] = jnp.broadcast_to(n_ge, nge_scr.shape)

    for jj in range(key_scr.shape[0]):
        @pl.when(j == jj)
        def _(jj=jj):
            if (jj + 1) * tq <= topk:
                thr_scr[...] = jnp.full(thr_scr.shape, int_min, I32)
                nge_scr[...] = jnp.full(nge_scr.shape, topk, I32)
            else:
                select_threshold(jj + 1)

    thr = thr_scr[0:1, :]
    n_ge = nge_scr[0:1, :]

    q = qT_ref[0]

    @pl.when(j == 0)
    def _():
        group_ones = _mx(_block_ones(LANE, HEAD_DIM))

        def body(kt, m):
            k = knat_ref[kt][:, 0:128].astype(F32)
            n2 = jnp.dot(_mx(k * k), group_ones, preferred_element_type=F32)
            return jnp.maximum(m, jnp.max(n2, axis=0, keepdims=True))

        kmax2 = lax.fori_loop(0, key_scr.shape[0], body, jnp.zeros((1, LANE), F32))
        kmax_scr[...] = jnp.broadcast_to(kmax2, kmax_scr.shape)

    qf = q.astype(F32)
    bound2 = jnp.zeros((1, tq), F32)
    for hd in range(ATT_HEADS):
        g = hd // ATT_GROUP
        qn2 = jnp.sum(qf[hd * 64:(hd + 1) * 64] * qf[hd * 64:(hd + 1) * 64], axis=0, keepdims=True)
        bound2 = jnp.maximum(bound2, qn2 * kmax_scr[0:1, g * HEAD_DIM:g * HEAD_DIM + 1])
    bound = BOUND_MARGIN * jnp.sqrt(bound2)
    bound_is_small = jnp.max(bound) < SAFE_LOGIT_BOUND
    neg_bound = -bound

    def mask_plain():
        def body(kt, carry):
            bias_scr[kt] = jnp.where(key_scr[kt] >= thr, neg_bound, NEG_BIG)
            return carry
        lax.fori_loop(0, nk, body, 0)

    def mask_ties():
        def count_gt(kt, c):
            ind = jnp.where(key_scr[kt] > thr, 1.0, 0.0)
            return c + jnp.sum(ind, axis=0, keepdims=True)
        n_gt = lax.fori_loop(0, nk, count_gt, jnp.zeros((1, tq), F32))
        need = topk - n_gt
        tri = _mx(jnp.where(srow >= tcol, 1.0, 0.0))

        def body(kt, off):
            k = key_scr[kt]
            eq = k == thr
            eqf = jnp.where(eq, 1.0, 0.0)
            rank = jnp.dot(tri, _mx(eqf), preferred_element_type=F32) + off
            tie_bias = jnp.where(rank <= need, neg_bound, NEG_BIG)
            bias_scr[kt] = jnp.where(k > thr, neg_bound, jnp.where(eq, tie_bias, NEG_BIG))
            return off + jnp.sum(eqf, axis=0, keepdims=True)
        lax.fori_loop(0, nk, body, jnp.zeros((1, tq), F32))

    no_excess_ties = jnp.max(jnp.abs(n_ge - topk)) == 0
    lax.cond(no_excess_ties, mask_plain, mask_ties)
    bias_scr[j] = jnp.where(causal, bias_scr[j], NEG_BIG)

    acc_scr[...] = jnp.zeros(acc_scr.shape, F32)
    zeros64 = jnp.zeros((64, tq), q.dtype)
    qwide = []
    for g in range(ATT_KV_HEADS):
        cols = []
        for hh in range(ATT_GROUP):
            hd = g * ATT_GROUP + hh
            qh = q[hd * 64:(hd + 1) * 64]
            cols.append(jnp.concatenate([qh, zeros64] if g == 0 else [zeros64, qh], axis=0))
        qwide.append(jnp.concatenate(cols, axis=1))

    ones_rows = jnp.ones((16, tq), q.dtype)

    def attend(kt, carry):
        ms, ls = carry
        kb = knat_ref[kt][:, 0:128]
        vtb = vT_ref[kt]
        bias = bias_scr[kt]
        lts = [jnp.dot(kb, qwide[g], preferred_element_type=F32) for g in range(ATT_KV_HEADS)]
        new_m, new_l = [], []
        for g in range(ATT_KV_HEADS):
            lt = jnp.concatenate([lts[g][:, hh * tq:(hh + 1) * tq] + bias for hh in range(ATT_GROUP)], axis=1)
            m_new = jnp.maximum(ms[g], jnp.max(lt, axis=0, keepdims=True))
            alpha = jnp.exp2(ms[g] - m_new)
            p = _mx(jnp.exp2(lt - m_new))
            v_aug = jnp.concatenate([vtb[g * 64:(g + 1) * 64], ones_rows], axis=0)
            pv = jnp.dot(v_aug, p, preferred_element_type=F32)
            new_l.append(alpha * ls[g] + pv[HEAD_DIM:HEAD_DIM + 1, :])
            acc_scr[g, 0:HEAD_DIM, :] = alpha * acc_scr[g, 0:HEAD_DIM, :] + pv[0:HEAD_DIM, :]
            new_m.append(m_new)
        return tuple(new_m), tuple(new_l)

    wide = ATT_GROUP * tq

    def run_online():
        init = (tuple(jnp.full((1, wide), NEG_BIG, F32) for _ in range(ATT_KV_HEADS)),
                tuple(jnp.zeros((1, wide), F32) for _ in range(ATT_KV_HEADS)))
        _, ls = lax.fori_loop(0, nk, attend, init)
        for g in range(ATT_KV_HEADS):
            acc_scr[g, HEAD_DIM:, :] = jnp.broadcast_to(ls[g], (16, wide))

    def attend_bounded(kt, carry):
        kb = knat_ref[kt][:, 0:128]
        vtb = vT_ref[kt]
        bias = bias_scr[kt]
        for g in range(ATT_KV_HEADS):
            lt = jnp.dot(kb, qwide[g], preferred_element_type=F32)
            p = _mx(jnp.exp2(jnp.concatenate([lt[:, hh * tq:(hh + 1) * tq] + bias
                                              for hh in range(ATT_GROUP)], axis=1)))
            v_aug = jnp.concatenate([vtb[g * 64:(g + 1) * 64], ones_rows], axis=0)
            acc_scr[g] = acc_scr[g] + jnp.dot(v_aug, p, preferred_element_type=F32)
        return carry

    def run_bounded():
        def pair(i, carry):
            attend_bounded(2 * i, carry)
            return attend_bounded(2 * i + 1, carry)

        lax.fori_loop(0, nk // 2, pair, 0)

        @pl.when(nk % 2 == 1)
        def _():
            attend_bounded(nk - 1, 0)

    lax.cond(bound_is_small, run_bounded, run_online)
    outs = []
    for g in range(ATT_KV_HEADS):
        og = acc_scr[g, 0:HEAD_DIM, :] / acc_scr[g, HEAD_DIM:HEAD_DIM + 1, :]
        outs += [og[:, hh * tq:(hh + 1) * tq] for hh in range(ATT_GROUP)]
    o_ref[...] = jnp.concatenate(outs, axis=0).T.astype(o_ref.dtype)


def _dsa(qiT3, wgT3, knat3, qT3, vT3, batch, seq):
    tq = DSA_TILE
    nq = seq // tq
    n = batch * seq
    topk = min(INDEX_TOPK_MAX, seq // 4)
    qtile = lambda r: pl.BlockSpec((1, r, tq), lambda b, j: (b * nq + j, 0, 0))
    return pl.pallas_call(
        functools.partial(_dsa_kernel, topk=topk),
        grid=(batch, nq),
        in_specs=[qtile(512), qtile(16),
                  pl.BlockSpec((nq, tq, 256), lambda b, j: (b, 0, 0)),
                  qtile(512),
                  pl.BlockSpec((nq, 128, tq), lambda b, j: (b, 0, 0))],
        out_specs=pl.BlockSpec((tq, 512), lambda b, j: (b * nq + j, 0)),
        out_shape=jax.ShapeDtypeStruct((n, 512), MXU_DTYPE),
        scratch_shapes=[pltpu.VMEM((nq, tq, tq), I32), pltpu.VMEM((nq, tq, tq), jnp.bfloat16),
                        pltpu.VMEM((nq, tq, tq), F32),
                        pltpu.VMEM((ATT_KV_HEADS, HEAD_DIM + 16, ATT_GROUP * tq), F32),
                        pltpu.VMEM((8, tq), I32), pltpu.VMEM((8, tq), I32), pltpu.VMEM((8, LANE), F32)],
        compiler_params=_params("parallel", "arbitrary"),
        name="dsa",
    )(qiT3, wgT3, knat3, qT3, vT3)


def _hgrn_kernel(hg_ref, lb_ref, gn_ref, o_ref, *scratch):
    st_scr, z_scr = scratch[0], scratch[7]

    @pl.when(pl.program_id(1) == 0)
    def _():
        st_scr[...] = jnp.zeros(st_scr.shape, F32)
        z_scr[...] = jnp.zeros(z_scr.shape, z_scr.dtype)

    seqs = [_hgrn_sequence(hg_ref.at[bi], lb_ref, gn_ref, o_ref.at[bi], *(s.at[bi] for s in scratch))
            for bi in range(HG_SEQS)]
    while seqs:
        for seq in list(seqs):
            if next(seq, "done") == "done":
                seqs.remove(seq)


def _hgrn_sequence(hg_ref, lb_ref, gn_ref, o_ref, st_scr, q_scr, k_scr, g_scr, qg_scr, kd_scr, o_scr, z_scr,
                   snap_scr):
    rows = hg_ref.shape[0]
    c = HG_SUB
    w = HG_WIDTH

    lb = lb_ref[...]
    q = _silu(hg_ref[:, 0:w]) * (HEAD_DIM ** -0.5)
    forget = lb + (1.0 - lb) * _sigmoid(hg_ref[:, w:2 * w])
    logf = jnp.log2(forget)
    kk = 1.0 - forget
    v = hg_ref[:, 2 * w:3 * w]

    ri = _iota((rows, rows), 0)
    ci = _iota((rows, rows), 1)
    same = (ri // c) == (ci // c)
    tri = _mx(jnp.where(same & (ci <= ri), 1.0, 0.0))
    ones_blk = _mx(jnp.where(same, 1.0, 0.0))
    gl = _dot01_left(tri, logf)
    glast = _dot01_left(ones_blk, logf)
    q_scr[...] = q
    k_scr[...] = gl - jnp.log2(kk)
    g_scr[...] = gl
    qg_scr[...] = q * jnp.exp2(gl)
    kd_scr[...] = kk * jnp.exp2(glast - gl)
    dec = jnp.exp2(glast)
    vT = _mx(v.T)
    head_ones = _mx(_block_ones(w, HEAD_DIM))
    pair_mask = _block_ones(LANE, HEAD_DIM)
    half = c // 2
    rhalf = _iota((half, w), 0)
    nsc = rows // c
    yield

    for sc in range(nsc):
        z_scr[sc, sc * c:(sc + 1) * c, :] = kd_scr[sc * c:(sc + 1) * c, :].astype(z_scr.dtype)
    for p in range(HG_HEADS // 2):
        ps = slice(p * LANE, (p + 1) * LANE)
        st = st_scr[p]
        for sc in range(nsc):
            snap_scr[sc, p] = st.astype(snap_scr.dtype)
            upd = jnp.dot(vT[ps, :], z_scr[sc, :, ps], preferred_element_type=F32)
            st = st * dec[sc * c:sc * c + 1, ps] + upd * pair_mask
        st_scr[p] = st
        yield

    for sc in range(nsc):
        r0 = sc * c
        q_top, q_bot = q_scr[r0:r0 + half, :], q_scr[r0 + half:r0 + c, :]
        g_top, g_bot = g_scr[r0:r0 + half, :], g_scr[r0 + half:r0 + c, :]
        bcast = lambda row: jnp.broadcast_to(row, (half, w))
        xs_top, xs_bot = [], []
        for s in range(c):
            hsr = bcast(k_scr[r0 + s:r0 + s + 1, :])
            if s < half:
                xs_top.append(q_top * jnp.exp2(jnp.where(rhalf >= s, g_top - hsr, -jnp.inf)))
                xs_bot.append(q_bot * jnp.exp2(g_bot - hsr))
            else:
                xs_bot.append(q_bot * jnp.exp2(jnp.where(rhalf >= s - half, g_bot - hsr, -jnp.inf)))
        y = jnp.dot(_mx(jnp.concatenate(xs_top + xs_bot, axis=0)), head_ones, preferred_element_type=F32)
        o_top = jnp.zeros((half, w), F32)
        o_bot = jnp.zeros((half, w), F32)
        for s in range(c):
            vsr = bcast(hg_ref[r0 + s:r0 + s + 1, 2 * w:3 * w])
            if s < half:
                o_top = o_top + y[s * half:(s + 1) * half, :] * vsr
            o_bot = o_bot + y[(half + s) * half:(half + s + 1) * half, :] * vsr
        o_diag = jnp.concatenate([o_top, o_bot], axis=0)
        o_pairs = [_dot_nt(qg_scr[r0:r0 + c, p * LANE:(p + 1) * LANE], snap_scr[sc, p])
                   for p in range(HG_HEADS // 2)]
        o_scr[r0:r0 + c, :] = jnp.concatenate(o_pairs, axis=1) + o_diag
        yield

    o = o_scr[...]
    ms = _dot01_right(o * o, head_ones) * (1.0 / HEAD_DIM)
    y = o * lax.rsqrt(ms + RMS_EPS) * gn_ref[...]
    o_ref[...] = (y * _silu(hg_ref[:, 3 * w:4 * w])).astype(o_ref.dtype)


def _hgrn(hg, lb, gn, batch, seq):
    rows = HG_ROWS
    ns = seq // rows
    n = batch * seq
    nb = HG_SEQS
    assert batch % nb == 0
    vm = lambda: pltpu.VMEM((nb, rows, HG_WIDTH), F32)
    out = pl.pallas_call(
        _hgrn_kernel,
        grid=(batch // nb, ns),
        in_specs=[pl.BlockSpec((nb, rows, 1024), lambda b, s: (b, s, 0)),
                  _const_spec((1, HG_WIDTH)), _const_spec((1, HG_WIDTH))],
        out_specs=pl.BlockSpec((nb, rows, HG_WIDTH), lambda b, s: (b, s, 0)),
        out_shape=jax.ShapeDtypeStruct((batch, seq, HG_WIDTH), MXU_DTYPE),
        scratch_shapes=[pltpu.VMEM((nb, HG_HEADS // 2, LANE, LANE), F32), vm(), vm(), vm(), vm(), vm(), vm(),
                        pltpu.VMEM((nb, rows // HG_SUB, rows, HG_WIDTH), MXU_DTYPE),
                        pltpu.VMEM((nb, rows // HG_SUB, HG_HEADS // 2, LANE, LANE), MXU_DTYPE)],
        compiler_params=_params("parallel", "arbitrary"),
        name="hgrn",
    )(hg.reshape(batch, seq, 1024), lb, gn)
    return out.reshape(n, HG_WIDTH)


def _mlstm_kernel(ml_ref, *refs):
    gT_refs = refs[:ML_SEQS]
    cw_ref, cb_ref, brow_ref, gn_ref, o_ref, prev_scr, c_scr, n_scr, m64_scr, m128_scr = refs[ML_SEQS:]

    @pl.when(pl.program_id(1) == 0)
    def _():
        prev_scr[...] = jnp.zeros(prev_scr.shape, F32)
        c_scr[...] = jnp.zeros(c_scr.shape, F32)
        n_scr[...] = jnp.zeros(n_scr.shape, F32)
        m64_scr[...] = jnp.full(m64_scr.shape, M_INIT, F32)
        m128_scr[...] = jnp.full(m128_scr.shape, M_INIT, F32)

    seqs = [_mlstm_sequence(ml_ref.at[bi], gT_ref, cw_ref, cb_ref, brow_ref, gn_ref, o_ref.at[bi],
                            prev_scr.at[bi], c_scr.at[bi], n_scr.at[bi], m64_scr.at[bi], m128_scr.at[bi])
            for bi, gT_ref in enumerate(gT_refs)]
    while seqs:
        for seq in list(seqs):
            if next(seq, "done") == "done":
                seqs.remove(seq)


def _mlstm_sequence(ml_ref, gT_ref, cw_ref, cb_ref, brow_ref, gn_ref, o_ref,
                    prev_scr, c_scr, n_scr, m64_scr, m128_scr):
    L = ML_CHUNK
    w = ML_WIDTH

    xqk = ml_ref[:, 0:512]
    prev = prev_scr[...]
    rows = _iota((L, 512), 0)
    acc = cb_ref[...] + cw_ref[ML_CONV - 1:ML_CONV, :] * xqk
    for sh in range(1, ML_CONV):
        shifted = jnp.where(rows >= sh, pltpu.roll(xqk, sh, 0), pltpu.roll(prev, sh, 0))
        acc = acc + cw_ref[ML_CONV - 1 - sh:ML_CONV - sh, :] * shifted
    prev_scr[...] = xqk
    qk = _silu(acc)
    q = qk[:, 0:w]
    k = qk[:, w:2 * w] * (HEAD_DIM ** -0.5)
    v = ml_ref[:, 512:768]

    srow = _iota((L, L), 0)
    scol = _iota((L, L), 1)
    tril = srow >= scol
    triu = _mx(jnp.where(srow <= scol, 1.0, 0.0))
    grow = gT_ref[0] + brow_ref[...]
    brow = _dot01_right(_log_sigmoid(grow), triu)
    yield

    def expand(x, width, row0):
        src = _iota((16, ML_HEADS * width), 0)
        dst = _iota((16, ML_HEADS * width), 1)
        e01 = _mx(jnp.where(src == row0 + dst // width, 1.0, 0.0))
        f = lambda p: lax.dot_general(p, e01, (((0,), (0,)), ((), ())), preferred_element_type=F32)
        hi, mid, lo = _split3(x)
        return f(hi) + f(mid) + f(lo)

    b128 = expand(brow, 128, 12)
    i128 = expand(grow, 128, 8)
    b64 = expand(brow, 64, 12)
    i64 = expand(grow, 64, 8)
    yield

    lane = _iota((L, LANE), 1)
    first = lane < HEAD_DIM
    pair_ones = _mx(_block_ones(LANE, HEAD_DIM))
    pair_mask = _block_ones(LANE, HEAD_DIM)
    m_prev128 = m128_scr[0:1, :]
    m_prev64 = m64_scr[0:1, :]
    n_row = n_scr[0:1, :]

    houts = []
    for p in range(ML_HEADS // 2):
        ps = slice(p * LANE, (p + 1) * LANE)
        q_pair, k_pair, v_pair = q[:, ps], k[:, ps], v[:, ps]
        per_head = []
        for hh in range(2):
            hd = 2 * p + hh
            hs = slice(hd * 128, (hd + 1) * 128)
            bt = b128[:, hs]
            d = bt - brow[12 + hd:13 + hd, :] + grow[8 + hd:9 + hd, :]
            d = jnp.where(tril, d, -jnp.inf)
            inter = bt + m_prev128[:, hs]
            m_t = jnp.maximum(inter, jnp.max(d, axis=1, keepdims=True))
            wgt = jnp.exp(d - m_t)
            a = jnp.exp(inter - m_t)
            qm = jnp.where(first if hh == 0 else ~first, q_pair, 0.0)
            s = _dot_nt(qm, k_pair) * wgt
            sv = _dot(s, v_pair)
            rs = jnp.sum(s, axis=1, keepdims=True)
            per_head.append((m_t, a, sv, rs))
            yield
        sel = lambda i: jnp.where(first, per_head[0][i], per_head[1][i])
        m_t, a, sv, rs = sel(0), sel(1), sel(2), sel(3)
        qc = _dot(q_pair, c_scr[p])
        qn = _dot(q_pair * n_row[:, ps], pair_ones)
        num = a * qc + sv
        den = a * qn + rs
        houts.append(num / jnp.maximum(jnp.abs(den), jnp.exp(-m_t)))
        yield
    hout = jnp.concatenate(houts, axis=1)

    def new_m(bx, ix, m_prev):
        blast = bx[L - 1:L, :]
        wl = blast - bx + ix
        m_new = jnp.maximum(blast + m_prev, jnp.max(wl, axis=0, keepdims=True))
        return blast, wl, m_new

    blast, wl, m_new64 = new_m(b64, i64, m_prev64)
    dec = jnp.exp(blast + m_prev64 - m_new64)
    kw = k * jnp.exp(wl - m_new64)
    for p in range(ML_HEADS // 2):
        ps = slice(p * LANE, (p + 1) * LANE)
        upd = jnp.dot(_mx(kw[:, ps].T), _mx(v[:, ps]), preferred_element_type=F32)
        c_scr[p] = c_scr[p] * dec[:, ps] + upd * pair_mask
    n_scr[...] = jnp.broadcast_to(dec * n_row + jnp.sum(kw, axis=0, keepdims=True), n_scr.shape)
    m64_scr[...] = jnp.broadcast_to(m_new64, m64_scr.shape)
    _, _, m_new128 = new_m(b128, i128, m_prev128)
    m128_scr[...] = jnp.broadcast_to(m_new128, m128_scr.shape)

    head_ones = _mx(_block_ones(w, HEAD_DIM))
    ms = _dot01_right(hout * hout, head_ones) * (1.0 / HEAD_DIM)
    y = hout * lax.rsqrt(ms + RMS_EPS) * gn_ref[...]
    o_ref[...] = (y * _sigmoid(ml_ref[:, 768:1024])).astype(o_ref.dtype)


def _mlstm(ml, wgT3, cw, cb, brow, gn, batch, seq):
    L = ML_CHUNK
    nc = seq // L
    n = batch * seq
    per_tile = DSA_TILE // L
    nb = ML_SEQS
    assert batch % nb == 0

    def gate_spec(off):
        def index(b, s):
            chunk = (nb * b + off) * nc + s
            return chunk // per_tile, 0, chunk % per_tile
        return pl.BlockSpec((1, 16, L), index)

    out = pl.pallas_call(
        _mlstm_kernel,
        grid=(batch // nb, nc),
        in_specs=[pl.BlockSpec((nb, L, 1024), lambda b, s: (b, s, 0))] + [gate_spec(o) for o in range(nb)]
                 + [_const_spec((ML_CONV, 512)), _const_spec((1, 512)),
                    _const_spec((16, L)), _const_spec((1, ML_WIDTH))],
        out_specs=pl.BlockSpec((nb, L, ML_WIDTH), lambda b, s: (b, s, 0)),
        out_shape=jax.ShapeDtypeStruct((batch, seq, ML_WIDTH), MXU_DTYPE),
        scratch_shapes=[pltpu.VMEM((nb, L, 512), F32), pltpu.VMEM((nb, 2, LANE, LANE), F32),
                        pltpu.VMEM((nb, 8, ML_WIDTH), F32), pltpu.VMEM((nb, 8, ML_WIDTH), F32),
                        pltpu.VMEM((nb, 8, 512), F32)],
        compiler_params=_params("parallel", "arbitrary"),
        name="mlstm",
    )(ml.reshape(batch, seq, 1024), *([wgT3] * nb), cw, cb, brow, gn)
    return out.reshape(n, ML_WIDTH)


def _rms(y, g):
    ms = jnp.mean(y * y, axis=-1, keepdims=True)
    return y * lax.rsqrt(ms + RMS_EPS) * g


def _mem_kv_kernel(m_ref, g_ref, w_ref, k_ref, v_ref):
    h = _mx(_rms(m_ref[...], g_ref[...]))
    kv = jnp.dot(h, w_ref[...], preferred_element_type=F32)
    k_ref[...] = kv[:, 0:D_MODEL].astype(k_ref.dtype)
    v_ref[...] = kv[:, D_MODEL:2 * D_MODEL].astype(v_ref.dtype)


def _mem_kv(mem2, g, w):
    n = mem2.shape[0]
    tm = ROW_TILE
    row = pl.BlockSpec((tm, D_MODEL), lambda i: (i, 0))
    return pl.pallas_call(
        _mem_kv_kernel,
        grid=(n // tm,),
        in_specs=[row, _const_spec((1, D_MODEL)), _const_spec((D_MODEL, 2 * D_MODEL))],
        out_specs=[row, row],
        out_shape=[jax.ShapeDtypeStruct((n, D_MODEL), MXU_DTYPE)] * 2,
        compiler_params=_params("parallel"),
        name="mem_kv",
    )(mem2, g, w)


def _post_mix_kernel(a_ref, b_ref, c_ref, x_ref, k_ref, v_ref, wout_ref, wq_ref, wo_ref, wu_ref, wd_ref,
                     g_ref, o_ref):
    g = g_ref[...]
    y = (jnp.dot(a_ref[...], wout_ref[0:512, :], preferred_element_type=F32)
         + jnp.dot(b_ref[...], wout_ref[512:768, :], preferred_element_type=F32)
         + jnp.dot(c_ref[...], wout_ref[768:1024, :], preferred_element_type=F32))
    x = x_ref[...] + _rms(y, g[0:1])
    h = _mx(_rms(x, g[1:2]))
    q = jnp.dot(h, wq_ref[...], preferred_element_type=F32) * (CROSS_HEAD_DIM ** -0.5)
    outs = []
    for hd in range(CROSS_HEADS):
        hs = slice(hd * CROSS_HEAD_DIM, (hd + 1) * CROSS_HEAD_DIM)
        logits = _dot_nt(q[:, hs], k_ref[:, hs])
        p = jnp.exp(logits - jnp.max(logits, axis=-1, keepdims=True))
        o = jnp.dot(_mx(p), v_ref[:, hs], preferred_element_type=F32)
        outs.append(o / jnp.sum(p, axis=-1, keepdims=True))
    y = jnp.dot(_mx(jnp.concatenate(outs, axis=1)), wo_ref[...], preferred_element_type=F32)
    x = x + _rms(y, g[2:3])
    h = _mx(_rms(x, g[3:4]))
    y = jnp.zeros(x.shape, F32)
    for c in range(MLP_HIDDEN // D_MODEL):
        cs = slice(c * D_MODEL, (c + 1) * D_MODEL)
        u = jnp.maximum(jnp.dot(h, wu_ref[:, cs], preferred_element_type=F32), 0.0)
        y = y + jnp.dot(_mx(u * u), wd_ref[cs, :], preferred_element_type=F32)
    o_ref[...] = x + _rms(y, g[4:5])


def _post_mix(a, b, c, x2, kmem, vmem, wout, wq, wo, wu, wd, gains, seq, n_mem):
    n = x2.shape[0]
    tm = POST_TILE
    per_b = seq // tm
    row = lambda width: pl.BlockSpec((tm, width), lambda i: (i, 0))
    memb = pl.BlockSpec((n_mem, D_MODEL), lambda i: (i // per_b, 0))
    sq = _const_spec((D_MODEL, D_MODEL))
    return pl.pallas_call(
        _post_mix_kernel,
        grid=(n // tm,),
        in_specs=[row(512), row(256), row(256), row(D_MODEL), memb, memb, sq, sq, sq,
                  _const_spec((D_MODEL, MLP_HIDDEN)), _const_spec((MLP_HIDDEN, D_MODEL)),
                  _const_spec((8, D_MODEL))],
        out_specs=row(D_MODEL),
        out_shape=jax.ShapeDtypeStruct((n, D_MODEL), F32),
        compiler_params=_params("parallel"),
        name="post_mix",
    )(a, b, c, x2, kmem, vmem, wout, wq, wo, wu, wd, gains)


def _rope_tables(seq):
    half = HEAD_DIM // 2
    inv = ROPE_THETA ** (-jnp.arange(0, HEAD_DIM, 2, dtype=F32) / HEAD_DIM)
    ang = jnp.arange(seq).astype(F32)[:, None] * inv[None, :]
    cos, sin = jnp.cos(ang), jnp.sin(ang)
    lane = jnp.arange(LANE)
    freq = (lane % HEAD_DIM) % half
    first = (lane % HEAD_DIM) < half
    cosn = cos[:, freq]
    sinn = sin[:, freq]
    sina = jnp.where(first[None, :], -sinn, 0.0)
    sinb = jnp.where(first[None, :], 0.0, sinn)
    return cosn, sina, sinb, cos.T, sin.T


def _layout_w_in(w_in):
    depth = w_in.shape[0]
    z = lambda wd: jnp.zeros((depth, D_MODEL, wd), w_in.dtype)
    col = lambda a, b: w_in[:, :, a:b]
    wn = jnp.concatenate([col(_O_AK, _O_AV), col(_O_XK, _O_XW), z(64),
                          col(_O_HQ, _O_MQK), col(_O_MQK, _O_MG)], axis=-1)
    wt = jnp.concatenate([col(_O_AQ, _O_AK), col(_O_XQ, _O_XK), col(_O_AV, _O_XQ),
                          col(_O_XW, _O_HQ), col(_O_MG, _IN_WIDTH)], axis=-1)
    return _mx(wn), _mx(jnp.swapaxes(wt, 1, 2))


def kernel(x, mem, mix_pre_g, w_in, ml_conv_w, ml_conv_b, ml_gate_b, hg_lb, hg_norm_g, ml_norm_g,
           w_out, mix_post_g, cross_pre_g, mem_norm_g, w_cq, w_ckv, w_co, cross_post_g, mlp_pre_g,
           w_up, w_down, mlp_post_g):
    batch, seq, _ = x.shape
    n_mem = mem.shape[1]
    depth = w_in.shape[0]
    n = batch * seq
    assert seq % max(ROW_TILE, POST_TILE, HG_ROWS, ML_CHUNK, DSA_TILE) == 0
    assert (batch * n_mem) % ROW_TILE == 0
    assert ROW_TILE % DSA_TILE == 0 and DSA_TILE % ML_CHUNK == 0

    lbs = jnp.cumsum(jax.nn.softmax(hg_lb.astype(F32), axis=0), axis=0)
    lbs = lbs - lbs[0:1]

    tabs = _rope_tables(seq)
    wn_all, wt_all = _layout_w_in(w_in)
    row = lambda a: a.astype(F32)[None, :]
    brow = jnp.pad(ml_gate_b.astype(F32), ((0, 0), (8, 0)))

    x2 = x.reshape(n, D_MODEL)
    mem2 = mem.reshape(batch * n_mem, D_MODEL)
    nl = n // DSA_TILE
    for l in range(depth):
        knat, hg, ml, qT3, qiT3, vT3, wgT3 = _in_proj(x2, row(mix_pre_g[l]), wn_all[l], wt_all[l], tabs, seq)
        a_out = _dsa(qiT3, wgT3, knat.reshape(nl, DSA_TILE, 256), qT3, vT3, batch, seq)
        b_out = _hgrn(hg, row(lbs[l]), row(jnp.tile(hg_norm_g[l], HG_HEADS)), batch, seq)
        c_out = _mlstm(ml, wgT3, ml_conv_w[l].astype(F32), row(ml_conv_b[l]),
                       jnp.broadcast_to(brow[l][:, None], (16, ML_CHUNK)),
                       row(jnp.tile(ml_norm_g[l], ML_HEADS)), batch, seq)
        kmem, vmem = _mem_kv(mem2, row(mem_norm_g[l]), _mx(w_ckv[l]))
        gains = jnp.stack([mix_post_g[l], cross_pre_g[l], cross_post_g[l], mlp_pre_g[l], mlp_post_g[l]]
                          + [jnp.zeros_like(mix_post_g[l])] * 3).astype(F32)
        x2 = _post_mix(a_out, b_out, c_out, x2, kmem, vmem, _mx(w_out[l]), _mx(w_cq[l]), _mx(w_co[l]),
                       _mx(w_up[l]), _mx(w_down[l]), gains, seq, n_mem)
    return x2.reshape(batch, seq, D_MODEL)
```

```python
import functools

import jax
import jax.numpy as jnp
from jax import lax
from jax.experimental import pallas as pl
from jax.experimental.pallas import tpu as pltpu

F32 = jnp.float32
I32 = jnp.int32
MXU_DTYPE = jnp.bfloat16

D_MODEL = 1024
RMS_EPS = 1e-6
ROPE_THETA = 10000.0
NEG_BIG = -1e30
M_INIT = -1e30
LOG2E = 1.4426950408889634
F32_MIN_NORMAL = 1.1754943508222875e-38
BOUND_MARGIN = 1.05
SAFE_LOGIT_BOUND = 60.0
HEAD_DIM = 64
ATT_HEADS = 8
ATT_KV_HEADS = 2
ATT_GROUP = ATT_HEADS // ATT_KV_HEADS
IDX_HEADS = 8
INDEX_TOPK_MAX = 256
HG_HEADS = 4
HG_WIDTH = HG_HEADS * HEAD_DIM
ML_HEADS = 4
ML_WIDTH = ML_HEADS * HEAD_DIM
ML_CONV = 4
CROSS_HEADS = 4
CROSS_HEAD_DIM = D_MODEL // CROSS_HEADS
MLP_HIDDEN = 4 * D_MODEL

_O_AQ, _O_AK, _O_AV, _O_XQ, _O_XK, _O_XW = 0, 512, 640, 768, 1280, 1344
_O_HQ, _O_MQK, _O_MV, _O_MO, _O_MG, _IN_WIDTH = 1352, 2376, 2888, 3144, 3400, 3408

NAT_WIDTH = 128 + 128 + 1024 + 1024
TR_WIDTH = 512 + 512 + 128 + 16

LANE = 128
DSA_TILE = 256
ROW_TILE = 512
POST_TILE = 512
HG_SUB = 16
HG_ROWS = 256
HG_SEQS = 4
ML_CHUNK = 128
ML_SEQS = 4
VMEM_LIMIT = 56 * 1024 * 1024

_NT = (((1,), (1,)), ((), ()))


def _mx(a):
    return a.astype(MXU_DTYPE)


def _dot(a, b):
    return jnp.dot(_mx(a), _mx(b), preferred_element_type=F32)


def _dot_nt(a, b):
    return lax.dot_general(_mx(a), _mx(b), _NT, preferred_element_type=F32)


def _split3(x):
    hi = _mx(x)
    r1 = x - hi.astype(F32)
    mid = _mx(r1)
    lo = _mx(r1 - mid.astype(F32))
    return hi, mid, lo


def _dot01_left(m01, x):
    hi, mid, lo = _split3(x)
    f = lambda p: jnp.dot(m01, p, preferred_element_type=F32)
    return f(hi) + f(mid) + f(lo)


def _dot01_right(x, m01):
    hi, mid, lo = _split3(x)
    f = lambda p: jnp.dot(p, m01, preferred_element_type=F32)
    return f(hi) + f(mid) + f(lo)


def _sigmoid(x):
    return 1.0 / (1.0 + jnp.exp(-x))


def _silu(x):
    return x * _sigmoid(x)


def _log_sigmoid(x):
    return jnp.minimum(x, 0.0) - jnp.log(1.0 + jnp.exp(-jnp.abs(x)))


def _iota(shape, dim):
    return lax.broadcasted_iota(I32, shape, dim)


def _block_ones(n, blk):
    same = (_iota((n, n), 0) // blk) == (_iota((n, n), 1) // blk)
    return jnp.where(same, 1.0, 0.0)


def _params(*sem):
    return pltpu.CompilerParams(dimension_semantics=sem, vmem_limit_bytes=VMEM_LIMIT)


def _const_spec(shape):
    nd = len(shape)
    return pl.BlockSpec(shape, lambda *_: (0,) * nd, pipeline_mode=pl.Buffered(1))


def _in_proj_kernel(x_ref, g_ref, wn_ref, wt_ref, cosn_ref, sina_ref, sinb_ref, cost_ref, sint_ref,
                    knat_ref, hg_ref, ml_ref, qT_ref, qiT_ref, vT_ref, wgT_ref, *, idx_scale):
    tm = x_ref.shape[0]
    x = x_ref[...]
    ms = jnp.mean(x * x, axis=-1, keepdims=True)
    h = _mx(x * lax.rsqrt(ms + RMS_EPS) * g_ref[...])
    nat = jnp.dot(h, wn_ref[...], preferred_element_type=F32)
    tr = lax.dot_general(wt_ref[...], h, _NT, preferred_element_type=F32)

    cosn, sina, sinb = cosn_ref[...], sina_ref[...], sinb_ref[...]

    def rope_nat(z):
        return z * cosn + pltpu.roll(z, 96, 1) * sina + pltpu.roll(z, 32, 1) * sinb

    knat_ref[:, 0:128] = rope_nat(nat[:, 0:128]).astype(knat_ref.dtype)
    knat_ref[:, 128:256] = rope_nat(nat[:, 128:256]).astype(knat_ref.dtype)
    hg_ref[...] = nat[:, 256:1280]
    ml_ref[...] = nat[:, 1280:NAT_WIDTH]

    cost, sint = cost_ref[...], sint_ref[...]

    def rope_t(z, scale):
        outs = []
        for hd in range(8):
            x1 = z[hd * 64:hd * 64 + 32]
            x2 = z[hd * 64 + 32:hd * 64 + 64]
            outs.append((x1 * cost - x2 * sint) * scale)
            outs.append((x2 * cost + x1 * sint) * scale)
        return jnp.concatenate(outs, axis=0)

    qT = rope_t(tr[0:512], (HEAD_DIM ** -0.5) * LOG2E).astype(qT_ref.dtype)
    qiT = rope_t(tr[512:1024], 1.0).astype(qiT_ref.dtype)
    vT = tr[1024:1152].astype(vT_ref.dtype)
    wg = tr[1152:1168]
    wg = jnp.where(_iota(wg.shape, 0) < 8, wg * idx_scale, wg)
    for c in range(tm // DSA_TILE):
        sl = slice(c * DSA_TILE, (c + 1) * DSA_TILE)
        qT_ref[c] = qT[:, sl]
        qiT_ref[c] = qiT[:, sl]
        vT_ref[c] = vT[:, sl]
        wgT_ref[c] = wg[:, sl]


def _in_proj(x2, g, wn, wt, tabs, seq):
    n = x2.shape[0]
    tm = ROW_TILE
    nt_seq = seq // tm
    cosn, sina, sinb, cost, sint = tabs
    idx_scale = (IDX_HEADS ** -0.5) * (HEAD_DIM ** -0.5)
    row = lambda w: pl.BlockSpec((tm, w), lambda i: (i, 0))
    tabn = pl.BlockSpec((tm, LANE), lambda i: (i % nt_seq, 0))
    tabt = pl.BlockSpec((32, tm), lambda i: (0, i % nt_seq))
    t3 = lambda r: pl.BlockSpec((tm // DSA_TILE, r, DSA_TILE), lambda i: (i, 0, 0))
    nl = n // DSA_TILE
    return pl.pallas_call(
        functools.partial(_in_proj_kernel, idx_scale=idx_scale),
        grid=(n // tm,),
        in_specs=[row(D_MODEL), _const_spec((1, D_MODEL)), _const_spec((D_MODEL, NAT_WIDTH)),
                  _const_spec((TR_WIDTH, D_MODEL)), tabn, tabn, tabn, tabt, tabt],
        out_specs=[row(256), row(1024), row(1024), t3(512), t3(512), t3(128), t3(16)],
        out_shape=[jax.ShapeDtypeStruct((n, 256), MXU_DTYPE),
                   jax.ShapeDtypeStruct((n, 1024), F32),
                   jax.ShapeDtypeStruct((n, 1024), F32),
                   jax.ShapeDtypeStruct((nl, 512, DSA_TILE), MXU_DTYPE),
                   jax.ShapeDtypeStruct((nl, 512, DSA_TILE), MXU_DTYPE),
                   jax.ShapeDtypeStruct((nl, 128, DSA_TILE), MXU_DTYPE),
                   jax.ShapeDtypeStruct((nl, 16, DSA_TILE), F32)],
        compiler_params=_params("parallel"),
        name="in_proj",
    )(x2, g, wn, wt, cosn, sina, sinb, cost, sint)


def _dsa_kernel(qiT_ref, wgT_ref, knat_ref, qT_ref, vT_ref, o_ref,
                key_scr, hi_scr, bias_scr, acc_scr, thr_scr, nge_scr, kmax_scr, *, topk):
    tq = DSA_TILE
    j = pl.program_id(1)
    nk = j + 1
    int_min = jnp.int32(-2 ** 31)
    srow = _iota((tq, tq), 0)
    tcol = _iota((tq, tq), 1)
    causal = srow <= tcol

    w = wgT_ref[0]
    qi = qiT_ref[0]
    qi_wide = jnp.concatenate([qi[hd * 64:(hd + 1) * 64] for hd in range(IDX_HEADS)], axis=1)

    def score_tile(kt, carry):
        kblk = knat_ref[kt][:, 128:192]
        acc = jnp.zeros((tq, tq), F32)
        ys = jnp.dot(kblk, qi_wide, preferred_element_type=F32)
        for hd in range(IDX_HEADS):
            acc = acc + w[hd:hd + 1, :] * jnp.maximum(ys[:, hd * tq:(hd + 1) * tq], 0.0)
        acc = jnp.where(jnp.abs(acc) < F32_MIN_NORMAL, 0.0, acc)
        acc = jnp.where(jnp.logical_or(kt < j, causal), acc, -jnp.inf)
        bits = pltpu.bitcast(acc, I32)
        key_scr[kt] = bits ^ ((bits >> 31) & jnp.int32(0x7FFFFFFF))
        hi_scr[kt] = pltpu.bitcast(bits & jnp.int32(-65536), F32).astype(jnp.bfloat16)
        return carry

    lax.fori_loop(0, nk, score_tile, 0)

    key16_neg_inf = (0xFF80 ^ 0x7FFF) - 65536
    one_h = jnp.ones((16, tq), jnp.bfloat16)
    zero_h = jnp.zeros((16, tq), jnp.bfloat16)

    def select_threshold(nks):
        def count16(cand16):
            b16 = cand16 ^ ((cand16 >> 15) & 0x7FFF)
            b16 = jnp.where((cand16 > 0) & (cand16 < 0x80), 0x80, b16)
            cb = jnp.broadcast_to(pltpu.bitcast(b16 << 16, F32).astype(jnp.bfloat16), (16, tq))
            accs = [zero_h] * 4
            n = 0
            for kt in range(nks):
                for r in range(tq // 16):
                    ind = jnp.where(hi_scr[kt, r * 16:(r + 1) * 16, :] >= cb, one_h, zero_h)
                    accs[n % 4] = accs[n % 4] + ind
                    n += 1
            tot = (accs[0] + accs[1]).astype(F32) + (accs[2] + accs[3]).astype(F32)
            return jnp.sum(tot, axis=0, keepdims=True)

        def count32(pred_fn):
            accs = [jnp.zeros((8, tq), I32)] * 4
            n = 0
            for kt in range(nks):
                for r in range(tq // 8):
                    ind = jnp.where(pred_fn(key_scr[kt, r * 8:(r + 1) * 8, :]), 1, 0).astype(I32)
                    accs[n % 4] = accs[n % 4] + ind
                    n += 1
            return jnp.sum((accs[0] + accs[1]) + (accs[2] + accs[3]), axis=0, keepdims=True)

        def stage16(i, lo):
            cand = lo + (jnp.int32(1) << (15 - i))
            ok = (count16(cand) >= topk) | (cand <= key16_neg_inf)
            return jnp.where(ok, cand, lo)

        lo16 = lax.fori_loop(0, 16, stage16, jnp.full((1, tq), -32768, I32))

        def stage32(i, carry):
            lo, _ = carry
            cand = lo + (jnp.int32(1) << (15 - i))
            cnt = count32(lambda k: k >= cand)
            ok = cnt >= topk
            return jnp.where(ok, cand, lo), jnp.where(ok, cnt, carry[1])

        thr, n_ge = lax.fori_loop(0, 16, stage32, (lo16 << 16, jnp.full((1, tq), nks * tq, I32)))
        thr_scr[...] = jnp.broadcast_to(thr, thr_scr.shape)
        nge_scr[...] = jnp.broadcast_to(n_ge, nge_scr.shape)

    for jj in range(key_scr.shape[0]):
        @pl.when(j == jj)
        def _(jj=jj):
            if (jj + 1) * tq <= topk:
                thr_scr[...] = jnp.full(thr_scr.shape, int_min, I32)
                nge_scr[...] = jnp.full(nge_scr.shape, topk, I32)
            else:
                select_threshold(jj + 1)

    thr = thr_scr[0:1, :]
    n_ge = nge_scr[0:1, :]

    q = qT_ref[0]

    @pl.when(j == 0)
    def _():
        group_ones = _mx(_block_ones(LANE, HEAD_DIM))

        def body(kt, m):
            k = knat_ref[kt][:, 0:128].astype(F32)
            n2 = jnp.dot(_mx(k * k), group_ones, preferred_element_type=F32)
            return jnp.maximum(m, jnp.max(n2, axis=0, keepdims=True))

        kmax2 = lax.fori_loop(0, key_scr.shape[0], body, jnp.zeros((1, LANE), F32))
        kmax_scr[...] = jnp.broadcast_to(kmax2, kmax_scr.shape)

    qf = q.astype(F32)
    bound2 = jnp.zeros((1, tq), F32)
    for hd in range(ATT_HEADS):
        g = hd // ATT_GROUP
        qn2 = jnp.sum(qf[hd * 64:(hd + 1) * 64] * qf[hd * 64:(hd + 1) * 64], axis=0, keepdims=True)
        bound2 = jnp.maximum(bound2, qn2 * kmax_scr[0:1, g * HEAD_DIM:g * HEAD_DIM + 1])
    bound = BOUND_MARGIN * jnp.sqrt(bound2)
    bound_is_small = jnp.max(bound) < SAFE_LOGIT_BOUND
    neg_bound = -bound

    def mask_plain():
        def body(kt, carry):
            bias_scr[kt] = jnp.where(key_scr[kt] >= thr, neg_bound, NEG_BIG)
            return carry
        lax.fori_loop(0, nk, body, 0)

    def mask_ties():
        def count_gt(kt, c):
            ind = jnp.where(key_scr[kt] > thr, 1.0, 0.0)
            return c + jnp.sum(ind, axis=0, keepdims=True)
        n_gt = lax.fori_loop(0, nk, count_gt, jnp.zeros((1, tq), F32))
        need = topk - n_gt
        tri = _mx(jnp.where(srow >= tcol, 1.0, 0.0))

        def body(kt, off):
            k = key_scr[kt]
            eq = k == thr
            eqf = jnp.where(eq, 1.0, 0.0)
            rank = jnp.dot(tri, _mx(eqf), preferred_element_type=F32) + off
            tie_bias = jnp.where(rank <= need, neg_bound, NEG_BIG)
            bias_scr[kt] = jnp.where(k > thr, neg_bound, jnp.where(eq, tie_bias, NEG_BIG))
            return off + jnp.sum(eqf, axis=0, keepdims=True)
        lax.fori_loop(0, nk, body, jnp.zeros((1, tq), F32))

    no_excess_ties = jnp.max(jnp.abs(n_ge - topk)) == 0
    lax.cond(no_excess_ties, mask_plain, mask_ties)
    bias_scr[j] = jnp.where(causal, bias_scr[j], NEG_BIG)

    acc_scr[...] = jnp.zeros(acc_scr.shape, F32)
    zeros64 = jnp.zeros((64, tq), q.dtype)
    qwide = []
    for g in range(ATT_KV_HEADS):
        cols = []
        for hh in range(ATT_GROUP):
            hd = g * ATT_GROUP + hh
            qh = q[hd * 64:(hd + 1) * 64]
            cols.append(jnp.concatenate([qh, zeros64] if g == 0 else [zeros64, qh], axis=0))
        qwide.append(jnp.concatenate(cols, axis=1))

    ones_rows = jnp.ones((16, tq), q.dtype)

    def attend(kt, carry):
        ms, ls = carry
        kb = knat_ref[kt][:, 0:128]
        vtb = vT_ref[kt]
        bias = bias_scr[kt]
        lts = [jnp.dot(kb, qwide[g], preferred_element_type=F32) for g in range(ATT_KV_HEADS)]
        new_m, new_l = [], []
        for g in range(ATT_KV_HEADS):
            lt = jnp.concatenate([lts[g][:, hh * tq:(hh + 1) * tq] + bias for hh in range(ATT_GROUP)], axis=1)
            m_new = jnp.maximum(ms[g], jnp.max(lt, axis=0, keepdims=True))
            alpha = jnp.exp2(ms[g] - m_new)
            p = _mx(jnp.exp2(lt - m_new))
            v_aug = jnp.concatenate([vtb[g * 64:(g + 1) * 64], ones_rows], axis=0)
            pv = jnp.dot(v_aug, p, preferred_element_type=F32)
            new_l.append(alpha * ls[g] + pv[HEAD_DIM:HEAD_DIM + 1, :])
            acc_scr[g, 0:HEAD_DIM, :] = alpha * acc_scr[g, 0:HEAD_DIM, :] + pv[0:HEAD_DIM, :]
            new_m.append(m_new)
        return tuple(new_m), tuple(new_l)

    wide = ATT_GROUP * tq

    def run_online():
        init = (tuple(jnp.full((1, wide), NEG_BIG, F32) for _ in range(ATT_KV_HEADS)),
                tuple(jnp.zeros((1, wide), F32) for _ in range(ATT_KV_HEADS)))
        _, ls = lax.fori_loop(0, nk, attend, init)
        for g in range(ATT_KV_HEADS):
            acc_scr[g, HEAD_DIM:, :] = jnp.broadcast_to(ls[g], (16, wide))

    def attend_bounded(kt, carry):
        kb = knat_ref[kt][:, 0:128]
        vtb = vT_ref[kt]
        bias = bias_scr[kt]
        for g in range(ATT_KV_HEADS):
            lt = jnp.dot(kb, qwide[g], preferred_element_type=F32)
            p = _mx(jnp.exp2(jnp.concatenate([lt[:, hh * tq:(hh + 1) * tq] + bias
                                              for hh in range(ATT_GROUP)], axis=1)))
            v_aug = jnp.concatenate([vtb[g * 64:(g + 1) * 64], ones_rows], axis=0)
            acc_scr[g] = acc_scr[g] + jnp.dot(v_aug, p, preferred_element_type=F32)
        return carry

    def run_bounded():
        def pair(i, carry):
            attend_bounded(2 * i, carry)
            return attend_bounded(2 * i + 1, carry)

        lax.fori_loop(0, nk // 2, pair, 0)

        @pl.when(nk % 2 == 1)
        def _():
            attend_bounded(nk - 1, 0)

    lax.cond(bound_is_small, run_bounded, run_online)
    outs = []
    for g in range(ATT_KV_HEADS):
        og = acc_scr[g, 0:HEAD_DIM, :] / acc_scr[g, HEAD_DIM:HEAD_DIM + 1, :]
        outs += [og[:, hh * tq:(hh + 1) * tq] for hh in range(ATT_GROUP)]
    o_ref[...] = jnp.concatenate(outs, axis=0).T.astype(o_ref.dtype)


def _dsa(qiT3, wgT3, knat3, qT3, vT3, batch, seq):
    tq = DSA_TILE
    nq = seq // tq
    n = batch * seq
    topk = min(INDEX_TOPK_MAX, seq // 4)
    qtile = lambda r: pl.BlockSpec((1, r, tq), lambda b, j: (b * nq + j, 0, 0))
    return pl.pallas_call(
        functools.partial(_dsa_kernel, topk=topk),
        grid=(batch, nq),
        in_specs=[qtile(512), qtile(16),
                  pl.BlockSpec((nq, tq, 256), lambda b, j: (b, 0, 0)),
                  qtile(512),
                  pl.BlockSpec((nq, 128, tq), lambda b, j: (b, 0, 0))],
        out_specs=pl.BlockSpec((tq, 512), lambda b, j: (b * nq + j, 0)),
        out_shape=jax.ShapeDtypeStruct((n, 512), MXU_DTYPE),
        scratch_shapes=[pltpu.VMEM((nq, tq, tq), I32), pltpu.VMEM((nq, tq, tq), jnp.bfloat16),
                        pltpu.VMEM((nq, tq, tq), F32),
                        pltpu.VMEM((ATT_KV_HEADS, HEAD_DIM + 16, ATT_GROUP * tq), F32),
                        pltpu.VMEM((8, tq), I32), pltpu.VMEM((8, tq), I32), pltpu.VMEM((8, LANE), F32)],
        compiler_params=_params("parallel", "arbitrary"),
        name="dsa",
    )(qiT3, wgT3, knat3, qT3, vT3)


def _hgrn_kernel(hg_ref, lb_ref, gn_ref, o_ref, *scratch):
    st_scr, z_scr = scratch[0], scratch[7]

    @pl.when(pl.program_id(1) == 0)
    def _():
        st_scr[...] = jnp.zeros(st_scr.shape, F32)
        z_scr[...] = jnp.zeros(z_scr.shape, z_scr.dtype)

    seqs = [_hgrn_sequence(hg_ref.at[bi], lb_ref, gn_ref, o_ref.at[bi], *(s.at[bi] for s in scratch))
            for bi in range(HG_SEQS)]
    while seqs:
        for seq in list(seqs):
            if next(seq, "done") == "done":
                seqs.remove(seq)


def _hgrn_sequence(hg_ref, lb_ref, gn_ref, o_ref, st_scr, q_scr, k_scr, g_scr, qg_scr, kd_scr, o_scr, z_scr,
                   snap_scr):
    rows = hg_ref.shape[0]
    c = HG_SUB
    w = HG_WIDTH

    lb = lb_ref[...]
    q = _silu(hg_ref[:, 0:w]) * (HEAD_DIM ** -0.5)
    forget = lb + (1.0 - lb) * _sigmoid(hg_ref[:, w:2 * w])
    logf = jnp.log2(forget)
    kk = 1.0 - forget
    v = hg_ref[:, 2 * w:3 * w]

    ri = _iota((rows, rows), 0)
    ci = _iota((rows, rows), 1)
    same = (ri // c) == (ci // c)
    tri = _mx(jnp.where(same & (ci <= ri), 1.0, 0.0))
    ones_blk = _mx(jnp.where(same, 1.0, 0.0))
    gl = _dot01_left(tri, logf)
    glast = _dot01_left(ones_blk, logf)
    q_scr[...] = q
    k_scr[...] = gl - jnp.log2(kk)
    g_scr[...] = gl
    qg_scr[...] = q * jnp.exp2(gl)
    kd_scr[...] = kk * jnp.exp2(glast - gl)
    dec = jnp.exp2(glast)
    vT = _mx(v.T)
    head_ones = _mx(_block_ones(w, HEAD_DIM))
    pair_mask = _block_ones(LANE, HEAD_DIM)
    half = c // 2
    rhalf = _iota((half, w), 0)
    nsc = rows // c
    yield

    for sc in range(nsc):
        z_scr[sc, sc * c:(sc + 1) * c, :] = kd_scr[sc * c:(sc + 1) * c, :].astype(z_scr.dtype)
    for p in range(HG_HEADS // 2):
        ps = slice(p * LANE, (p + 1) * LANE)
        st = st_scr[p]
        for sc in range(nsc):
            snap_scr[sc, p] = st.astype(snap_scr.dtype)
            upd = jnp.dot(vT[ps, :], z_scr[sc, :, ps], preferred_element_type=F32)
            st = st * dec[sc * c:sc * c + 1, ps] + upd * pair_mask
        st_scr[p] = st
        yield

    for sc in range(nsc):
        r0 = sc * c
        q_top, q_bot = q_scr[r0:r0 + half, :], q_scr[r0 + half:r0 + c, :]
        g_top, g_bot = g_scr[r0:r0 + half, :], g_scr[r0 + half:r0 + c, :]
        bcast = lambda row: jnp.broadcast_to(row, (half, w))
        xs_top, xs_bot = [], []
        for s in range(c):
            hsr = bcast(k_scr[r0 + s:r0 + s + 1, :])
            if s < half:
                xs_top.append(q_top * jnp.exp2(jnp.where(rhalf >= s, g_top - hsr, -jnp.inf)))
                xs_bot.append(q_bot * jnp.exp2(g_bot - hsr))
            else:
                xs_bot.append(q_bot * jnp.exp2(jnp.where(rhalf >= s - half, g_bot - hsr, -jnp.inf)))
        y = jnp.dot(_mx(jnp.concatenate(xs_top + xs_bot, axis=0)), head_ones, preferred_element_type=F32)
        o_top = jnp.zeros((half, w), F32)
        o_bot = jnp.zeros((half, w), F32)
        for s in range(c):
            vsr = bcast(hg_ref[r0 + s:r0 + s + 1, 2 * w:3 * w])
            if s < half:
                o_top = o_top + y[s * half:(s + 1) * half, :] * vsr
            o_bot = o_bot + y[(half + s) * half:(half + s + 1) * half, :] * vsr
        o_diag = jnp.concatenate([o_top, o_bot], axis=0)
        o_pairs = [_dot_nt(qg_scr[r0:r0 + c, p * LANE:(p + 1) * LANE], snap_scr[sc, p])
                   for p in range(HG_HEADS // 2)]
        o_scr[r0:r0 + c, :] = jnp.concatenate(o_pairs, axis=1) + o_diag
        yield

    o = o_scr[...]
    ms = _dot01_right(o * o, head_ones) * (1.0 / HEAD_DIM)
    y = o * lax.rsqrt(ms + RMS_EPS) * gn_ref[...]
    o_ref[...] = (y * _silu(hg_ref[:, 3 * w:4 * w])).astype(o_ref.dtype)


def _hgrn(hg, lb, gn, batch, seq):
    rows = HG_ROWS
    ns = seq // rows
    n = batch * seq
    nb = HG_SEQS
    assert batch % nb == 0
    vm = lambda: pltpu.VMEM((nb, rows, HG_WIDTH), F32)
    out = pl.pallas_call(
        _hgrn_kernel,
        grid=(batch // nb, ns),
        in_specs=[pl.BlockSpec((nb, rows, 1024), lambda b, s: (b, s, 0)),
                  _const_spec((1, HG_WIDTH)), _const_spec((1, HG_WIDTH))],
        out_specs=pl.BlockSpec((nb, rows, HG_WIDTH), lambda b, s: (b, s, 0)),
        out_shape=jax.ShapeDtypeStruct((batch, seq, HG_WIDTH), MXU_DTYPE),
        scratch_shapes=[pltpu.VMEM((nb, HG_HEADS // 2, LANE, LANE), F32), vm(), vm(), vm(), vm(), vm(), vm(),
                        pltpu.VMEM((nb, rows // HG_SUB, rows, HG_WIDTH), MXU_DTYPE),
                        pltpu.VMEM((nb, rows // HG_SUB, HG_HEADS // 2, LANE, LANE), MXU_DTYPE)],
        compiler_params=_params("parallel", "arbitrary"),
        name="hgrn",
    )(hg.reshape(batch, seq, 1024), lb, gn)
    return out.reshape(n, HG_WIDTH)


def _mlstm_kernel(ml_ref, *refs):
    gT_refs = refs[:ML_SEQS]
    cw_ref, cb_ref, brow_ref, gn_ref, o_ref, prev_scr, c_scr, n_scr, m64_scr, m128_scr = refs[ML_SEQS:]

    @pl.when(pl.program_id(1) == 0)
    def _():
        prev_scr[...] = jnp.zeros(prev_scr.shape, F32)
        c_scr[...] = jnp.zeros(c_scr.shape, F32)
        n_scr[...] = jnp.zeros(n_scr.shape, F32)
        m64_scr[...] = jnp.full(m64_scr.shape, M_INIT, F32)
        m128_scr[...] = jnp.full(m128_scr.shape, M_INIT, F32)

    seqs = [_mlstm_sequence(ml_ref.at[bi], gT_ref, cw_ref, cb_ref, brow_ref, gn_ref, o_ref.at[bi],
                            prev_scr.at[bi], c_scr.at[bi], n_scr.at[bi], m64_scr.at[bi], m128_scr.at[bi])
            for bi, gT_ref in enumerate(gT_refs)]
    while seqs:
        for seq in list(seqs):
            if next(seq, "done") == "done":
                seqs.remove(seq)


def _mlstm_sequence(ml_ref, gT_ref, cw_ref, cb_ref, brow_ref, gn_ref, o_ref,
                    prev_scr, c_scr, n_scr, m64_scr, m128_scr):
    L = ML_CHUNK
    w = ML_WIDTH

    xqk = ml_ref[:, 0:512]
    prev = prev_scr[...]
    rows = _iota((L, 512), 0)
    acc = cb_ref[...] + cw_ref[ML_CONV - 1:ML_CONV, :] * xqk
    for sh in range(1, ML_CONV):
        shifted = jnp.where(rows >= sh, pltpu.roll(xqk, sh, 0), pltpu.roll(prev, sh, 0))
        acc = acc + cw_ref[ML_CONV - 1 - sh:ML_CONV - sh, :] * shifted
    prev_scr[...] = xqk
    qk = _silu(acc)
    q = qk[:, 0:w]
    k = qk[:, w:2 * w] * (HEAD_DIM ** -0.5)
    v = ml_ref[:, 512:768]

    srow = _iota((L, L), 0)
    scol = _iota((L, L), 1)
    tril = srow >= scol
    triu = _mx(jnp.where(srow <= scol, 1.0, 0.0))
    grow = gT_ref[0] + brow_ref[...]
    brow = _dot01_right(_log_sigmoid(grow), triu)
    yield

    def expand(x, width, row0):
        src = _iota((16, ML_HEADS * width), 0)
        dst = _iota((16, ML_HEADS * width), 1)
        e01 = _mx(jnp.where(src == row0 + dst // width, 1.0, 0.0))
        f = lambda p: lax.dot_general(p, e01, (((0,), (0,)), ((), ())), preferred_element_type=F32)
        hi, mid, lo = _split3(x)
        return f(hi) + f(mid) + f(lo)

    b128 = expand(brow, 128, 12)
    i128 = expand(grow, 128, 8)
    b64 = expand(brow, 64, 12)
    i64 = expand(grow, 64, 8)
    yield

    lane = _iota((L, LANE), 1)
    first = lane < HEAD_DIM
    pair_ones = _mx(_block_ones(LANE, HEAD_DIM))
    pair_mask = _block_ones(LANE, HEAD_DIM)
    m_prev128 = m128_scr[0:1, :]
    m_prev64 = m64_scr[0:1, :]
    n_row = n_scr[0:1, :]

    houts = []
    for p in range(ML_HEADS // 2):
        ps = slice(p * LANE, (p + 1) * LANE)
        q_pair, k_pair, v_pair = q[:, ps], k[:, ps], v[:, ps]
        per_head = []
        for hh in range(2):
            hd = 2 * p + hh
            hs = slice(hd * 128, (hd + 1) * 128)
            bt = b128[:, hs]
            d = bt - brow[12 + hd:13 + hd, :] + grow[8 + hd:9 + hd, :]
            d = jnp.where(tril, d, -jnp.inf)
            inter = bt + m_prev128[:, hs]
            m_t = jnp.maximum(inter, jnp.max(d, axis=1, keepdims=True))
            wgt = jnp.exp(d - m_t)
            a = jnp.exp(inter - m_t)
            qm = jnp.where(first if hh == 0 else ~first, q_pair, 0.0)
            s = _dot_nt(qm, k_pair) * wgt
            sv = _dot(s, v_pair)
            rs = jnp.sum(s, axis=1, keepdims=True)
            per_head.append((m_t, a, sv, rs))
            yield
        sel = lambda i: jnp.where(first, per_head[0][i], per_head[1][i])
        m_t, a, sv, rs = sel(0), sel(1), sel(2), sel(3)
        qc = _dot(q_pair, c_scr[p])
        qn = _dot(q_pair * n_row[:, ps], pair_ones)
        num = a * qc + sv
        den = a * qn + rs
        houts.append(num / jnp.maximum(jnp.abs(den), jnp.exp(-m_t)))
        yield
    hout = jnp.concatenate(houts, axis=1)

    def new_m(bx, ix, m_prev):
        blast = bx[L - 1:L, :]
        wl = blast - bx + ix
        m_new = jnp.maximum(blast + m_prev, jnp.max(wl, axis=0, keepdims=True))
        return blast, wl, m_new

    blast, wl, m_new64 = new_m(b64, i64, m_prev64)
    dec = jnp.exp(blast + m_prev64 - m_new64)
    kw = k * jnp.exp(wl - m_new64)
    for p in range(ML_HEADS // 2):
        ps = slice(p * LANE, (p + 1) * LANE)
        upd = jnp.dot(_mx(kw[:, ps].T), _mx(v[:, ps]), preferred_element_type=F32)
        c_scr[p] = c_scr[p] * dec[:, ps] + upd * pair_mask
    n_scr[...] = jnp.broadcast_to(dec * n_row + jnp.sum(kw, axis=0, keepdims=True), n_scr.shape)
    m64_scr[...] = jnp.broadcast_to(m_new64, m64_scr.shape)
    _, _, m_new128 = new_m(b128, i128, m_prev128)
    m128_scr[...] = jnp.broadcast_to(m_new128, m128_scr.shape)

    head_ones = _mx(_block_ones(w, HEAD_DIM))
    ms = _dot01_right(hout * hout, head_ones) * (1.0 / HEAD_DIM)
    y = hout * lax.rsqrt(ms + RMS_EPS) * gn_ref[...]
    o_ref[...] = (y * _sigmoid(ml_ref[:, 768:1024])).astype(o_ref.dtype)


def _mlstm(ml, wgT3, cw, cb, brow, gn, batch, seq):
    L = ML_CHUNK
    nc = seq // L
    n = batch * seq
    per_tile = DSA_TILE // L
    nb = ML_SEQS
    assert batch % nb == 0

    def gate_spec(off):
        def index(b, s):
            chunk = (nb * b + off) * nc + s
            return chunk // per_tile, 0, chunk % per_tile
        return pl.BlockSpec((1, 16, L), index)

    out = pl.pallas_call(
        _mlstm_kernel,
        grid=(batch // nb, nc),
        in_specs=[pl.BlockSpec((nb, L, 1024), lambda b, s: (b, s, 0))] + [gate_spec(o) for o in range(nb)]
                 + [_const_spec((ML_CONV, 512)), _const_spec((1, 512)),
                    _const_spec((16, L)), _const_spec((1, ML_WIDTH))],
        out_specs=pl.BlockSpec((nb, L, ML_WIDTH), lambda b, s: (b, s, 0)),
        out_shape=jax.ShapeDtypeStruct((batch, seq, ML_WIDTH), MXU_DTYPE),
        scratch_shapes=[pltpu.VMEM((nb, L, 512), F32), pltpu.VMEM((nb, 2, LANE, LANE), F32),
                        pltpu.VMEM((nb, 8, ML_WIDTH), F32), pltpu.VMEM((nb, 8, ML_WIDTH), F32),
                        pltpu.VMEM((nb, 8, 512), F32)],
        compiler_params=_params("parallel", "arbitrary"),
        name="mlstm",
    )(ml.reshape(batch, seq, 1024), *([wgT3] * nb), cw, cb, brow, gn)
    return out.reshape(n, ML_WIDTH)


def _rms(y, g):
    ms = jnp.mean(y * y, axis=-1, keepdims=True)
    return y * lax.rsqrt(ms + RMS_EPS) * g


def _mem_kv_kernel(m_ref, g_ref, w_ref, k_ref, v_ref):
    h = _mx(_rms(m_ref[...], g_ref[...]))
    kv = jnp.dot(h, w_ref[...], preferred_element_type=F32)
    k_ref[...] = kv[:, 0:D_MODEL].astype(k_ref.dtype)
    v_ref[...] = kv[:, D_MODEL:2 * D_MODEL].astype(v_ref.dtype)


def _mem_kv(mem2, g, w):
    n = mem2.shape[0]
    tm = ROW_TILE
    row = pl.BlockSpec((tm, D_MODEL), lambda i: (i, 0))
    return pl.pallas_call(
        _mem_kv_kernel,
        grid=(n // tm,),
        in_specs=[row, _const_spec((1, D_MODEL)), _const_spec((D_MODEL, 2 * D_MODEL))],
        out_specs=[row, row],
        out_shape=[jax.ShapeDtypeStruct((n, D_MODEL), MXU_DTYPE)] * 2,
        compiler_params=_params("parallel"),
        name="mem_kv",
    )(mem2, g, w)


def _post_mix_kernel(a_ref, b_ref, c_ref, x_ref, k_ref, v_ref, wout_ref, wq_ref, wo_ref, wu_ref, wd_ref,
                     g_ref, o_ref):
    g = g_ref[...]
    y = (jnp.dot(a_ref[...], wout_ref[0:512, :], preferred_element_type=F32)
         + jnp.dot(b_ref[...], wout_ref[512:768, :], preferred_element_type=F32)
         + jnp.dot(c_ref[...], wout_ref[768:1024, :], preferred_element_type=F32))
    x = x_ref[...] + _rms(y, g[0:1])
    h = _mx(_rms(x, g[1:2]))
    q = jnp.dot(h, wq_ref[...], preferred_element_type=F32) * (CROSS_HEAD_DIM ** -0.5)
    outs = []
    for hd in range(CROSS_HEADS):
        hs = slice(hd * CROSS_HEAD_DIM, (hd + 1) * CROSS_HEAD_DIM)
        logits = _dot_nt(q[:, hs], k_ref[:, hs])
        p = jnp.exp(logits - jnp.max(logits, axis=-1, keepdims=True))
        o = jnp.dot(_mx(p), v_ref[:, hs], preferred_element_type=F32)
        outs.append(o / jnp.sum(p, axis=-1, keepdims=True))
    y = jnp.dot(_mx(jnp.concatenate(outs, axis=1)), wo_ref[...], preferred_element_type=F32)
    x = x + _rms(y, g[2:3])
    h = _mx(_rms(x, g[3:4]))
    y = jnp.zeros(x.shape, F32)
    for c in range(MLP_HIDDEN // D_MODEL):
        cs = slice(c * D_MODEL, (c + 1) * D_MODEL)
        u = jnp.maximum(jnp.dot(h, wu_ref[:, cs], preferred_element_type=F32), 0.0)
        y = y + jnp.dot(_mx(u * u), wd_ref[cs, :], preferred_element_type=F32)
    o_ref[...] = x + _rms(y, g[4:5])


def _post_mix(a, b, c, x2, kmem, vmem, wout, wq, wo, wu, wd, gains, seq, n_mem):
    n = x2.shape[0]
    tm = POST_TILE
    per_b = seq // tm
    row = lambda width: pl.BlockSpec((tm, width), lambda i: (i, 0))
    memb = pl.BlockSpec((n_mem, D_MODEL), lambda i: (i // per_b, 0))
    sq = _const_spec((D_MODEL, D_MODEL))
    return pl.pallas_call(
        _post_mix_kernel,
        grid=(n // tm,),
        in_specs=[row(512), row(256), row(256), row(D_MODEL), memb, memb, sq, sq, sq,
                  _const_spec((D_MODEL, MLP_HIDDEN)), _const_spec((MLP_HIDDEN, D_MODEL)),
                  _const_spec((8, D_MODEL))],
        out_specs=row(D_MODEL),
        out_shape=jax.ShapeDtypeStruct((n, D_MODEL), F32),
        compiler_params=_params("parallel"),
        name="post_mix",
    )(a, b, c, x2, kmem, vmem, wout, wq, wo, wu, wd, gains)


def _rope_tables(seq):
    half = HEAD_DIM // 2
    inv = ROPE_THETA ** (-jnp.arange(0, HEAD_DIM, 2, dtype=F32) / HEAD_DIM)
    ang = jnp.arange(seq).astype(F32)[:, None] * inv[None, :]
    cos, sin = jnp.cos(ang), jnp.sin(ang)
    lane = jnp.arange(LANE)
    freq = (lane % HEAD_DIM) % half
    first = (lane % HEAD_DIM) < half
    cosn = cos[:, freq]
    sinn = sin[:, freq]
    sina = jnp.where(first[None, :], -sinn, 0.0)
    sinb = jnp.where(first[None, :], 0.0, sinn)
    return cosn, sina, sinb, cos.T, sin.T


def _layout_w_in(w_in):
    depth = w_in.shape[0]
    z = lambda wd: jnp.zeros((depth, D_MODEL, wd), w_in.dtype)
    col = lambda a, b: w_in[:, :, a:b]
    wn = jnp.concatenate([col(_O_AK, _O_AV), col(_O_XK, _O_XW), z(64),
                          col(_O_HQ, _O_MQK), col(_O_MQK, _O_MG)], axis=-1)
    wt = jnp.concatenate([col(_O_AQ, _O_AK), col(_O_XQ, _O_XK), col(_O_AV, _O_XQ),
                          col(_O_XW, _O_HQ), col(_O_MG, _IN_WIDTH)], axis=-1)
    return _mx(wn), _mx(jnp.swapaxes(wt, 1, 2))


def kernel(x, mem, mix_pre_g, w_in, ml_conv_w, ml_conv_b, ml_gate_b, hg_lb, hg_norm_g, ml_norm_g,
           w_out, mix_post_g, cross_pre_g, mem_norm_g, w_cq, w_ckv, w_co, cross_post_g, mlp_pre_g,
           w_up, w_down, mlp_post_g):
    batch, seq, _ = x.shape
    n_mem = mem.shape[1]
    depth = w_in.shape[0]
    n = batch * seq
    assert seq % max(ROW_TILE, POST_TILE, HG_ROWS, ML_CHUNK, DSA_TILE) == 0
    assert (batch * n_mem) % ROW_TILE == 0
    assert ROW_TILE % DSA_TILE == 0 and DSA_TILE % ML_CHUNK == 0

    lbs = jnp.cumsum(jax.nn.softmax(hg_lb.astype(F32), axis=0), axis=0)
    lbs = lbs - lbs[0:1]

    tabs = _rope_tables(seq)
    wn_all, wt_all = _layout_w_in(w_in)
    row = lambda a: a.astype(F32)[None, :]
    brow = jnp.pad(ml_gate_b.astype(F32), ((0, 0), (8, 0)))

    x2 = x.reshape(n, D_MODEL)
    mem2 = mem.reshape(batch * n_mem, D_MODEL)
    nl = n // DSA_TILE
    for l in range(depth):
        knat, hg, ml, qT3, qiT3, vT3, wgT3 = _in_proj(x2, row(mix_pre_g[l]), wn_all[l], wt_all[l], tabs, seq)
        a_out = _dsa(qiT3, wgT3, knat.reshape(nl, DSA_TILE, 256), qT3, vT3, batch, seq)
        b_out = _hgrn(hg, row(lbs[l]), row(jnp.tile(hg_norm_g[l], HG_HEADS)), batch, seq)
        c_out = _mlstm(ml, wgT3, ml_conv_w[l].astype(F32), row(ml_conv_b[l]),
                       jnp.broadcast_to(brow[l][:, None], (16, ML_CHUNK)),
                       row(jnp.tile(ml_norm_g[l], ML_HEADS)), batch, seq)
        kmem, vmem = _mem_kv(mem2, row(mem_norm_g[l]), _mx(w_ckv[l]))
        gains = jnp.stack([mix_post_g[l], cross_pre_g[l], cross_post_g[l], mlp_pre_g[l], mlp_post_g[l]]
                          + [jnp.zeros_like(mix_post_g[l])] * 3).astype(F32)
        x2 = _post_mix(a_out, b_out, c_out, x2, kmem, vmem, _mx(w_out[l]), _mx(w_cq[l]), _mx(w_co[l]),
                       _mx(w_up[l]), _mx(w_down[l]), gains, seq, n_mem)
    return x2.reshape(batch, seq, D_MODEL)
```

```python
import functools

import jax
import jax.numpy as jnp
from jax import lax
from jax.experimental import pallas as pl
from jax.experimental.pallas import tpu as pltpu

F32 = jnp.float32
I32 = jnp.int32
MXU_DTYPE = jnp.bfloat16

D_MODEL = 1024
RMS_EPS = 1e-6
ROPE_THETA = 10000.0
NEG_BIG = -1e30
M_INIT = -1e30
LOG2E = 1.4426950408889634
F32_MIN_NORMAL = 1.1754943508222875e-38
BOUND_MARGIN = 1.05
SAFE_LOGIT_BOUND = 60.0
HEAD_DIM = 64
ATT_HEADS = 8
ATT_KV_HEADS = 2
ATT_GROUP = ATT_HEADS // ATT_KV_HEADS
IDX_HEADS = 8
INDEX_TOPK_MAX = 256
HG_HEADS = 4
HG_WIDTH = HG_HEADS * HEAD_DIM
ML_HEADS = 4
ML_WIDTH = ML_HEADS * HEAD_DIM
ML_CONV = 4
CROSS_HEADS = 4
CROSS_HEAD_DIM = D_MODEL // CROSS_HEADS
MLP_HIDDEN = 4 * D_MODEL

_O_AQ, _O_AK, _O_AV, _O_XQ, _O_XK, _O_XW = 0, 512, 640, 768, 1280, 1344
_O_HQ, _O_MQK, _O_MV, _O_MO, _O_MG, _IN_WIDTH = 1352, 2376, 2888, 3144, 3400, 3408

NAT_WIDTH = 128 + 128 + 1024 + 1024
TR_WIDTH = 512 + 512 + 128 + 16

LANE = 128
DSA_TILE = 256
ROW_TILE = 512
POST_TILE = 512
HG_SUB = 16
HG_ROWS = 256
HG_SEQS = 4
ML_CHUNK = 256
ML_SEQS = 4
VMEM_LIMIT = 56 * 1024 * 1024

_NT = (((1,), (1,)), ((), ()))


def _mx(a):
    return a.astype(MXU_DTYPE)


def _dot(a, b):
    return jnp.dot(_mx(a), _mx(b), preferred_element_type=F32)


def _dot_nt(a, b):
    return lax.dot_general(_mx(a), _mx(b), _NT, preferred_element_type=F32)


def _split3(x):
    hi = _mx(x)
    r1 = x - hi.astype(F32)
    mid = _mx(r1)
    lo = _mx(r1 - mid.astype(F32))
    return hi, mid, lo


def _dot01_left(m01, x):
    hi, mid, lo = _split3(x)
    f = lambda p: jnp.dot(m01, p, preferred_element_type=F32)
    return f(hi) + f(mid) + f(lo)


def _dot01_right(x, m01):
    hi, mid, lo = _split3(x)
    f = lambda p: jnp.dot(p, m01, preferred_element_type=F32)
    return f(hi) + f(mid) + f(lo)


def _sigmoid(x):
    return 1.0 / (1.0 + jnp.exp(-x))


def _silu(x):
    return x * _sigmoid(x)


def _log_sigmoid(x):
    return jnp.minimum(x, 0.0) - jnp.log(1.0 + jnp.exp(-jnp.abs(x)))


def _iota(shape, dim):
    return lax.broadcasted_iota(I32, shape, dim)


def _block_ones(n, blk):
    same = (_iota((n, n), 0) // blk) == (_iota((n, n), 1) // blk)
    return jnp.where(same, 1.0, 0.0)


def _params(*sem):
    return pltpu.CompilerParams(dimension_semantics=sem, vmem_limit_bytes=VMEM_LIMIT)


def _const_spec(shape):
    nd = len(shape)
    return pl.BlockSpec(shape, lambda *_: (0,) * nd, pipeline_mode=pl.Buffered(1))


def _in_proj_kernel(x_ref, g_ref, wn_ref, wt_ref, cosn_ref, sina_ref, sinb_ref, cost_ref, sint_ref,
                    knat_ref, hg_ref, ml_ref, qT_ref, qiT_ref, vT_ref, wgT_ref, *, idx_scale):
    tm = x_ref.shape[0]
    x = x_ref[...]
    ms = jnp.mean(x * x, axis=-1, keepdims=True)
    h = _mx(x * lax.rsqrt(ms + RMS_EPS) * g_ref[...])
    nat = jnp.dot(h, wn_ref[...], preferred_element_type=F32)
    tr = lax.dot_general(wt_ref[...], h, _NT, preferred_element_type=F32)

    cosn, sina, sinb = cosn_ref[...], sina_ref[...], sinb_ref[...]

    def rope_nat(z):
        return z * cosn + pltpu.roll(z, 96, 1) * sina + pltpu.roll(z, 32, 1) * sinb

    knat_ref[:, 0:128] = rope_nat(nat[:, 0:128]).astype(knat_ref.dtype)
    knat_ref[:, 128:256] = rope_nat(nat[:, 128:256]).astype(knat_ref.dtype)
    hg_ref[...] = nat[:, 256:1280]
    ml_ref[...] = nat[:, 1280:NAT_WIDTH]

    cost, sint = cost_ref[...], sint_ref[...]

    def rope_t(z, scale):
        outs = []
        for hd in range(8):
            x1 = z[hd * 64:hd * 64 + 32]
            x2 = z[hd * 64 + 32:hd * 64 + 64]
            outs.append((x1 * cost - x2 * sint) * scale)
            outs.append((x2 * cost + x1 * sint) * scale)
        return jnp.concatenate(outs, axis=0)

    qT = rope_t(tr[0:512], (HEAD_DIM ** -0.5) * LOG2E).astype(qT_ref.dtype)
    qiT = rope_t(tr[512:1024], 1.0).astype(qiT_ref.dtype)
    vT = tr[1024:1152].astype(vT_ref.dtype)
    wg = tr[1152:1168]
    wg = jnp.where(_iota(wg.shape, 0) < 8, wg * idx_scale, wg)
    for c in range(tm // DSA_TILE):
        sl = slice(c * DSA_TILE, (c + 1) * DSA_TILE)
        qT_ref[c] = qT[:, sl]
        qiT_ref[c] = qiT[:, sl]
        vT_ref[c] = vT[:, sl]
        wgT_ref[c] = wg[:, sl]


def _in_proj(x2, g, wn, wt, tabs, seq):
    n = x2.shape[0]
    tm = ROW_TILE
    nt_seq = seq // tm
    cosn, sina, sinb, cost, sint = tabs
    idx_scale = (IDX_HEADS ** -0.5) * (HEAD_DIM ** -0.5)
    row = lambda w: pl.BlockSpec((tm, w), lambda i: (i, 0))
    tabn = pl.BlockSpec((tm, LANE), lambda i: (i % nt_seq, 0))
    tabt = pl.BlockSpec((32, tm), lambda i: (0, i % nt_seq))
    t3 = lambda r: pl.BlockSpec((tm // DSA_TILE, r, DSA_TILE), lambda i: (i, 0, 0))
    nl = n // DSA_TILE
    return pl.pallas_call(
        functools.partial(_in_proj_kernel, idx_scale=idx_scale),
        grid=(n // tm,),
        in_specs=[row(D_MODEL), _const_spec((1, D_MODEL)), _const_spec((D_MODEL, NAT_WIDTH)),
                  _const_spec((TR_WIDTH, D_MODEL)), tabn, tabn, tabn, tabt, tabt],
        out_specs=[row(256), row(1024), row(1024), t3(512), t3(512), t3(128), t3(16)],
        out_shape=[jax.ShapeDtypeStruct((n, 256), MXU_DTYPE),
                   jax.ShapeDtypeStruct((n, 1024), F32),
                   jax.ShapeDtypeStruct((n, 1024), F32),
                   jax.ShapeDtypeStruct((nl, 512, DSA_TILE), MXU_DTYPE),
                   jax.ShapeDtypeStruct((nl, 512, DSA_TILE), MXU_DTYPE),
                   jax.ShapeDtypeStruct((nl, 128, DSA_TILE), MXU_DTYPE),
                   jax.ShapeDtypeStruct((nl, 16, DSA_TILE), F32)],
        compiler_params=_params("parallel"),
        name="in_proj",
    )(x2, g, wn, wt, cosn, sina, sinb, cost, sint)


def _dsa_kernel(qiT_ref, wgT_ref, knat_ref, qT_ref, vT_ref, o_ref,
                key_scr, hi_scr, bias_scr, acc_scr, thr_scr, nge_scr, kmax_scr, *, topk):
    tq = DSA_TILE
    j = pl.program_id(1)
    nk = j + 1
    int_min = jnp.int32(-2 ** 31)
    srow = _iota((tq, tq), 0)
    tcol = _iota((tq, tq), 1)
    causal = srow <= tcol

    w = wgT_ref[0]
    qi = qiT_ref[0]
    qi_wide = jnp.concatenate([qi[hd * 64:(hd + 1) * 64] for hd in range(IDX_HEADS)], axis=1)

    def score_tile(kt, carry):
        kblk = knat_ref[kt][:, 128:192]
        acc = jnp.zeros((tq, tq), F32)
        ys = jnp.dot(kblk, qi_wide, preferred_element_type=F32)
        for hd in range(IDX_HEADS):
            acc = acc + w[hd:hd + 1, :] * jnp.maximum(ys[:, hd * tq:(hd + 1) * tq], 0.0)
        acc = jnp.where(jnp.abs(acc) < F32_MIN_NORMAL, 0.0, acc)
        acc = jnp.where(jnp.logical_or(kt < j, causal), acc, -jnp.inf)
        bits = pltpu.bitcast(acc, I32)
        key_scr[kt] = bits ^ ((bits >> 31) & jnp.int32(0x7FFFFFFF))
        hi_scr[kt] = pltpu.bitcast(bits & jnp.int32(-65536), F32).astype(jnp.bfloat16)
        return carry

    lax.fori_loop(0, nk, score_tile, 0)

    key16_neg_inf = (0xFF80 ^ 0x7FFF) - 65536
    one_h = jnp.ones((16, tq), jnp.bfloat16)
    zero_h = jnp.zeros((16, tq), jnp.bfloat16)

    def select_threshold(nks):
        def count16(cand16):
            b16 = cand16 ^ ((cand16 >> 15) & 0x7FFF)
            b16 = jnp.where((cand16 > 0) & (cand16 < 0x80), 0x80, b16)
            cb = jnp.broadcast_to(pltpu.bitcast(b16 << 16, F32).astype(jnp.bfloat16), (16, tq))
            accs = [zero_h] * 4
            n = 0
            for kt in range(nks):
                for r in range(tq // 16):
                    ind = jnp.where(hi_scr[kt, r * 16:(r + 1) * 16, :] >= cb, one_h, zero_h)
                    accs[n % 4] = accs[n % 4] + ind
                    n += 1
            tot = (accs[0] + accs[1]).astype(F32) + (accs[2] + accs[3]).astype(F32)
            return jnp.sum(tot, axis=0, keepdims=True)

        def count32(pred_fn):
            accs = [jnp.zeros((8, tq), I32)] * 4
            n = 0
            for kt in range(nks):
                for r in range(tq // 8):
                    ind = jnp.where(pred_fn(key_scr[kt, r * 8:(r + 1) * 8, :]), 1, 0).astype(I32)
                    accs[n % 4] = accs[n % 4] + ind
                    n += 1
            return jnp.sum((accs[0] + accs[1]) + (accs[2] + accs[3]), axis=0, keepdims=True)

        def stage16(i, lo):
            cand = lo + (jnp.int32(1) << (15 - i))
            ok = (count16(cand) >= topk) | (cand <= key16_neg_inf)
            return jnp.where(ok, cand, lo)

        lo16 = lax.fori_loop(0, 16, stage16, jnp.full((1, tq), -32768, I32))

        def stage32(i, carry):
            lo, _ = carry
            cand = lo + (jnp.int32(1) << (15 - i))
            cnt = count32(lambda k: k >= cand)
            ok = cnt >= topk
            return jnp.where(ok, cand, lo), jnp.where(ok, cnt, carry[1])

        thr, n_ge = lax.fori_loop(0, 16, stage32, (lo16 << 16, jnp.full((1, tq), nks * tq, I32)))
        thr_scr[...] = jnp.broadcast_to(thr, thr_scr.shape)
        nge_scr[...] = jnp.broadcast_to(n_ge, nge_scr.shape)

    for jj in range(key_scr.shape[0]):
        @pl.when(j == jj)
        def _(jj=jj):
            if (jj + 1) * tq <= topk:
                thr_scr[...] = jnp.full(thr_scr.shape, int_min, I32)
                nge_scr[...] = jnp.full(nge_scr.shape, topk, I32)
            else:
                select_threshold(jj + 1)

    thr = thr_scr[0:1, :]
    n_ge = nge_scr[0:1, :]

    q = qT_ref[0]

    @pl.when(j == 0)
    def _():
        group_ones = _mx(_block_ones(LANE, HEAD_DIM))

        def body(kt, m):
            k = knat_ref[kt][:, 0:128].astype(F32)
            n2 = jnp.dot(_mx(k * k), group_ones, preferred_element_type=F32)
            return jnp.maximum(m, jnp.max(n2, axis=0, keepdims=True))

        kmax2 = lax.fori_loop(0, key_scr.shape[0], body, jnp.zeros((1, LANE), F32))
        kmax_scr[...] = jnp.broadcast_to(kmax2, kmax_scr.shape)

    qf = q.astype(F32)
    bound2 = jnp.zeros((1, tq), F32)
    for hd in range(ATT_HEADS):
        g = hd // ATT_GROUP
        qn2 = jnp.sum(qf[hd * 64:(hd + 1) * 64] * qf[hd * 64:(hd + 1) * 64], axis=0, keepdims=True)
        bound2 = jnp.maximum(bound2, qn2 * kmax_scr[0:1, g * HEAD_DIM:g * HEAD_DIM + 1])
    bound = BOUND_MARGIN * jnp.sqrt(bound2)
    bound_is_small = jnp.max(bound) < SAFE_LOGIT_BOUND
    neg_bound = -bound

    def mask_plain():
        def body(kt, carry):
            bias_scr[kt] = jnp.where(key_scr[kt] >= thr, neg_bound, NEG_BIG)
            return carry
        lax.fori_loop(0, nk, body, 0)

    def mask_ties():
        def count_gt(kt, c):
            ind = jnp.where(key_scr[kt] > thr, 1.0, 0.0)
            return c + jnp.sum(ind, axis=0, keepdims=True)
        n_gt = lax.fori_loop(0, nk, count_gt, jnp.zeros((1, tq), F32))
        need = topk - n_gt
        tri = _mx(jnp.where(srow >= tcol, 1.0, 0.0))

        def body(kt, off):
            k = key_scr[kt]
            eq = k == thr
            eqf = jnp.where(eq, 1.0, 0.0)
            rank = jnp.dot(tri, _mx(eqf), preferred_element_type=F32) + off
            tie_bias = jnp.where(rank <= need, neg_bound, NEG_BIG)
            bias_scr[kt] = jnp.where(k > thr, neg_bound, jnp.where(eq, tie_bias, NEG_BIG))
            return off + jnp.sum(eqf, axis=0, keepdims=True)
        lax.fori_loop(0, nk, body, jnp.zeros((1, tq), F32))

    no_excess_ties = jnp.max(jnp.abs(n_ge - topk)) == 0
    lax.cond(no_excess_ties, mask_plain, mask_ties)
    bias_scr[j] = jnp.where(causal, bias_scr[j], NEG_BIG)

    acc_scr[...] = jnp.zeros(acc_scr.shape, F32)
    zeros64 = jnp.zeros((64, tq), q.dtype)
    qwide = []
    for g in range(ATT_KV_HEADS):
        cols = []
        for hh in range(ATT_GROUP):
            hd = g * ATT_GROUP + hh
            qh = q[hd * 64:(hd + 1) * 64]
            cols.append(jnp.concatenate([qh, zeros64] if g == 0 else [zeros64, qh], axis=0))
        qwide.append(jnp.concatenate(cols, axis=1))

    ones_rows = jnp.ones((16, tq), q.dtype)

    def attend(kt, carry):
        ms, ls = carry
        kb = knat_ref[kt][:, 0:128]
        vtb = vT_ref[kt]
        bias = bias_scr[kt]
        lts = [jnp.dot(kb, qwide[g], preferred_element_type=F32) for g in range(ATT_KV_HEADS)]
        new_m, new_l = [], []
        for g in range(ATT_KV_HEADS):
            lt = jnp.concatenate([lts[g][:, hh * tq:(hh + 1) * tq] + bias for hh in range(ATT_GROUP)], axis=1)
            m_new = jnp.maximum(ms[g], jnp.max(lt, axis=0, keepdims=True))
            alpha = jnp.exp2(ms[g] - m_new)
            p = _mx(jnp.exp2(lt - m_new))
            v_aug = jnp.concatenate([vtb[g * 64:(g + 1) * 64], ones_rows], axis=0)
            pv = jnp.dot(v_aug, p, preferred_element_type=F32)
            new_l.append(alpha * ls[g] + pv[HEAD_DIM:HEAD_DIM + 1, :])
            acc_scr[g, 0:HEAD_DIM, :] = alpha * acc_scr[g, 0:HEAD_DIM, :] + pv[0:HEAD_DIM, :]
            new_m.append(m_new)
        return tuple(new_m), tuple(new_l)

    wide = ATT_GROUP * tq

    def run_online():
        init = (tuple(jnp.full((1, wide), NEG_BIG, F32) for _ in range(ATT_KV_HEADS)),
                tuple(jnp.zeros((1, wide), F32) for _ in range(ATT_KV_HEADS)))
        _, ls = lax.fori_loop(0, nk, attend, init)
        for g in range(ATT_KV_HEADS):
            acc_scr[g, HEAD_DIM:, :] = jnp.broadcast_to(ls[g], (16, wide))

    def attend_bounded(kt, carry):
        kb = knat_ref[kt][:, 0:128]
        vtb = vT_ref[kt]
        bias = bias_scr[kt]
        for g in range(ATT_KV_HEADS):
            lt = jnp.dot(kb, qwide[g], preferred_element_type=F32)
            p = _mx(jnp.exp2(jnp.concatenate([lt[:, hh * tq:(hh + 1) * tq] + bias
                                              for hh in range(ATT_GROUP)], axis=1)))
            v_aug = jnp.concatenate([vtb[g * 64:(g + 1) * 64], ones_rows], axis=0)
            acc_scr[g] = acc_scr[g] + jnp.dot(v_aug, p, preferred_element_type=F32)
        return carry

    def run_bounded():
        def pair(i, carry):
            attend_bounded(2 * i, carry)
            return attend_bounded(2 * i + 1, carry)

        lax.fori_loop(0, nk // 2, pair, 0)

        @pl.when(nk % 2 == 1)
        def _():
            attend_bounded(nk - 1, 0)

    lax.cond(bound_is_small, run_bounded, run_online)
    outs = []
    for g in range(ATT_KV_HEADS):
        og = acc_scr[g, 0:HEAD_DIM, :] / acc_scr[g, HEAD_DIM:HEAD_DIM + 1, :]
        outs += [og[:, hh * tq:(hh + 1) * tq] for hh in range(ATT_GROUP)]
    o_ref[...] = jnp.concatenate(outs, axis=0).T.astype(o_ref.dtype)


def _dsa(qiT3, wgT3, knat3, qT3, vT3, batch, seq):
    tq = DSA_TILE
    nq = seq // tq
    n = batch * seq
    topk = min(INDEX_TOPK_MAX, seq // 4)
    qtile = lambda r: pl.BlockSpec((1, r, tq), lambda b, j: (b * nq + j, 0, 0))
    return pl.pallas_call(
        functools.partial(_dsa_kernel, topk=topk),
        grid=(batch, nq),
        in_specs=[qtile(512), qtile(16),
                  pl.BlockSpec((nq, tq, 256), lambda b, j: (b, 0, 0)),
                  qtile(512),
                  pl.BlockSpec((nq, 128, tq), lambda b, j: (b, 0, 0))],
        out_specs=pl.BlockSpec((tq, 512), lambda b, j: (b * nq + j, 0)),
        out_shape=jax.ShapeDtypeStruct((n, 512), MXU_DTYPE),
        scratch_shapes=[pltpu.VMEM((nq, tq, tq), I32), pltpu.VMEM((nq, tq, tq), jnp.bfloat16),
                        pltpu.VMEM((nq, tq, tq), F32),
                        pltpu.VMEM((ATT_KV_HEADS, HEAD_DIM + 16, ATT_GROUP * tq), F32),
                        pltpu.VMEM((8, tq), I32), pltpu.VMEM((8, tq), I32), pltpu.VMEM((8, LANE), F32)],
        compiler_params=_params("parallel", "arbitrary"),
        name="dsa",
    )(qiT3, wgT3, knat3, qT3, vT3)


def _hgrn_kernel(hg_ref, lb_ref, gn_ref, o_ref, *scratch):
    st_scr, z_scr = scratch[0], scratch[7]

    @pl.when(pl.program_id(1) == 0)
    def _():
        st_scr[...] = jnp.zeros(st_scr.shape, F32)
        z_scr[...] = jnp.zeros(z_scr.shape, z_scr.dtype)

    seqs = [_hgrn_sequence(hg_ref.at[bi], lb_ref, gn_ref, o_ref.at[bi], *(s.at[bi] for s in scratch))
            for bi in range(HG_SEQS)]
    while seqs:
        for seq in list(seqs):
            if next(seq, "done") == "done":
                seqs.remove(seq)


def _hgrn_sequence(hg_ref, lb_ref, gn_ref, o_ref, st_scr, q_scr, k_scr, g_scr, qg_scr, kd_scr, o_scr, z_scr,
                   snap_scr):
    rows = hg_ref.shape[0]
    c = HG_SUB
    w = HG_WIDTH

    lb = lb_ref[...]
    q = _silu(hg_ref[:, 0:w]) * (HEAD_DIM ** -0.5)
    forget = lb + (1.0 - lb) * _sigmoid(hg_ref[:, w:2 * w])
    logf = jnp.log2(forget)
    kk = 1.0 - forget
    v = hg_ref[:, 2 * w:3 * w]

    ri = _iota((rows, rows), 0)
    ci = _iota((rows, rows), 1)
    same = (ri // c) == (ci // c)
    tri = _mx(jnp.where(same & (ci <= ri), 1.0, 0.0))
    ones_blk = _mx(jnp.where(same, 1.0, 0.0))
    gl = _dot01_left(tri, logf)
    glast = _dot01_left(ones_blk, logf)
    q_scr[...] = q
    k_scr[...] = gl - jnp.log2(kk)
    g_scr[...] = gl
    qg_scr[...] = q * jnp.exp2(gl)
    kd_scr[...] = kk * jnp.exp2(glast - gl)
    dec = jnp.exp2(glast)
    vT = _mx(v.T)
    head_ones = _mx(_block_ones(w, HEAD_DIM))
    pair_mask = _block_ones(LANE, HEAD_DIM)
    half = c // 2
    rhalf = _iota((half, w), 0)
    nsc = rows // c
    yield

    for sc in range(nsc):
        z_scr[sc, sc * c:(sc + 1) * c, :] = kd_scr[sc * c:(sc + 1) * c, :].astype(z_scr.dtype)
    for p in range(HG_HEADS // 2):
        ps = slice(p * LANE, (p + 1) * LANE)
        st = st_scr[p]
        for sc in range(nsc):
            snap_scr[sc, p] = st.astype(snap_scr.dtype)
            upd = jnp.dot(vT[ps, :], z_scr[sc, :, ps], preferred_element_type=F32)
            st = st * dec[sc * c:sc * c + 1, ps] + upd * pair_mask
        st_scr[p] = st
        yield

    for sc in range(nsc):
        r0 = sc * c
        q_top, q_bot = q_scr[r0:r0 + half, :], q_scr[r0 + half:r0 + c, :]
        g_top, g_bot = g_scr[r0:r0 + half, :], g_scr[r0 + half:r0 + c, :]
        bcast = lambda row: jnp.broadcast_to(row, (half, w))
        xs_top, xs_bot = [], []
        for s in range(c):
            hsr = bcast(k_scr[r0 + s:r0 + s + 1, :])
            if s < half:
                xs_top.append(q_top * jnp.exp2(jnp.where(rhalf >= s, g_top - hsr, -jnp.inf)))
                xs_bot.append(q_bot * jnp.exp2(g_bot - hsr))
            else:
                xs_bot.append(q_bot * jnp.exp2(jnp.where(rhalf >= s - half, g_bot - hsr, -jnp.inf)))
        y = jnp.dot(_mx(jnp.concatenate(xs_top + xs_bot, axis=0)), head_ones, preferred_element_type=F32)
        o_top = jnp.zeros((half, w), F32)
        o_bot = jnp.zeros((half, w), F32)
        for s in range(c):
            vsr = bcast(hg_ref[r0 + s:r0 + s + 1, 2 * w:3 * w])
            if s < half:
                o_top = o_top + y[s * half:(s + 1) * half, :] * vsr
            o_bot = o_bot + y[(half + s) * half:(half + s + 1) * half, :] * vsr
        o_diag = jnp.concatenate([o_top, o_bot], axis=0)
        o_pairs = [_dot_nt(qg_scr[r0:r0 + c, p * LANE:(p + 1) * LANE], snap_scr[sc, p])
                   for p in range(HG_HEADS // 2)]
        o_scr[r0:r0 + c, :] = jnp.concatenate(o_pairs, axis=1) + o_diag
        yield

    o = o_scr[...]
    ms = _dot01_right(o * o, head_ones) * (1.0 / HEAD_DIM)
    y = o * lax.rsqrt(ms + RMS_EPS) * gn_ref[...]
    o_ref[...] = (y * _silu(hg_ref[:, 3 * w:4 * w])).astype(o_ref.dtype)


def _hgrn(hg, lb, gn, batch, seq):
    rows = HG_ROWS
    ns = seq // rows
    n = batch * seq
    nb = HG_SEQS
    assert batch % nb == 0
    vm = lambda: pltpu.VMEM((nb, rows, HG_WIDTH), F32)
    out = pl.pallas_call(
        _hgrn_kernel,
        grid=(batch // nb, ns),
        in_specs=[pl.BlockSpec((nb, rows, 1024), lambda b, s: (b, s, 0)),
                  _const_spec((1, HG_WIDTH)), _const_spec((1, HG_WIDTH))],
        out_specs=pl.BlockSpec((nb, rows, HG_WIDTH), lambda b, s: (b, s, 0)),
        out_shape=jax.ShapeDtypeStruct((batch, seq, HG_WIDTH), MXU_DTYPE),
        scratch_shapes=[pltpu.VMEM((nb, HG_HEADS // 2, LANE, LANE), F32), vm(), vm(), vm(), vm(), vm(), vm(),
                        pltpu.VMEM((nb, rows // HG_SUB, rows, HG_WIDTH), MXU_DTYPE),
                        pltpu.VMEM((nb, rows // HG_SUB, HG_HEADS // 2, LANE, LANE), MXU_DTYPE)],
        compiler_params=_params("parallel", "arbitrary"),
        name="hgrn",
    )(hg.reshape(batch, seq, 1024), lb, gn)
    return out.reshape(n, HG_WIDTH)


def _mlstm_kernel(ml_ref, *refs):
    gT_refs = refs[:ML_SEQS]
    cw_ref, cb_ref, brow_ref, gn_ref, o_ref, prev_scr, c_scr, n_scr, m64_scr, m128_scr = refs[ML_SEQS:]

    @pl.when(pl.program_id(1) == 0)
    def _():
        prev_scr[...] = jnp.zeros(prev_scr.shape, F32)
        c_scr[...] = jnp.zeros(c_scr.shape, F32)
        n_scr[...] = jnp.zeros(n_scr.shape, F32)
        m64_scr[...] = jnp.full(m64_scr.shape, M_INIT, F32)
        m128_scr[...] = jnp.full(m128_scr.shape, M_INIT, F32)

    seqs = [_mlstm_sequence(ml_ref.at[bi], gT_ref, cw_ref, cb_ref, brow_ref, gn_ref, o_ref.at[bi],
                            prev_scr.at[bi], c_scr.at[bi], n_scr.at[bi], m64_scr.at[bi], m128_scr.at[bi])
            for bi, gT_ref in enumerate(gT_refs)]
    while seqs:
        for seq in list(seqs):
            if next(seq, "done") == "done":
                seqs.remove(seq)


def _mlstm_sequence(ml_ref, gT_ref, cw_ref, cb_ref, brow_ref, gn_ref, o_ref,
                    prev_scr, c_scr, n_scr, m64_scr, m128_scr):
    L = ML_CHUNK
    w = ML_WIDTH

    xqk = ml_ref[:, 0:512]
    prev = prev_scr[...]
    rows = _iota((L, 512), 0)
    acc = cb_ref[...] + cw_ref[ML_CONV - 1:ML_CONV, :] * xqk
    for sh in range(1, ML_CONV):
        shifted = jnp.where(rows >= sh, pltpu.roll(xqk, sh, 0), pltpu.roll(prev, sh, 0))
        acc = acc + cw_ref[ML_CONV - 1 - sh:ML_CONV - sh, :] * shifted
    prev_scr[...] = xqk
    qk = _silu(acc)
    q = qk[:, 0:w]
    k = qk[:, w:2 * w] * (HEAD_DIM ** -0.5)
    v = ml_ref[:, 512:768]

    srow = _iota((L, L), 0)
    scol = _iota((L, L), 1)
    tril = srow >= scol
    triu = _mx(jnp.where(srow <= scol, 1.0, 0.0))
    grow = gT_ref[0] + brow_ref[...]
    brow = _dot01_right(_log_sigmoid(grow), triu)
    yield

    def expand(x, width, row0):
        src = _iota((16, ML_HEADS * width), 0)
        dst = _iota((16, ML_HEADS * width), 1)
        e01 = _mx(jnp.where(src == row0 + dst // width, 1.0, 0.0))
        f = lambda p: lax.dot_general(p, e01, (((0,), (0,)), ((), ())), preferred_element_type=F32)
        hi, mid, lo = _split3(x)
        return f(hi) + f(mid) + f(lo)

    b128 = expand(brow, L, 12)
    i128 = expand(grow, L, 8)
    b64 = expand(brow, 64, 12)
    i64 = expand(grow, 64, 8)
    yield

    lane = _iota((L, LANE), 1)
    first = lane < HEAD_DIM
    pair_ones = _mx(_block_ones(LANE, HEAD_DIM))
    pair_mask = _block_ones(LANE, HEAD_DIM)
    m_prev128 = m128_scr[0:1, :]
    m_prev64 = m64_scr[0:1, :]
    n_row = n_scr[0:1, :]

    houts = []
    for p in range(ML_HEADS // 2):
        ps = slice(p * LANE, (p + 1) * LANE)
        q_pair, k_pair, v_pair = q[:, ps], k[:, ps], v[:, ps]
        per_head = []
        for hh in range(2):
            hd = 2 * p + hh
            hs = slice(hd * L, (hd + 1) * L)
            bt = b128[:, hs]
            d = bt - brow[12 + hd:13 + hd, :] + grow[8 + hd:9 + hd, :]
            d = jnp.where(tril, d, -jnp.inf)
            inter = bt + m_prev128[:, hs]
            m_t = jnp.maximum(inter, jnp.max(d, axis=1, keepdims=True))
            wgt = jnp.exp(d - m_t)
            a = jnp.exp(inter - m_t)
            qm = jnp.where(first if hh == 0 else ~first, q_pair, 0.0)
            s = _dot_nt(qm, k_pair) * wgt
            sv = _dot(s, v_pair)
            rs = jnp.sum(s, axis=1, keepdims=True)
            per_head.append((m_t[:, 0:LANE], a[:, 0:LANE], sv, rs))
            yield
        sel = lambda i: jnp.where(first, per_head[0][i], per_head[1][i])
        m_t, a, sv, rs = sel(0), sel(1), sel(2), sel(3)
        qc = _dot(q_pair, c_scr[p])
        qn = _dot(q_pair * n_row[:, ps], pair_ones)
        num = a * qc + sv
        den = a * qn + rs
        houts.append(num / jnp.maximum(jnp.abs(den), jnp.exp(-m_t)))
        yield
    hout = jnp.concatenate(houts, axis=1)

    def new_m(bx, ix, m_prev):
        blast = bx[L - 1:L, :]
        wl = blast - bx + ix
        m_new = jnp.maximum(blast + m_prev, jnp.max(wl, axis=0, keepdims=True))
        return blast, wl, m_new

    blast, wl, m_new64 = new_m(b64, i64, m_prev64)
    dec = jnp.exp(blast + m_prev64 - m_new64)
    kw = k * jnp.exp(wl - m_new64)
    for p in range(ML_HEADS // 2):
        ps = slice(p * LANE, (p + 1) * LANE)
        upd = jnp.dot(_mx(kw[:, ps].T), _mx(v[:, ps]), preferred_element_type=F32)
        c_scr[p] = c_scr[p] * dec[:, ps] + upd * pair_mask
    n_scr[...] = jnp.broadcast_to(dec * n_row + jnp.sum(kw, axis=0, keepdims=True), n_scr.shape)
    m64_scr[...] = jnp.broadcast_to(m_new64, m64_scr.shape)
    _, _, m_new128 = new_m(b128, i128, m_prev128)
    m128_scr[...] = jnp.broadcast_to(m_new128, m128_scr.shape)

    head_ones = _mx(_block_ones(w, HEAD_DIM))
    ms = _dot01_right(hout * hout, head_ones) * (1.0 / HEAD_DIM)
    y = hout * lax.rsqrt(ms + RMS_EPS) * gn_ref[...]
    o_ref[...] = (y * _sigmoid(ml_ref[:, 768:1024])).astype(o_ref.dtype)


def _mlstm(ml, wgT3, cw, cb, brow, gn, batch, seq):
    L = ML_CHUNK
    nc = seq // L
    n = batch * seq
    per_tile = DSA_TILE // L
    nb = ML_SEQS
    assert batch % nb == 0

    def gate_spec(off):
        def index(b, s):
            chunk = (nb * b + off) * nc + s
            return chunk // per_tile, 0, chunk % per_tile
        return pl.BlockSpec((1, 16, L), index)

    out = pl.pallas_call(
        _mlstm_kernel,
        grid=(batch // nb, nc),
        in_specs=[pl.BlockSpec((nb, L, 1024), lambda b, s: (b, s, 0))] + [gate_spec(o) for o in range(nb)]
                 + [_const_spec((ML_CONV, 512)), _const_spec((1, 512)),
                    _const_spec((16, L)), _const_spec((1, ML_WIDTH))],
        out_specs=pl.BlockSpec((nb, L, ML_WIDTH), lambda b, s: (b, s, 0)),
        out_shape=jax.ShapeDtypeStruct((batch, seq, ML_WIDTH), MXU_DTYPE),
        scratch_shapes=[pltpu.VMEM((nb, L, 512), F32), pltpu.VMEM((nb, 2, LANE, LANE), F32),
                        pltpu.VMEM((nb, 8, ML_WIDTH), F32), pltpu.VMEM((nb, 8, ML_WIDTH), F32),
                        pltpu.VMEM((nb, 8, ML_HEADS * L), F32)],
        compiler_params=_params("parallel", "arbitrary"),
        name="mlstm",
    )(ml.reshape(batch, seq, 1024), *([wgT3] * nb), cw, cb, brow, gn)
    return out.reshape(n, ML_WIDTH)


def _rms(y, g):
    ms = jnp.mean(y * y, axis=-1, keepdims=True)
    return y * lax.rsqrt(ms + RMS_EPS) * g


def _mem_kv_kernel(m_ref, g_ref, w_ref, k_ref, v_ref):
    h = _mx(_rms(m_ref[...], g_ref[...]))
    kv = jnp.dot(h, w_ref[...], preferred_element_type=F32)
    k_ref[...] = kv[:, 0:D_MODEL].astype(k_ref.dtype)
    v_ref[...] = kv[:, D_MODEL:2 * D_MODEL].astype(v_ref.dtype)


def _mem_kv(mem2, g, w):
    n = mem2.shape[0]
    tm = ROW_TILE
    row = pl.BlockSpec((tm, D_MODEL), lambda i: (i, 0))
    return pl.pallas_call(
        _mem_kv_kernel,
        grid=(n // tm,),
        in_specs=[row, _const_spec((1, D_MODEL)), _const_spec((D_MODEL, 2 * D_MODEL))],
        out_specs=[row, row],
        out_shape=[jax.ShapeDtypeStruct((n, D_MODEL), MXU_DTYPE)] * 2,
        compiler_params=_params("parallel"),
        name="mem_kv",
    )(mem2, g, w)


def _post_mix_kernel(a_ref, b_ref, c_ref, x_ref, k_ref, v_ref, wout_ref, wq_ref, wo_ref, wu_ref, wd_ref,
                     g_ref, o_ref):
    g = g_ref[...]
    y = (jnp.dot(a_ref[...], wout_ref[0:512, :], preferred_element_type=F32)
         + jnp.dot(b_ref[...], wout_ref[512:768, :], preferred_element_type=F32)
         + jnp.dot(c_ref[...], wout_ref[768:1024, :], preferred_element_type=F32))
    x = x_ref[...] + _rms(y, g[0:1])
    h = _mx(_rms(x, g[1:2]))
    q = jnp.dot(h, wq_ref[...], preferred_element_type=F32) * (CROSS_HEAD_DIM ** -0.5)
    outs = []
    for hd in range(CROSS_HEADS):
        hs = slice(hd * CROSS_HEAD_DIM, (hd + 1) * CROSS_HEAD_DIM)
        logits = _dot_nt(q[:, hs], k_ref[:, hs])
        p = jnp.exp(logits - jnp.max(logits, axis=-1, keepdims=True))
        o = jnp.dot(_mx(p), v_ref[:, hs], preferred_element_type=F32)
        outs.append(o / jnp.sum(p, axis=-1, keepdims=True))
    y = jnp.dot(_mx(jnp.concatenate(outs, axis=1)), wo_ref[...], preferred_element_type=F32)
    x = x + _rms(y, g[2:3])
    h = _mx(_rms(x, g[3:4]))
    y = jnp.zeros(x.shape, F32)
    for c in range(MLP_HIDDEN // D_MODEL):
        cs = slice(c * D_MODEL, (c + 1) * D_MODEL)
        u = jnp.maximum(jnp.dot(h, wu_ref[:, cs], preferred_element_type=F32), 0.0)
        y = y + jnp.dot(_mx(u * u), wd_ref[cs, :], preferred_element_type=F32)
    o_ref[...] = x + _rms(y, g[4:5])


def _post_mix(a, b, c, x2, kmem, vmem, wout, wq, wo, wu, wd, gains, seq, n_mem):
    n = x2.shape[0]
    tm = POST_TILE
    per_b = seq // tm
    row = lambda width: pl.BlockSpec((tm, width), lambda i: (i, 0))
    memb = pl.BlockSpec((n_mem, D_MODEL), lambda i: (i // per_b, 0))
    sq = _const_spec((D_MODEL, D_MODEL))
    return pl.pallas_call(
        _post_mix_kernel,
        grid=(n // tm,),
        in_specs=[row(512), row(256), row(256), row(D_MODEL), memb, memb, sq, sq, sq,
                  _const_spec((D_MODEL, MLP_HIDDEN)), _const_spec((MLP_HIDDEN, D_MODEL)),
                  _const_spec((8, D_MODEL))],
        out_specs=row(D_MODEL),
        out_shape=jax.ShapeDtypeStruct((n, D_MODEL), F32),
        compiler_params=_params("parallel"),
        name="post_mix",
    )(a, b, c, x2, kmem, vmem, wout, wq, wo, wu, wd, gains)


def _rope_tables(seq):
    half = HEAD_DIM // 2
    inv = ROPE_THETA ** (-jnp.arange(0, HEAD_DIM, 2, dtype=F32) / HEAD_DIM)
    ang = jnp.arange(seq).astype(F32)[:, None] * inv[None, :]
    cos, sin = jnp.cos(ang), jnp.sin(ang)
    lane = jnp.arange(LANE)
    freq = (lane % HEAD_DIM) % half
    first = (lane % HEAD_DIM) < half
    cosn = cos[:, freq]
    sinn = sin[:, freq]
    sina = jnp.where(first[None, :], -sinn, 0.0)
    sinb = jnp.where(first[None, :], 0.0, sinn)
    return cosn, sina, sinb, cos.T, sin.T


def _layout_w_in(w_in):
    depth = w_in.shape[0]
    z = lambda wd: jnp.zeros((depth, D_MODEL, wd), w_in.dtype)
    col = lambda a, b: w_in[:, :, a:b]
    wn = jnp.concatenate([col(_O_AK, _O_AV), col(_O_XK, _O_XW), z(64),
                          col(_O_HQ, _O_MQK), col(_O_MQK, _O_MG)], axis=-1)
    wt = jnp.concatenate([col(_O_AQ, _O_AK), col(_O_XQ, _O_XK), col(_O_AV, _O_XQ),
                          col(_O_XW, _O_HQ), col(_O_MG, _IN_WIDTH)], axis=-1)
    return _mx(wn), _mx(jnp.swapaxes(wt, 1, 2))


def kernel(x, mem, mix_pre_g, w_in, ml_conv_w, ml_conv_b, ml_gate_b, hg_lb, hg_norm_g, ml_norm_g,
           w_out, mix_post_g, cross_pre_g, mem_norm_g, w_cq, w_ckv, w_co, cross_post_g, mlp_pre_g,
           w_up, w_down, mlp_post_g):
    batch, seq, _ = x.shape
    n_mem = mem.shape[1]
    depth = w_in.shape[0]
    n = batch * seq
    assert seq % max(ROW_TILE, POST_TILE, HG_ROWS, ML_CHUNK, DSA_TILE) == 0
    assert (batch * n_mem) % ROW_TILE == 0
    assert ROW_TILE % DSA_TILE == 0 and DSA_TILE % ML_CHUNK == 0

    lbs = jnp.cumsum(jax.nn.softmax(hg_lb.astype(F32), axis=0), axis=0)
    lbs = lbs - lbs[0:1]

    tabs = _rope_tables(seq)
    wn_all, wt_all = _layout_w_in(w_in)
    row = lambda a: a.astype(F32)[None, :]
    brow = jnp.pad(ml_gate_b.astype(F32), ((0, 0), (8, 0)))

    x2 = x.reshape(n, D_MODEL)
    mem2 = mem.reshape(batch * n_mem, D_MODEL)
    nl = n // DSA_TILE
    for l in range(depth):
        knat, hg, ml, qT3, qiT3, vT3, wgT3 = _in_proj(x2, row(mix_pre_g[l]), wn_all[l], wt_all[l], tabs, seq)
        a_out = _dsa(qiT3, wgT3, knat.reshape(nl, DSA_TILE, 256), qT3, vT3, batch, seq)
        b_out = _hgrn(hg, row(lbs[l]), row(jnp.tile(hg_norm_g[l], HG_HEADS)), batch, seq)
        c_out = _mlstm(ml, wgT3, ml_conv_w[l].astype(F32), row(ml_conv_b[l]),
                       jnp.broadcast_to(brow[l][:, None], (16, ML_CHUNK)),
                       row(jnp.tile(ml_norm_g[l], ML_HEADS)), batch, seq)
        kmem, vmem = _mem_kv(mem2, row(mem_norm_g[l]), _mx(w_ckv[l]))
        gains = jnp.stack([mix_post_g[l], cross_pre_g[l], cross_post_g[l], mlp_pre_g[l], mlp_post_g[l]]
                          + [jnp.zeros_like(mix_post_g[l])] * 3).astype(F32)
        x2 = _post_mix(a_out, b_out, c_out, x2, kmem, vmem, _mx(w_out[l]), _mx(w_cq[l]), _mx(w_co[l]),
                       _mx(w_up[l]), _mx(w_down[l]), gains, seq, n_mem)
    return x2.reshape(batch, seq, D_MODEL)
```

```python
import functools

import jax
import jax.numpy as jnp
from jax import lax
from jax.experimental import pallas as pl
from jax.experimental.pallas import tpu as pltpu

F32 = jnp.float32
I32 = jnp.int32
MXU_DTYPE = jnp.bfloat16

D_MODEL = 1024
RMS_EPS = 1e-6
ROPE_THETA = 10000.0
NEG_BIG = -1e30
M_INIT = -1e30
LOG2E = 1.4426950408889634
F32_MIN_NORMAL = 1.1754943508222875e-38
BOUND_MARGIN = 1.05
SAFE_LOGIT_BOUND = 60.0
HEAD_DIM = 64
ATT_HEADS = 8
ATT_KV_HEADS = 2
ATT_GROUP = ATT_HEADS // ATT_KV_HEADS
IDX_HEADS = 8
INDEX_TOPK_MAX = 256
HG_HEADS = 4
HG_WIDTH = HG_HEADS * HEAD_DIM
ML_HEADS = 4
ML_WIDTH = ML_HEADS * HEAD_DIM
ML_CONV = 4
CROSS_HEADS = 4
CROSS_HEAD_DIM = D_MODEL // CROSS_HEADS
MLP_HIDDEN = 4 * D_MODEL

_O_AQ, _O_AK, _O_AV, _O_XQ, _O_XK, _O_XW = 0, 512, 640, 768, 1280, 1344
_O_HQ, _O_MQK, _O_MV, _O_MO, _O_MG, _IN_WIDTH = 1352, 2376, 2888, 3144, 3400, 3408

NAT_WIDTH = 128 + 128 + 1024 + 1024
TR_WIDTH = 512 + 512 + 128 + 16

LANE = 128
DSA_TILE = 256
ROW_TILE = 512
POST_TILE = 512
HG_SUB = 16
HG_ROWS = 128
HG_SEQS = 8
ML_CHUNK = 256
ML_SEQS = 4
VMEM_LIMIT = 56 * 1024 * 1024

_NT = (((1,), (1,)), ((), ()))


def _mx(a):
    return a.astype(MXU_DTYPE)


def _dot(a, b):
    return jnp.dot(_mx(a), _mx(b), preferred_element_type=F32)


def _dot_nt(a, b):
    return lax.dot_general(_mx(a), _mx(b), _NT, preferred_element_type=F32)


def _split3(x):
    hi = _mx(x)
    r1 = x - hi.astype(F32)
    mid = _mx(r1)
    lo = _mx(r1 - mid.astype(F32))
    return hi, mid, lo


def _dot01_left(m01, x):
    hi, mid, lo = _split3(x)
    f = lambda p: jnp.dot(m01, p, preferred_element_type=F32)
    return f(hi) + f(mid) + f(lo)


def _dot01_right(x, m01):
    hi, mid, lo = _split3(x)
    f = lambda p: jnp.dot(p, m01, preferred_element_type=F32)
    return f(hi) + f(mid) + f(lo)


def _sigmoid(x):
    return 1.0 / (1.0 + jnp.exp(-x))


def _silu(x):
    return x * _sigmoid(x)


def _log_sigmoid(x):
    return jnp.minimum(x, 0.0) - jnp.log(1.0 + jnp.exp(-jnp.abs(x)))


def _iota(shape, dim):
    return lax.broadcasted_iota(I32, shape, dim)


def _block_ones(n, blk):
    same = (_iota((n, n), 0) // blk) == (_iota((n, n), 1) // blk)
    return jnp.where(same, 1.0, 0.0)


def _params(*sem):
    return pltpu.CompilerParams(dimension_semantics=sem, vmem_limit_bytes=VMEM_LIMIT)


def _const_spec(shape):
    nd = len(shape)
    return pl.BlockSpec(shape, lambda *_: (0,) * nd, pipeline_mode=pl.Buffered(1))


def _in_proj_kernel(x_ref, g_ref, wn_ref, wt_ref, cosn_ref, sina_ref, sinb_ref, cost_ref, sint_ref,
                    knat_ref, hg_ref, ml_ref, qT_ref, qiT_ref, vT_ref, wgT_ref, *, idx_scale):
    tm = x_ref.shape[0]
    x = x_ref[...]
    ms = jnp.mean(x * x, axis=-1, keepdims=True)
    h = _mx(x * lax.rsqrt(ms + RMS_EPS) * g_ref[...])
    nat = jnp.dot(h, wn_ref[...], preferred_element_type=F32)
    tr = lax.dot_general(wt_ref[...], h, _NT, preferred_element_type=F32)

    cosn, sina, sinb = cosn_ref[...], sina_ref[...], sinb_ref[...]

    def rope_nat(z):
        return z * cosn + pltpu.roll(z, 96, 1) * sina + pltpu.roll(z, 32, 1) * sinb

    knat_ref[:, 0:128] = rope_nat(nat[:, 0:128]).astype(knat_ref.dtype)
    knat_ref[:, 128:256] = rope_nat(nat[:, 128:256]).astype(knat_ref.dtype)
    hg_ref[...] = nat[:, 256:1280]
    ml_ref[...] = nat[:, 1280:NAT_WIDTH]

    cost, sint = cost_ref[...], sint_ref[...]

    def rope_t(z, scale):
        outs = []
        for hd in range(8):
            x1 = z[hd * 64:hd * 64 + 32]
            x2 = z[hd * 64 + 32:hd * 64 + 64]
            outs.append((x1 * cost - x2 * sint) * scale)
            outs.append((x2 * cost + x1 * sint) * scale)
        return jnp.concatenate(outs, axis=0)

    qT = rope_t(tr[0:512], (HEAD_DIM ** -0.5) * LOG2E).astype(qT_ref.dtype)
    qiT = rope_t(tr[512:1024], 1.0).astype(qiT_ref.dtype)
    vT = tr[1024:1152].astype(vT_ref.dtype)
    wg = tr[1152:1168]
    wg = jnp.where(_iota(wg.shape, 0) < 8, wg * idx_scale, wg)
    for c in range(tm // DSA_TILE):
        sl = slice(c * DSA_TILE, (c + 1) * DSA_TILE)
        qT_ref[c] = qT[:, sl]
        qiT_ref[c] = qiT[:, sl]
        vT_ref[c] = vT[:, sl]
        wgT_ref[c] = wg[:, sl]


def _in_proj(x2, g, wn, wt, tabs, seq):
    n = x2.shape[0]
    tm = ROW_TILE
    nt_seq = seq // tm
    cosn, sina, sinb, cost, sint = tabs
    idx_scale = (IDX_HEADS ** -0.5) * (HEAD_DIM ** -0.5)
    row = lambda w: pl.BlockSpec((tm, w), lambda i: (i, 0))
    tabn = pl.BlockSpec((tm, LANE), lambda i: (i % nt_seq, 0))
    tabt = pl.BlockSpec((32, tm), lambda i: (0, i % nt_seq))
    t3 = lambda r: pl.BlockSpec((tm // DSA_TILE, r, DSA_TILE), lambda i: (i, 0, 0))
    nl = n // DSA_TILE
    return pl.pallas_call(
        functools.partial(_in_proj_kernel, idx_scale=idx_scale),
        grid=(n // tm,),
        in_specs=[row(D_MODEL), _const_spec((1, D_MODEL)), _const_spec((D_MODEL, NAT_WIDTH)),
                  _const_spec((TR_WIDTH, D_MODEL)), tabn, tabn, tabn, tabt, tabt],
        out_specs=[row(256), row(1024), row(1024), t3(512), t3(512), t3(128), t3(16)],
        out_shape=[jax.ShapeDtypeStruct((n, 256), MXU_DTYPE),
                   jax.ShapeDtypeStruct((n, 1024), F32),
                   jax.ShapeDtypeStruct((n, 1024), F32),
                   jax.ShapeDtypeStruct((nl, 512, DSA_TILE), MXU_DTYPE),
                   jax.ShapeDtypeStruct((nl, 512, DSA_TILE), MXU_DTYPE),
                   jax.ShapeDtypeStruct((nl, 128, DSA_TILE), MXU_DTYPE),
                   jax.ShapeDtypeStruct((nl, 16, DSA_TILE), F32)],
        compiler_params=_params("parallel"),
        name="in_proj",
    )(x2, g, wn, wt, cosn, sina, sinb, cost, sint)


def _dsa_kernel(qiT_ref, wgT_ref, knat_ref, qT_ref, vT_ref, o_ref,
                key_scr, hi_scr, bias_scr, acc_scr, thr_scr, nge_scr, kmax_scr, *, topk):
    tq = DSA_TILE
    j = pl.program_id(1)
    nk = j + 1
    int_min = jnp.int32(-2 ** 31)
    srow = _iota((tq, tq), 0)
    tcol = _iota((tq, tq), 1)
    causal = srow <= tcol

    w = wgT_ref[0]
    qi = qiT_ref[0]
    qi_wide = jnp.concatenate([qi[hd * 64:(hd + 1) * 64] for hd in range(IDX_HEADS)], axis=1)

    def score_tile(kt, carry):
        kblk = knat_ref[kt][:, 128:192]
        acc = jnp.zeros((tq, tq), F32)
        ys = jnp.dot(kblk, qi_wide, preferred_element_type=F32)
        for hd in range(IDX_HEADS):
            acc = acc + w[hd:hd + 1, :] * jnp.maximum(ys[:, hd * tq:(hd + 1) * tq], 0.0)
        acc = jnp.where(jnp.abs(acc) < F32_MIN_NORMAL, 0.0, acc)
        acc = jnp.where(jnp.logical_or(kt < j, causal), acc, -jnp.inf)
        bits = pltpu.bitcast(acc, I32)
        key_scr[kt] = bits ^ ((bits >> 31) & jnp.int32(0x7FFFFFFF))
        hi_scr[kt] = pltpu.bitcast(bits & jnp.int32(-65536), F32).astype(jnp.bfloat16)
        return carry

    lax.fori_loop(0, nk, score_tile, 0)

    key16_neg_inf = (0xFF80 ^ 0x7FFF) - 65536
    one_h = jnp.ones((16, tq), jnp.bfloat16)
    zero_h = jnp.zeros((16, tq), jnp.bfloat16)

    def select_threshold(nks):
        def count16(cand16):
            b16 = cand16 ^ ((cand16 >> 15) & 0x7FFF)
            b16 = jnp.where((cand16 > 0) & (cand16 < 0x80), 0x80, b16)
            cb = jnp.broadcast_to(pltpu.bitcast(b16 << 16, F32).astype(jnp.bfloat16), (16, tq))
            accs = [zero_h] * 4
            n = 0
            for kt in range(nks):
                for r in range(tq // 16):
                    ind = jnp.where(hi_scr[kt, r * 16:(r + 1) * 16, :] >= cb, one_h, zero_h)
                    accs[n % 4] = accs[n % 4] + ind
                    n += 1
            tot = (accs[0] + accs[1]).astype(F32) + (accs[2] + accs[3]).astype(F32)
            return jnp.sum(tot, axis=0, keepdims=True)

        def count32(pred_fn):
            accs = [jnp.zeros((8, tq), I32)] * 4
            n = 0
            for kt in range(nks):
                for r in range(tq // 8):
                    ind = jnp.where(pred_fn(key_scr[kt, r * 8:(r + 1) * 8, :]), 1, 0).astype(I32)
                    accs[n % 4] = accs[n % 4] + ind
                    n += 1
            return jnp.sum((accs[0] + accs[1]) + (accs[2] + accs[3]), axis=0, keepdims=True)

        def stage16(i, lo):
            cand = lo + (jnp.int32(1) << (15 - i))
            ok = (count16(cand) >= topk) | (cand <= key16_neg_inf)
            return jnp.where(ok, cand, lo)

        lo16 = lax.fori_loop(0, 16, stage16, jnp.full((1, tq), -32768, I32))

        def stage32(i, carry):
            lo, _ = carry
            cand = lo + (jnp.int32(1) << (15 - i))
            cnt = count32(lambda k: k >= cand)
            ok = cnt >= topk
            return jnp.where(ok, cand, lo), jnp.where(ok, cnt, carry[1])

        thr, n_ge = lax.fori_loop(0, 16, stage32, (lo16 << 16, jnp.full((1, tq), nks * tq, I32)))
        thr_scr[...] = jnp.broadcast_to(thr, thr_scr.shape)
        nge_scr[...] = jnp.broadcast_to(n_ge, nge_scr.shape)

    for jj in range(key_scr.shape[0]):
        @pl.when(j == jj)
        def _(jj=jj):
            if (jj + 1) * tq <= topk:
                thr_scr[...] = jnp.full(thr_scr.shape, int_min, I32)
                nge_scr[...] = jnp.full(nge_scr.shape, topk, I32)
            else:
                select_threshold(jj + 1)

    thr = thr_scr[0:1, :]
    n_ge = nge_scr[0:1, :]

    q = qT_ref[0]

    @pl.when(j == 0)
    def _():
        group_ones = _mx(_block_ones(LANE, HEAD_DIM))

        def body(kt, m):
            k = knat_ref[kt][:, 0:128].astype(F32)
            n2 = jnp.dot(_mx(k * k), group_ones, preferred_element_type=F32)
            return jnp.maximum(m, jnp.max(n2, axis=0, keepdims=True))

        kmax2 = lax.fori_loop(0, key_scr.shape[0], body, jnp.zeros((1, LANE), F32))
        kmax_scr[...] = jnp.broadcast_to(kmax2, kmax_scr.shape)

    qf = q.astype(F32)
    bound2 = jnp.zeros((1, tq), F32)
    for hd in range(ATT_HEADS):
        g = hd // ATT_GROUP
        qn2 = jnp.sum(qf[hd * 64:(hd + 1) * 64] * qf[hd * 64:(hd + 1) * 64], axis=0, keepdims=True)
        bound2 = jnp.maximum(bound2, qn2 * kmax_scr[0:1, g * HEAD_DIM:g * HEAD_DIM + 1])
    bound = BOUND_MARGIN * jnp.sqrt(bound2)
    bound_is_small = jnp.max(bound) < SAFE_LOGIT_BOUND
    neg_bound = -bound

    def mask_plain():
        def body(kt, carry):
            bias_scr[kt] = jnp.where(key_scr[kt] >= thr, neg_bound, NEG_BIG)
            return carry
        lax.fori_loop(0, nk, body, 0)

    def mask_ties():
        def count_gt(kt, c):
            ind = jnp.where(key_scr[kt] > thr, 1.0, 0.0)
            return c + jnp.sum(ind, axis=0, keepdims=True)
        n_gt = lax.fori_loop(0, nk, count_gt, jnp.zeros((1, tq), F32))
        need = topk - n_gt
        tri = _mx(jnp.where(srow >= tcol, 1.0, 0.0))

        def body(kt, off):
            k = key_scr[kt]
            eq = k == thr
            eqf = jnp.where(eq, 1.0, 0.0)
            rank = jnp.dot(tri, _mx(eqf), preferred_element_type=F32) + off
            tie_bias = jnp.where(rank <= need, neg_bound, NEG_BIG)
            bias_scr[kt] = jnp.where(k > thr, neg_bound, jnp.where(eq, tie_bias, NEG_BIG))
            return off + jnp.sum(eqf, axis=0, keepdims=True)
        lax.fori_loop(0, nk, body, jnp.zeros((1, tq), F32))

    no_excess_ties = jnp.max(jnp.abs(n_ge - topk)) == 0
    lax.cond(no_excess_ties, mask_plain, mask_ties)
    bias_scr[j] = jnp.where(causal, bias_scr[j], NEG_BIG)

    acc_scr[...] = jnp.zeros(acc_scr.shape, F32)
    zeros64 = jnp.zeros((64, tq), q.dtype)
    qwide = []
    for g in range(ATT_KV_HEADS):
        cols = []
        for hh in range(ATT_GROUP):
            hd = g * ATT_GROUP + hh
            qh = q[hd * 64:(hd + 1) * 64]
            cols.append(jnp.concatenate([qh, zeros64] if g == 0 else [zeros64, qh], axis=0))
        qwide.append(jnp.concatenate(cols, axis=1))

    ones_rows = jnp.ones((16, tq), q.dtype)

    def attend(kt, carry):
        ms, ls = carry
        kb = knat_ref[kt][:, 0:128]
        vtb = vT_ref[kt]
        bias = bias_scr[kt]
        lts = [jnp.dot(kb, qwide[g], preferred_element_type=F32) for g in range(ATT_KV_HEADS)]
        new_m, new_l = [], []
        for g in range(ATT_KV_HEADS):
            lt = jnp.concatenate([lts[g][:, hh * tq:(hh + 1) * tq] + bias for hh in range(ATT_GROUP)], axis=1)
            m_new = jnp.maximum(ms[g], jnp.max(lt, axis=0, keepdims=True))
            alpha = jnp.exp2(ms[g] - m_new)
            p = _mx(jnp.exp2(lt - m_new))
            v_aug = jnp.concatenate([vtb[g * 64:(g + 1) * 64], ones_rows], axis=0)
            pv = jnp.dot(v_aug, p, preferred_element_type=F32)
            new_l.append(alpha * ls[g] + pv[HEAD_DIM:HEAD_DIM + 1, :])
            acc_scr[g, 0:HEAD_DIM, :] = alpha * acc_scr[g, 0:HEAD_DIM, :] + pv[0:HEAD_DIM, :]
            new_m.append(m_new)
        return tuple(new_m), tuple(new_l)

    wide = ATT_GROUP * tq

    def run_online():
        init = (tuple(jnp.full((1, wide), NEG_BIG, F32) for _ in range(ATT_KV_HEADS)),
                tuple(jnp.zeros((1, wide), F32) for _ in range(ATT_KV_HEADS)))
        _, ls = lax.fori_loop(0, nk, attend, init)
        for g in range(ATT_KV_HEADS):
            acc_scr[g, HEAD_DIM:, :] = jnp.broadcast_to(ls[g], (16, wide))

    def attend_bounded(kt, carry):
        kb = knat_ref[kt][:, 0:128]
        vtb = vT_ref[kt]
        bias = bias_scr[kt]
        for g in range(ATT_KV_HEADS):
            lt = jnp.dot(kb, qwide[g], preferred_element_type=F32)
            p = _mx(jnp.exp2(jnp.concatenate([lt[:, hh * tq:(hh + 1) * tq] + bias
                                              for hh in range(ATT_GROUP)], axis=1)))
            v_aug = jnp.concatenate([vtb[g * 64:(g + 1) * 64], ones_rows], axis=0)
            acc_scr[g] = acc_scr[g] + jnp.dot(v_aug, p, preferred_element_type=F32)
        return carry

    def run_bounded():
        def pair(i, carry):
            attend_bounded(2 * i, carry)
            return attend_bounded(2 * i + 1, carry)

        lax.fori_loop(0, nk // 2, pair, 0)

        @pl.when(nk % 2 == 1)
        def _():
            attend_bounded(nk - 1, 0)

    lax.cond(bound_is_small, run_bounded, run_online)
    outs = []
    for g in range(ATT_KV_HEADS):
        og = acc_scr[g, 0:HEAD_DIM, :] / acc_scr[g, HEAD_DIM:HEAD_DIM + 1, :]
        outs += [og[:, hh * tq:(hh + 1) * tq] for hh in range(ATT_GROUP)]
    o_ref[...] = jnp.concatenate(outs, axis=0).T.astype(o_ref.dtype)


def _dsa(qiT3, wgT3, knat3, qT3, vT3, batch, seq):
    tq = DSA_TILE
    nq = seq // tq
    n = batch * seq
    topk = min(INDEX_TOPK_MAX, seq // 4)
    qtile = lambda r: pl.BlockSpec((1, r, tq), lambda b, j: (b * nq + j, 0, 0))
    return pl.pallas_call(
        functools.partial(_dsa_kernel, topk=topk),
        grid=(batch, nq),
        in_specs=[qtile(512), qtile(16),
                  pl.BlockSpec((nq, tq, 256), lambda b, j: (b, 0, 0)),
                  qtile(512),
                  pl.BlockSpec((nq, 128, tq), lambda b, j: (b, 0, 0))],
        out_specs=pl.BlockSpec((tq, 512), lambda b, j: (b * nq + j, 0)),
        out_shape=jax.ShapeDtypeStruct((n, 512), MXU_DTYPE),
        scratch_shapes=[pltpu.VMEM((nq, tq, tq), I32), pltpu.VMEM((nq, tq, tq), jnp.bfloat16),
                        pltpu.VMEM((nq, tq, tq), F32),
                        pltpu.VMEM((ATT_KV_HEADS, HEAD_DIM + 16, ATT_GROUP * tq), F32),
                        pltpu.VMEM((8, tq), I32), pltpu.VMEM((8, tq), I32), pltpu.VMEM((8, LANE), F32)],
        compiler_params=_params("parallel", "arbitrary"),
        name="dsa",
    )(qiT3, wgT3, knat3, qT3, vT3)


def _hgrn_kernel(hg_ref, lb_ref, gn_ref, o_ref, *scratch):
    st_scr, z_scr = scratch[0], scratch[7]

    @pl.when(pl.program_id(1) == 0)
    def _():
        st_scr[...] = jnp.zeros(st_scr.shape, F32)
        z_scr[...] = jnp.zeros(z_scr.shape, z_scr.dtype)

    seqs = [_hgrn_sequence(hg_ref.at[bi], lb_ref, gn_ref, o_ref.at[bi], *(s.at[bi] for s in scratch))
            for bi in range(HG_SEQS)]
    while seqs:
        for seq in list(seqs):
            if next(seq, "done") == "done":
                seqs.remove(seq)


def _hgrn_sequence(hg_ref, lb_ref, gn_ref, o_ref, st_scr, q_scr, k_scr, g_scr, qg_scr, kd_scr, o_scr, z_scr,
                   snap_scr):
    rows = hg_ref.shape[0]
    c = HG_SUB
    w = HG_WIDTH

    lb = lb_ref[...]
    q = _silu(hg_ref[:, 0:w]) * (HEAD_DIM ** -0.5)
    forget = lb + (1.0 - lb) * _sigmoid(hg_ref[:, w:2 * w])
    logf = jnp.log2(forget)
    kk = 1.0 - forget
    v = hg_ref[:, 2 * w:3 * w]

    ri = _iota((rows, rows), 0)
    ci = _iota((rows, rows), 1)
    same = (ri // c) == (ci // c)
    tri = _mx(jnp.where(same & (ci <= ri), 1.0, 0.0))
    ones_blk = _mx(jnp.where(same, 1.0, 0.0))
    gl = _dot01_left(tri, logf)
    glast = _dot01_left(ones_blk, logf)
    q_scr[...] = q
    k_scr[...] = gl - jnp.log2(kk)
    g_scr[...] = gl
    qg_scr[...] = q * jnp.exp2(gl)
    kd_scr[...] = kk * jnp.exp2(glast - gl)
    dec = jnp.exp2(glast)
    vT = _mx(v.T)
    head_ones = _mx(_block_ones(w, HEAD_DIM))
    pair_mask = _block_ones(LANE, HEAD_DIM)
    half = c // 2
    rhalf = _iota((half, w), 0)
    nsc = rows // c
    yield

    for sc in range(nsc):
        z_scr[sc, sc * c:(sc + 1) * c, :] = kd_scr[sc * c:(sc + 1) * c, :].astype(z_scr.dtype)
    for p in range(HG_HEADS // 2):
        ps = slice(p * LANE, (p + 1) * LANE)
        st = st_scr[p]
        for sc in range(nsc):
            snap_scr[sc, p] = st.astype(snap_scr.dtype)
            upd = jnp.dot(vT[ps, :], z_scr[sc, :, ps], preferred_element_type=F32)
            st = st * dec[sc * c:sc * c + 1, ps] + upd * pair_mask
        st_scr[p] = st
        yield

    for sc in range(nsc):
        r0 = sc * c
        q_top, q_bot = q_scr[r0:r0 + half, :], q_scr[r0 + half:r0 + c, :]
        g_top, g_bot = g_scr[r0:r0 + half, :], g_scr[r0 + half:r0 + c, :]
        bcast = lambda row: jnp.broadcast_to(row, (half, w))
        xs_top, xs_bot = [], []
        for s in range(c):
            hsr = bcast(k_scr[r0 + s:r0 + s + 1, :])
            if s < half:
                xs_top.append(q_top * jnp.exp2(jnp.where(rhalf >= s, g_top - hsr, -jnp.inf)))
                xs_bot.append(q_bot * jnp.exp2(g_bot - hsr))
            else:
                xs_bot.append(q_bot * jnp.exp2(jnp.where(rhalf >= s - half, g_bot - hsr, -jnp.inf)))
        y = jnp.dot(_mx(jnp.concatenate(xs_top + xs_bot, axis=0)), head_ones, preferred_element_type=F32)
        o_top = jnp.zeros((half, w), F32)
        o_bot = jnp.zeros((half, w), F32)
        for s in range(c):
            vsr = bcast(hg_ref[r0 + s:r0 + s + 1, 2 * w:3 * w])
            if s < half:
                o_top = o_top + y[s * half:(s + 1) * half, :] * vsr
            o_bot = o_bot + y[(half + s) * half:(half + s + 1) * half, :] * vsr
        o_diag = jnp.concatenate([o_top, o_bot], axis=0)
        o_pairs = [_dot_nt(qg_scr[r0:r0 + c, p * LANE:(p + 1) * LANE], snap_scr[sc, p])
                   for p in range(HG_HEADS // 2)]
        o_scr[r0:r0 + c, :] = jnp.concatenate(o_pairs, axis=1) + o_diag
        yield

    o = o_scr[...]
    ms = _dot01_right(o * o, head_ones) * (1.0 / HEAD_DIM)
    y = o * lax.rsqrt(ms + RMS_EPS) * gn_ref[...]
    o_ref[...] = (y * _silu(hg_ref[:, 3 * w:4 * w])).astype(o_ref.dtype)


def _hgrn(hg, lb, gn, batch, seq):
    rows = HG_ROWS
    ns = seq // rows
    n = batch * seq
    nb = HG_SEQS
    assert batch % nb == 0
    vm = lambda: pltpu.VMEM((nb, rows, HG_WIDTH), F32)
    out = pl.pallas_call(
        _hgrn_kernel,
        grid=(batch // nb, ns),
        in_specs=[pl.BlockSpec((nb, rows, 1024), lambda b, s: (b, s, 0)),
                  _const_spec((1, HG_WIDTH)), _const_spec((1, HG_WIDTH))],
        out_specs=pl.BlockSpec((nb, rows, HG_WIDTH), lambda b, s: (b, s, 0)),
        out_shape=jax.ShapeDtypeStruct((batch, seq, HG_WIDTH), MXU_DTYPE),
        scratch_shapes=[pltpu.VMEM((nb, HG_HEADS // 2, LANE, LANE), F32), vm(), vm(), vm(), vm(), vm(), vm(),
                        pltpu.VMEM((nb, rows // HG_SUB, rows, HG_WIDTH), MXU_DTYPE),
                        pltpu.VMEM((nb, rows // HG_SUB, HG_HEADS // 2, LANE, LANE), MXU_DTYPE)],
        compiler_params=_params("parallel", "arbitrary"),
        name="hgrn",
    )(hg.reshape(batch, seq, 1024), lb, gn)
    return out.reshape(n, HG_WIDTH)


def _mlstm_kernel(ml_ref, *refs):
    gT_refs = refs[:ML_SEQS]
    cw_ref, cb_ref, brow_ref, gn_ref, o_ref, prev_scr, c_scr, n_scr, m64_scr, m128_scr = refs[ML_SEQS:]

    @pl.when(pl.program_id(1) == 0)
    def _():
        prev_scr[...] = jnp.zeros(prev_scr.shape, F32)
        c_scr[...] = jnp.zeros(c_scr.shape, F32)
        n_scr[...] = jnp.zeros(n_scr.shape, F32)
        m64_scr[...] = jnp.full(m64_scr.shape, M_INIT, F32)
        m128_scr[...] = jnp.full(m128_scr.shape, M_INIT, F32)

    seqs = [_mlstm_sequence(ml_ref.at[bi], gT_ref, cw_ref, cb_ref, brow_ref, gn_ref, o_ref.at[bi],
                            prev_scr.at[bi], c_scr.at[bi], n_scr.at[bi], m64_scr.at[bi], m128_scr.at[bi])
            for bi, gT_ref in enumerate(gT_refs)]
    while seqs:
        for seq in list(seqs):
            if next(seq, "done") == "done":
                seqs.remove(seq)


def _mlstm_sequence(ml_ref, gT_ref, cw_ref, cb_ref, brow_ref, gn_ref, o_ref,
                    prev_scr, c_scr, n_scr, m64_scr, m128_scr):
    L = ML_CHUNK
    w = ML_WIDTH

    xqk = ml_ref[:, 0:512]
    prev = prev_scr[...]
    rows = _iota((L, 512), 0)
    acc = cb_ref[...] + cw_ref[ML_CONV - 1:ML_CONV, :] * xqk
    for sh in range(1, ML_CONV):
        shifted = jnp.where(rows >= sh, pltpu.roll(xqk, sh, 0), pltpu.roll(prev, sh, 0))
        acc = acc + cw_ref[ML_CONV - 1 - sh:ML_CONV - sh, :] * shifted
    prev_scr[...] = xqk
    qk = _silu(acc)
    q = qk[:, 0:w]
    k = qk[:, w:2 * w] * (HEAD_DIM ** -0.5)
    v = ml_ref[:, 512:768]

    srow = _iota((L, L), 0)
    scol = _iota((L, L), 1)
    tril = srow >= scol
    triu = _mx(jnp.where(srow <= scol, 1.0, 0.0))
    grow = gT_ref[0] + brow_ref[...]
    brow = _dot01_right(_log_sigmoid(grow), triu)
    yield

    def expand(x, width, row0):
        src = _iota((16, ML_HEADS * width), 0)
        dst = _iota((16, ML_HEADS * width), 1)
        e01 = _mx(jnp.where(src == row0 + dst // width, 1.0, 0.0))
        f = lambda p: lax.dot_general(p, e01, (((0,), (0,)), ((), ())), preferred_element_type=F32)
        hi, mid, lo = _split3(x)
        return f(hi) + f(mid) + f(lo)

    b128 = expand(brow, L, 12)
    i128 = expand(grow, L, 8)
    b64 = expand(brow, 64, 12)
    i64 = expand(grow, 64, 8)
    yield

    lane = _iota((L, LANE), 1)
    first = lane < HEAD_DIM
    pair_ones = _mx(_block_ones(LANE, HEAD_DIM))
    pair_mask = _block_ones(LANE, HEAD_DIM)
    m_prev128 = m128_scr[0:1, :]
    m_prev64 = m64_scr[0:1, :]
    n_row = n_scr[0:1, :]

    houts = []
    for p in range(ML_HEADS // 2):
        ps = slice(p * LANE, (p + 1) * LANE)
        q_pair, k_pair, v_pair = q[:, ps], k[:, ps], v[:, ps]
        per_head = []
        for hh in range(2):
            hd = 2 * p + hh
            hs = slice(hd * L, (hd + 1) * L)
            bt = b128[:, hs]
            d = bt - brow[12 + hd:13 + hd, :] + grow[8 + hd:9 + hd, :]
            d = jnp.where(tril, d, -jnp.inf)
            inter = bt + m_prev128[:, hs]
            m_t = jnp.maximum(inter, jnp.max(d, axis=1, keepdims=True))
            wgt = jnp.exp(d - m_t)
            a = jnp.exp(inter - m_t)
            qm = jnp.where(first if hh == 0 else ~first, q_pair, 0.0)
            s = _dot_nt(qm, k_pair) * wgt
            sv = _dot(s, v_pair)
            rs = jnp.sum(s, axis=1, keepdims=True)
            per_head.append((m_t[:, 0:LANE], a[:, 0:LANE], sv, rs))
            yield
        sel = lambda i: jnp.where(first, per_head[0][i], per_head[1][i])
        m_t, a, sv, rs = sel(0), sel(1), sel(2), sel(3)
        qc = _dot(q_pair, c_scr[p])
        qn = _dot(q_pair * n_row[:, ps], pair_ones)
        num = a * qc + sv
        den = a * qn + rs
        houts.append(num / jnp.maximum(jnp.abs(den), jnp.exp(-m_t)))
        yield
    hout = jnp.concatenate(houts, axis=1)

    def new_m(bx, ix, m_prev):
        blast = bx[L - 1:L, :]
        wl = blast - bx + ix
        m_new = jnp.maximum(blast + m_prev, jnp.max(wl, axis=0, keepdims=True))
        return blast, wl, m_new

    blast, wl, m_new64 = new_m(b64, i64, m_prev64)
    dec = jnp.exp(blast + m_prev64 - m_new64)
    kw = k * jnp.exp(wl - m_new64)
    for p in range(ML_HEADS // 2):
        ps = slice(p * LANE, (p + 1) * LANE)
        upd = jnp.dot(_mx(kw[:, ps].T), _mx(v[:, ps]), preferred_element_type=F32)
        c_scr[p] = c_scr[p] * dec[:, ps] + upd * pair_mask
    n_scr[...] = jnp.broadcast_to(dec * n_row + jnp.sum(kw, axis=0, keepdims=True), n_scr.shape)
    m64_scr[...] = jnp.broadcast_to(m_new64, m64_scr.shape)
    _, _, m_new128 = new_m(b128, i128, m_prev128)
    m128_scr[...] = jnp.broadcast_to(m_new128, m128_scr.shape)

    head_ones = _mx(_block_ones(w, HEAD_DIM))
    ms = _dot01_right(hout * hout, head_ones) * (1.0 / HEAD_DIM)
    y = hout * lax.rsqrt(ms + RMS_EPS) * gn_ref[...]
    o_ref[...] = (y * _sigmoid(ml_ref[:, 768:1024])).astype(o_ref.dtype)


def _mlstm(ml, wgT3, cw, cb, brow, gn, batch, seq):
    L = ML_CHUNK
    nc = seq // L
    n = batch * seq
    per_tile = DSA_TILE // L
    nb = ML_SEQS
    assert batch % nb == 0

    def gate_spec(off):
        def index(b, s):
            chunk = (nb * b + off) * nc + s
            return chunk // per_tile, 0, chunk % per_tile
        return pl.BlockSpec((1, 16, L), index)

    out = pl.pallas_call(
        _mlstm_kernel,
        grid=(batch // nb, nc),
        in_specs=[pl.BlockSpec((nb, L, 1024), lambda b, s: (b, s, 0))] + [gate_spec(o) for o in range(nb)]
                 + [_const_spec((ML_CONV, 512)), _const_spec((1, 512)),
                    _const_spec((16, L)), _const_spec((1, ML_WIDTH))],
        out_specs=pl.BlockSpec((nb, L, ML_WIDTH), lambda b, s: (b, s, 0)),
        out_shape=jax.ShapeDtypeStruct((batch, seq, ML_WIDTH), MXU_DTYPE),
        scratch_shapes=[pltpu.VMEM((nb, L, 512), F32), pltpu.VMEM((nb, 2, LANE, LANE), F32),
                        pltpu.VMEM((nb, 8, ML_WIDTH), F32), pltpu.VMEM((nb, 8, ML_WIDTH), F32),
                        pltpu.VMEM((nb, 8, ML_HEADS * L), F32)],
        compiler_params=_params("parallel", "arbitrary"),
        name="mlstm",
    )(ml.reshape(batch, seq, 1024), *([wgT3] * nb), cw, cb, brow, gn)
    return out.reshape(n, ML_WIDTH)


def _rms(y, g):
    ms = jnp.mean(y * y, axis=-1, keepdims=True)
    return y * lax.rsqrt(ms + RMS_EPS) * g


def _mem_kv_kernel(m_ref, g_ref, w_ref, k_ref, v_ref):
    h = _mx(_rms(m_ref[...], g_ref[...]))
    kv = jnp.dot(h, w_ref[...], preferred_element_type=F32)
    k_ref[...] = kv[:, 0:D_MODEL].astype(k_ref.dtype)
    v_ref[...] = kv[:, D_MODEL:2 * D_MODEL].astype(v_ref.dtype)


def _mem_kv(mem2, g, w):
    n = mem2.shape[0]
    tm = ROW_TILE
    row = pl.BlockSpec((tm, D_MODEL), lambda i: (i, 0))
    return pl.pallas_call(
        _mem_kv_kernel,
        grid=(n // tm,),
        in_specs=[row, _const_spec((1, D_MODEL)), _const_spec((D_MODEL, 2 * D_MODEL))],
        out_specs=[row, row],
        out_shape=[jax.ShapeDtypeStruct((n, D_MODEL), MXU_DTYPE)] * 2,
        compiler_params=_params("parallel"),
        name="mem_kv",
    )(mem2, g, w)


def _post_mix_kernel(a_ref, b_ref, c_ref, x_ref, k_ref, v_ref, wout_ref, wq_ref, wo_ref, wu_ref, wd_ref,
                     g_ref, o_ref):
    g = g_ref[...]
    y = (jnp.dot(a_ref[...], wout_ref[0:512, :], preferred_element_type=F32)
         + jnp.dot(b_ref[...], wout_ref[512:768, :], preferred_element_type=F32)
         + jnp.dot(c_ref[...], wout_ref[768:1024, :], preferred_element_type=F32))
    x = x_ref[...] + _rms(y, g[0:1])
    h = _mx(_rms(x, g[1:2]))
    q = jnp.dot(h, wq_ref[...], preferred_element_type=F32) * (CROSS_HEAD_DIM ** -0.5)
    outs = []
    for hd in range(CROSS_HEADS):
        hs = slice(hd * CROSS_HEAD_DIM, (hd + 1) * CROSS_HEAD_DIM)
        logits = _dot_nt(q[:, hs], k_ref[:, hs])
        p = jnp.exp(logits - jnp.max(logits, axis=-1, keepdims=True))
        o = jnp.dot(_mx(p), v_ref[:, hs], preferred_element_type=F32)
        outs.append(o / jnp.sum(p, axis=-1, keepdims=True))
    y = jnp.dot(_mx(jnp.concatenate(outs, axis=1)), wo_ref[...], preferred_element_type=F32)
    x = x + _rms(y, g[2:3])
    h = _mx(_rms(x, g[3:4]))
    y = jnp.zeros(x.shape, F32)
    for c in range(MLP_HIDDEN // D_MODEL):
        cs = slice(c * D_MODEL, (c + 1) * D_MODEL)
        u = jnp.maximum(jnp.dot(h, wu_ref[:, cs], preferred_element_type=F32), 0.0)
        y = y + jnp.dot(_mx(u * u), wd_ref[cs, :], preferred_element_type=F32)
    o_ref[...] = x + _rms(y, g[4:5])


def _post_mix(a, b, c, x2, kmem, vmem, wout, wq, wo, wu, wd, gains, seq, n_mem):
    n = x2.shape[0]
    tm = POST_TILE
    per_b = seq // tm
    row = lambda width: pl.BlockSpec((tm, width), lambda i: (i, 0))
    memb = pl.BlockSpec((n_mem, D_MODEL), lambda i: (i // per_b, 0))
    sq = _const_spec((D_MODEL, D_MODEL))
    return pl.pallas_call(
        _post_mix_kernel,
        grid=(n // tm,),
        in_specs=[row(512), row(256), row(256), row(D_MODEL), memb, memb, sq, sq, sq,
                  _const_spec((D_MODEL, MLP_HIDDEN)), _const_spec((MLP_HIDDEN, D_MODEL)),
                  _const_spec((8, D_MODEL))],
        out_specs=row(D_MODEL),
        out_shape=jax.ShapeDtypeStruct((n, D_MODEL), F32),
        compiler_params=_params("parallel"),
        name="post_mix",
    )(a, b, c, x2, kmem, vmem, wout, wq, wo, wu, wd, gains)


def _rope_tables(seq):
    half = HEAD_DIM // 2
    inv = ROPE_THETA ** (-jnp.arange(0, HEAD_DIM, 2, dtype=F32) / HEAD_DIM)
    ang = jnp.arange(seq).astype(F32)[:, None] * inv[None, :]
    cos, sin = jnp.cos(ang), jnp.sin(ang)
    lane = jnp.arange(LANE)
    freq = (lane % HEAD_DIM) % half
    first = (lane % HEAD_DIM) < half
    cosn = cos[:, freq]
    sinn = sin[:, freq]
    sina = jnp.where(first[None, :], -sinn, 0.0)
    sinb = jnp.where(first[None, :], 0.0, sinn)
    return cosn, sina, sinb, cos.T, sin.T


def _layout_w_in(w_in):
    depth = w_in.shape[0]
    z = lambda wd: jnp.zeros((depth, D_MODEL, wd), w_in.dtype)
    col = lambda a, b: w_in[:, :, a:b]
    wn = jnp.concatenate([col(_O_AK, _O_AV), col(_O_XK, _O_XW), z(64),
                          col(_O_HQ, _O_MQK), col(_O_MQK, _O_MG)], axis=-1)
    wt = jnp.concatenate([col(_O_AQ, _O_AK), col(_O_XQ, _O_XK), col(_O_AV, _O_XQ),
                          col(_O_XW, _O_HQ), col(_O_MG, _IN_WIDTH)], axis=-1)
    return _mx(wn), _mx(jnp.swapaxes(wt, 1, 2))


def kernel(x, mem, mix_pre_g, w_in, ml_conv_w, ml_conv_b, ml_gate_b, hg_lb, hg_norm_g, ml_norm_g,
           w_out, mix_post_g, cross_pre_g, mem_norm_g, w_cq, w_ckv, w_co, cross_post_g, mlp_pre_g,
           w_up, w_down, mlp_post_g):
    batch, seq, _ = x.shape
    n_mem = mem.shape[1]
    depth = w_in.shape[0]
    n = batch * seq
    assert seq % max(ROW_TILE, POST_TILE, HG_ROWS, ML_CHUNK, DSA_TILE) == 0
    assert (batch * n_mem) % ROW_TILE == 0
    assert ROW_TILE % DSA_TILE == 0 and DSA_TILE % ML_CHUNK == 0

    lbs = jnp.cumsum(jax.nn.softmax(hg_lb.astype(F32), axis=0), axis=0)
    lbs = lbs - lbs[0:1]

    tabs = _rope_tables(seq)
    wn_all, wt_all = _layout_w_in(w_in)
    row = lambda a: a.astype(F32)[None, :]
    brow = jnp.pad(ml_gate_b.astype(F32), ((0, 0), (8, 0)))

    x2 = x.reshape(n, D_MODEL)
    mem2 = mem.reshape(batch * n_mem, D_MODEL)
    nl = n // DSA_TILE
    for l in range(depth):
        knat, hg, ml, qT3, qiT3, vT3, wgT3 = _in_proj(x2, row(mix_pre_g[l]), wn_all[l], wt_all[l], tabs, seq)
        a_out = _dsa(qiT3, wgT3, knat.reshape(nl, DSA_TILE, 256), qT3, vT3, batch, seq)
        b_out = _hgrn(hg, row(lbs[l]), row(jnp.tile(hg_norm_g[l], HG_HEADS)), batch, seq)
        c_out = _mlstm(ml, wgT3, ml_conv_w[l].astype(F32), row(ml_conv_b[l]),
                       jnp.broadcast_to(brow[l][:, None], (16, ML_CHUNK)),
                       row(jnp.tile(ml_norm_g[l], ML_HEADS)), batch, seq)
        kmem, vmem = _mem_kv(mem2, row(mem_norm_g[l]), _mx(w_ckv[l]))
        gains = jnp.stack([mix_post_g[l], cross_pre_g[l], cross_post_g[l], mlp_pre_g[l], mlp_post_g[l]]
                          + [jnp.zeros_like(mix_post_g[l])] * 3).astype(F32)
        x2 = _post_mix(a_out, b_out, c_out, x2, kmem, vmem, _mx(w_out[l]), _mx(w_cq[l]), _mx(w_co[l]),
                       _mx(w_up[l]), _mx(w_down[l]), gains, seq, n_mem)
    return x2.reshape(batch, seq, D_MODEL)
```

```python
import functools

import jax
import jax.numpy as jnp
from jax import lax
from jax.experimental import pallas as pl
from jax.experimental.pallas import tpu as pltpu

F32 = jnp.float32
I32 = jnp.int32
MXU_DTYPE = jnp.bfloat16

D_MODEL = 1024
RMS_EPS = 1e-6
ROPE_THETA = 10000.0
NEG_BIG = -1e30
M_INIT = -1e30
LOG2E = 1.4426950408889634
F32_MIN_NORMAL = 1.1754943508222875e-38
BOUND_MARGIN = 1.05
SAFE_LOGIT_BOUND = 60.0
HEAD_DIM = 64
ATT_HEADS = 8
ATT_KV_HEADS = 2
ATT_GROUP = ATT_HEADS // ATT_KV_HEADS
IDX_HEADS = 8
INDEX_TOPK_MAX = 256
HG_HEADS = 4
HG_WIDTH = HG_HEADS * HEAD_DIM
ML_HEADS = 4
ML_WIDTH = ML_HEADS * HEAD_DIM
ML_CONV = 4
CROSS_HEADS = 4
CROSS_HEAD_DIM = D_MODEL // CROSS_HEADS
MLP_HIDDEN = 4 * D_MODEL

_O_AQ, _O_AK, _O_AV, _O_XQ, _O_XK, _O_XW = 0, 512, 640, 768, 1280, 1344
_O_HQ, _O_MQK, _O_MV, _O_MO, _O_MG, _IN_WIDTH = 1352, 2376, 2888, 3144, 3400, 3408

NAT_WIDTH = 128 + 128 + 1024 + 1024
TR_WIDTH = 512 + 512 + 128 + 16

LANE = 128
DSA_TILE = 256
ROW_TILE = 512
POST_TILE = 512
HG_SUB = 16
HG_ROWS = 128
HG_SEQS = 8
ML_CHUNK = 256
ML_SEQS = 4
VMEM_LIMIT = 56 * 1024 * 1024

_NT = (((1,), (1,)), ((), ()))


def _mx(a):
    return a.astype(MXU_DTYPE)


def _dot(a, b):
    return jnp.dot(_mx(a), _mx(b), preferred_element_type=F32)


def _dot_nt(a, b):
    return lax.dot_general(_mx(a), _mx(b), _NT, preferred_element_type=F32)


def _split3(x):
    hi = _mx(x)
    r1 = x - hi.astype(F32)
    mid = _mx(r1)
    lo = _mx(r1 - mid.astype(F32))
    return hi, mid, lo


def _dot01_left(m01, x):
    hi, mid, lo = _split3(x)
    f = lambda p: jnp.dot(m01, p, preferred_element_type=F32)
    return f(hi) + f(mid) + f(lo)


def _dot01_right(x, m01):
    hi, mid, lo = _split3(x)
    f = lambda p: jnp.dot(p, m01, preferred_element_type=F32)
    return f(hi) + f(mid) + f(lo)


def _sigmoid(x):
    return 1.0 / (1.0 + jnp.exp(-x))


def _silu(x):
    return x * _sigmoid(x)


def _log_sigmoid(x):
    return jnp.minimum(x, 0.0) - jnp.log(1.0 + jnp.exp(-jnp.abs(x)))


def _iota(shape, dim):
    return lax.broadcasted_iota(I32, shape, dim)


def _block_ones(n, blk):
    same = (_iota((n, n), 0) // blk) == (_iota((n, n), 1) // blk)
    return jnp.where(same, 1.0, 0.0)


def _params(*sem):
    return pltpu.CompilerParams(dimension_semantics=sem, vmem_limit_bytes=VMEM_LIMIT)


def _const_spec(shape):
    nd = len(shape)
    return pl.BlockSpec(shape, lambda *_: (0,) * nd, pipeline_mode=pl.Buffered(1))


def _in_proj_kernel(x_ref, g_ref, wn_ref, wt_ref, cosn_ref, sina_ref, sinb_ref, cost_ref, sint_ref,
                    knat_ref, hg_ref, ml_ref, qT_ref, qiT_ref, vT_ref, wgT_ref, *, idx_scale):
    tm = x_ref.shape[0]
    x = x_ref[...]
    ms = jnp.mean(x * x, axis=-1, keepdims=True)
    h = _mx(x * lax.rsqrt(ms + RMS_EPS) * g_ref[...])
    nat = jnp.dot(h, wn_ref[...], preferred_element_type=F32)
    tr = lax.dot_general(wt_ref[...], h, _NT, preferred_element_type=F32)

    cosn, sina, sinb = cosn_ref[...], sina_ref[...], sinb_ref[...]

    def rope_nat(z):
        return z * cosn + pltpu.roll(z, 96, 1) * sina + pltpu.roll(z, 32, 1) * sinb

    knat_ref[:, 0:128] = rope_nat(nat[:, 0:128]).astype(knat_ref.dtype)
    knat_ref[:, 128:256] = rope_nat(nat[:, 128:256]).astype(knat_ref.dtype)
    hg_ref[...] = nat[:, 256:1280]
    ml_ref[...] = nat[:, 1280:NAT_WIDTH]

    cost, sint = cost_ref[...], sint_ref[...]

    def rope_t(z, scale):
        outs = []
        for hd in range(8):
            x1 = z[hd * 64:hd * 64 + 32]
            x2 = z[hd * 64 + 32:hd * 64 + 64]
            outs.append((x1 * cost - x2 * sint) * scale)
            outs.append((x2 * cost + x1 * sint) * scale)
        return jnp.concatenate(outs, axis=0)

    qT = rope_t(tr[0:512], (HEAD_DIM ** -0.5) * LOG2E).astype(qT_ref.dtype)
    qiT = rope_t(tr[512:1024], 1.0).astype(qiT_ref.dtype)
    vT = tr[1024:1152].astype(vT_ref.dtype)
    wg = tr[1152:1168]
    wg = jnp.where(_iota(wg.shape, 0) < 8, wg * idx_scale, wg)
    for c in range(tm // DSA_TILE):
        sl = slice(c * DSA_TILE, (c + 1) * DSA_TILE)
        qT_ref[c] = qT[:, sl]
        qiT_ref[c] = qiT[:, sl]
        vT_ref[c] = vT[:, sl]
        wgT_ref[c] = wg[:, sl]


def _in_proj(x2, g, wn, wt, tabs, seq):
    n = x2.shape[0]
    tm = ROW_TILE
    nt_seq = seq // tm
    cosn, sina, sinb, cost, sint = tabs
    idx_scale = (IDX_HEADS ** -0.5) * (HEAD_DIM ** -0.5)
    row = lambda w: pl.BlockSpec((tm, w), lambda i: (i, 0))
    tabn = pl.BlockSpec((tm, LANE), lambda i: (i % nt_seq, 0))
    tabt = pl.BlockSpec((32, tm), lambda i: (0, i % nt_seq))
    t3 = lambda r: pl.BlockSpec((tm // DSA_TILE, r, DSA_TILE), lambda i: (i, 0, 0))
    nl = n // DSA_TILE
    return pl.pallas_call(
        functools.partial(_in_proj_kernel, idx_scale=idx_scale),
        grid=(n // tm,),
        in_specs=[row(D_MODEL), _const_spec((1, D_MODEL)), _const_spec((D_MODEL, NAT_WIDTH)),
                  _const_spec((TR_WIDTH, D_MODEL)), tabn, tabn, tabn, tabt, tabt],
        out_specs=[row(256), row(1024), row(1024), t3(512), t3(512), t3(128), t3(16)],
        out_shape=[jax.ShapeDtypeStruct((n, 256), MXU_DTYPE),
                   jax.ShapeDtypeStruct((n, 1024), F32),
                   jax.ShapeDtypeStruct((n, 1024), F32),
                   jax.ShapeDtypeStruct((nl, 512, DSA_TILE), MXU_DTYPE),
                   jax.ShapeDtypeStruct((nl, 512, DSA_TILE), MXU_DTYPE),
                   jax.ShapeDtypeStruct((nl, 128, DSA_TILE), MXU_DTYPE),
                   jax.ShapeDtypeStruct((nl, 16, DSA_TILE), F32)],
        compiler_params=_params("parallel"),
        name="in_proj",
    )(x2, g, wn, wt, cosn, sina, sinb, cost, sint)


def _dsa_kernel(qiT_ref, wgT_ref, knat_ref, qT_ref, vT_ref, o_ref,
                key_scr, hi_scr, bias_scr, acc_scr, thr_scr, nge_scr, kmax_scr, *, topk):
    tq = DSA_TILE
    j = pl.program_id(1)
    nk = j + 1
    int_min = jnp.int32(-2 ** 31)
    srow = _iota((tq, tq), 0)
    tcol = _iota((tq, tq), 1)
    causal = srow <= tcol

    w = wgT_ref[0]
    qi = qiT_ref[0]
    qi_wide = jnp.concatenate([qi[hd * 64:(hd + 1) * 64] for hd in range(IDX_HEADS)], axis=1)

    def score_tile(kt, carry):
        kblk = knat_ref[kt][:, 128:192]
        acc = jnp.zeros((tq, tq), F32)
        ys = jnp.dot(kblk, qi_wide, preferred_element_type=F32)
        for hd in range(IDX_HEADS):
            acc = acc + w[hd:hd + 1, :] * jnp.maximum(ys[:, hd * tq:(hd + 1) * tq], 0.0)
        acc = jnp.where(jnp.abs(acc) < F32_MIN_NORMAL, 0.0, acc)
        acc = jnp.where(jnp.logical_or(kt < j, causal), acc, -jnp.inf)
        bits = pltpu.bitcast(acc, I32)
        key_scr[kt] = bits ^ ((bits >> 31) & jnp.int32(0x7FFFFFFF))
        hi_scr[kt] = pltpu.bitcast(bits & jnp.int32(-65536), F32).astype(jnp.bfloat16)
        return carry

    lax.fori_loop(0, nk, score_tile, 0)

    key16_neg_inf = (0xFF80 ^ 0x7FFF) - 65536
    one_h = jnp.ones((16, tq), jnp.bfloat16)
    zero_h = jnp.zeros((16, tq), jnp.bfloat16)

    def select_threshold(nks):
        def count16(cand16):
            b16 = cand16 ^ ((cand16 >> 15) & 0x7FFF)
            b16 = jnp.where((cand16 > 0) & (cand16 < 0x80), 0x80, b16)
            cb = jnp.broadcast_to(pltpu.bitcast(b16 << 16, F32).astype(jnp.bfloat16), (16, tq))
            accs = [zero_h] * 4
            n = 0
            for kt in range(nks):
                for r in range(tq // 16):
                    ind = jnp.where(hi_scr[kt, r * 16:(r + 1) * 16, :] >= cb, one_h, zero_h)
                    accs[n % 4] = accs[n % 4] + ind
                    n += 1
            tot = (accs[0] + accs[1]).astype(F32) + (accs[2] + accs[3]).astype(F32)
            return jnp.sum(tot, axis=0, keepdims=True)

        def count32(pred_fn):
            accs = [jnp.zeros((8, tq), I32)] * 4
            n = 0
            for kt in range(nks):
                for r in range(tq // 8):
                    ind = jnp.where(pred_fn(key_scr[kt, r * 8:(r + 1) * 8, :]), 1, 0).astype(I32)
                    accs[n % 4] = accs[n % 4] + ind
                    n += 1
            return jnp.sum((accs[0] + accs[1]) + (accs[2] + accs[3]), axis=0, keepdims=True)

        def stage16(i, lo):
            cand = lo + (jnp.int32(1) << (15 - i))
            ok = (count16(cand) >= topk) | (cand <= key16_neg_inf)
            return jnp.where(ok, cand, lo)

        lo16 = lax.fori_loop(0, 16, stage16, jnp.full((1, tq), -32768, I32))

        def stage32(i, carry):
            lo, _ = carry
            cand = lo + (jnp.int32(1) << (15 - i))
            cnt = count32(lambda k: k >= cand)
            ok = cnt >= topk
            return jnp.where(ok, cand, lo), jnp.where(ok, cnt, carry[1])

        thr, n_ge = lax.fori_loop(0, 16, stage32, (lo16 << 16, jnp.full((1, tq), nks * tq, I32)))
        thr_scr[...] = jnp.broadcast_to(thr, thr_scr.shape)
        nge_scr[...] = jnp.broadcast_to(n_ge, nge_scr.shape)

    for jj in range(key_scr.shape[0]):
        @pl.when(j == jj)
        def _(jj=jj):
            if (jj + 1) * tq <= topk:
                thr_scr[...] = jnp.full(thr_scr.shape, int_min, I32)
                nge_scr[...] = jnp.full(nge_scr.shape, topk, I32)
            else:
                select_threshold(jj + 1)

    thr = thr_scr[0:1, :]
    n_ge = nge_scr[0:1, :]

    q = qT_ref[0]

    @pl.when(j == 0)
    def _():
        group_ones = _mx(_block_ones(LANE, HEAD_DIM))

        def body(kt, m):
            k = knat_ref[kt][:, 0:128].astype(F32)
            n2 = jnp.dot(_mx(k * k), group_ones, preferred_element_type=F32)
            return jnp.maximum(m, jnp.max(n2, axis=0, keepdims=True))

        kmax2 = lax.fori_loop(0, key_scr.shape[0], body, jnp.zeros((1, LANE), F32))
        kmax_scr[...] = jnp.broadcast_to(kmax2, kmax_scr.shape)

    qf = q.astype(F32)
    bound2 = jnp.zeros((1, tq), F32)
    for hd in range(ATT_HEADS):
        g = hd // ATT_GROUP
        qn2 = jnp.sum(qf[hd * 64:(hd + 1) * 64] * qf[hd * 64:(hd + 1) * 64], axis=0, keepdims=True)
        bound2 = jnp.maximum(bound2, qn2 * kmax_scr[0:1, g * HEAD_DIM:g * HEAD_DIM + 1])
    bound = BOUND_MARGIN * jnp.sqrt(bound2)
    bound_is_small = jnp.max(bound) < SAFE_LOGIT_BOUND
    neg_bound = -bound

    def mask_plain():
        def body(kt, carry):
            bias_scr[kt] = jnp.where(key_scr[kt] >= thr, neg_bound, NEG_BIG)
            return carry
        lax.fori_loop(0, nk, body, 0)

    def mask_ties():
        def count_gt(kt, c):
            ind = jnp.where(key_scr[kt] > thr, 1.0, 0.0)
            return c + jnp.sum(ind, axis=0, keepdims=True)
        n_gt = lax.fori_loop(0, nk, count_gt, jnp.zeros((1, tq), F32))
        need = topk - n_gt
        tri = _mx(jnp.where(srow >= tcol, 1.0, 0.0))

        def body(kt, off):
            k = key_scr[kt]
            eq = k == thr
            eqf = jnp.where(eq, 1.0, 0.0)
            rank = jnp.dot(tri, _mx(eqf), preferred_element_type=F32) + off
            tie_bias = jnp.where(rank <= need, neg_bound, NEG_BIG)
            bias_scr[kt] = jnp.where(k > thr, neg_bound, jnp.where(eq, tie_bias, NEG_BIG))
            return off + jnp.sum(eqf, axis=0, keepdims=True)
        lax.fori_loop(0, nk, body, jnp.zeros((1, tq), F32))

    no_excess_ties = jnp.max(jnp.abs(n_ge - topk)) == 0
    lax.cond(no_excess_ties, mask_plain, mask_ties)
    bias_scr[j] = jnp.where(causal, bias_scr[j], NEG_BIG)

    acc_scr[...] = jnp.zeros(acc_scr.shape, F32)
    zeros64 = jnp.zeros((64, tq), q.dtype)
    qwide = []
    for g in range(ATT_KV_HEADS):
        cols = []
        for hh in range(ATT_GROUP):
            hd = g * ATT_GROUP + hh
            qh = q[hd * 64:(hd + 1) * 64]
            cols.append(jnp.concatenate([qh, zeros64] if g == 0 else [zeros64, qh], axis=0))
        qwide.append(jnp.concatenate(cols, axis=1))

    ones_rows = jnp.ones((16, tq), q.dtype)

    def attend(kt, carry):
        ms, ls = carry
        kb = knat_ref[kt][:, 0:128]
        vtb = vT_ref[kt]
        bias = bias_scr[kt]
        lts = [jnp.dot(kb, qwide[g], preferred_element_type=F32) for g in range(ATT_KV_HEADS)]
        new_m, new_l = [], []
        for g in range(ATT_KV_HEADS):
            lt = jnp.concatenate([lts[g][:, hh * tq:(hh + 1) * tq] + bias for hh in range(ATT_GROUP)], axis=1)
            m_new = jnp.maximum(ms[g], jnp.max(lt, axis=0, keepdims=True))
            alpha = jnp.exp2(ms[g] - m_new)
            p = _mx(jnp.exp2(lt - m_new))
            v_aug = jnp.concatenate([vtb[g * 64:(g + 1) * 64], ones_rows], axis=0)
            pv = jnp.dot(v_aug, p, preferred_element_type=F32)
            new_l.append(alpha * ls[g] + pv[HEAD_DIM:HEAD_DIM + 1, :])
            acc_scr[g, 0:HEAD_DIM, :] = alpha * acc_scr[g, 0:HEAD_DIM, :] + pv[0:HEAD_DIM, :]
            new_m.append(m_new)
        return tuple(new_m), tuple(new_l)

    wide = ATT_GROUP * tq

    def run_online():
        init = (tuple(jnp.full((1, wide), NEG_BIG, F32) for _ in range(ATT_KV_HEADS)),
                tuple(jnp.zeros((1, wide), F32) for _ in range(ATT_KV_HEADS)))
        _, ls = lax.fori_loop(0, nk, attend, init)
        for g in range(ATT_KV_HEADS):
            acc_scr[g, HEAD_DIM:, :] = jnp.broadcast_to(ls[g], (16, wide))

    def attend_bounded(kt, carry):
        kb = knat_ref[kt][:, 0:128]
        vtb = vT_ref[kt]
        bias = bias_scr[kt]
        for g in range(ATT_KV_HEADS):
            lt = jnp.dot(kb, qwide[g], preferred_element_type=F32)
            p = _mx(jnp.exp2(jnp.concatenate([lt[:, hh * tq:(hh + 1) * tq] + bias
                                              for hh in range(ATT_GROUP)], axis=1)))
            v_aug = jnp.concatenate([vtb[g * 64:(g + 1) * 64], ones_rows], axis=0)
            acc_scr[g] = acc_scr[g] + jnp.dot(v_aug, p, preferred_element_type=F32)
        return carry

    def run_bounded():
        def pair(i, carry):
            attend_bounded(2 * i, carry)
            return attend_bounded(2 * i + 1, carry)

        lax.fori_loop(0, nk // 2, pair, 0)

        @pl.when(nk % 2 == 1)
        def _():
            attend_bounded(nk - 1, 0)

    lax.cond(bound_is_small, run_bounded, run_online)
    outs = []
    for g in range(ATT_KV_HEADS):
        og = acc_scr[g, 0:HEAD_DIM, :] / acc_scr[g, HEAD_DIM:HEAD_DIM + 1, :]
        outs += [og[:, hh * tq:(hh + 1) * tq] for hh in range(ATT_GROUP)]
    o_ref[...] = jnp.concatenate(outs, axis=0).T.astype(o_ref.dtype)


def _dsa(qiT3, wgT3, knat3, qT3, vT3, batch, seq):
    tq = DSA_TILE
    nq = seq // tq
    n = batch * seq
    topk = min(INDEX_TOPK_MAX, seq // 4)
    qtile = lambda r: pl.BlockSpec((1, r, tq), lambda b, j: (b * nq + j, 0, 0))
    return pl.pallas_call(
        functools.partial(_dsa_kernel, topk=topk),
        grid=(batch, nq),
        in_specs=[qtile(512), qtile(16),
                  pl.BlockSpec((nq, tq, 256), lambda b, j: (b, 0, 0)),
                  qtile(512),
                  pl.BlockSpec((nq, 128, tq), lambda b, j: (b, 0, 0))],
        out_specs=pl.BlockSpec((tq, 512), lambda b, j: (b * nq + j, 0)),
        out_shape=jax.ShapeDtypeStruct((n, 512), MXU_DTYPE),
        scratch_shapes=[pltpu.VMEM((nq, tq, tq), I32), pltpu.VMEM((nq, tq, tq), jnp.bfloat16),
                        pltpu.VMEM((nq, tq, tq), F32),
                        pltpu.VMEM((ATT_KV_HEADS, HEAD_DIM + 16, ATT_GROUP * tq), F32),
                        pltpu.VMEM((8, tq), I32), pltpu.VMEM((8, tq), I32), pltpu.VMEM((8, LANE), F32)],
        compiler_params=_params("parallel", "arbitrary"),
        name="dsa",
    )(qiT3, wgT3, knat3, qT3, vT3)


def _hgrn_kernel(hg_ref, lb_ref, gn_ref, o_ref, *scratch):
    st_scr, z_scr = scratch[0], scratch[7]

    @pl.when(pl.program_id(1) == 0)
    def _():
        st_scr[...] = jnp.zeros(st_scr.shape, F32)
        z_scr[...] = jnp.zeros(z_scr.shape, z_scr.dtype)

    seqs = [_hgrn_sequence(hg_ref.at[bi], lb_ref, gn_ref, o_ref.at[bi], *(s.at[bi] for s in scratch))
            for bi in range(HG_SEQS)]
    while seqs:
        for seq in list(seqs):
            if next(seq, "done") == "done":
                seqs.remove(seq)


def _hgrn_sequence(hg_ref, lb_ref, gn_ref, o_ref, st_scr, q_scr, k_scr, g_scr, qg_scr, kd_scr, o_scr, z_scr,
                   snap_scr):
    rows = hg_ref.shape[0]
    c = HG_SUB
    w = HG_WIDTH

    lb = lb_ref[...]
    q = _silu(hg_ref[:, 0:w]) * (HEAD_DIM ** -0.5)
    forget = lb + (1.0 - lb) * _sigmoid(hg_ref[:, w:2 * w])
    logf = jnp.log2(forget)
    kk = 1.0 - forget
    v = hg_ref[:, 2 * w:3 * w]

    ri = _iota((rows, rows), 0)
    ci = _iota((rows, rows), 1)
    same = (ri // c) == (ci // c)
    tri = _mx(jnp.where(same & (ci <= ri), 1.0, 0.0))
    ones_blk = _mx(jnp.where(same, 1.0, 0.0))
    gl = _dot01_left(tri, logf)
    glast = _dot01_left(ones_blk, logf)
    q_scr[...] = q
    k_scr[...] = gl - jnp.log2(kk)
    g_scr[...] = gl
    qg_scr[...] = q * jnp.exp2(gl)
    kd_scr[...] = kk * jnp.exp2(glast - gl)
    dec = jnp.exp2(glast)
    vT = _mx(v.T)
    head_ones = _mx(_block_ones(w, HEAD_DIM))
    pair_mask = _block_ones(LANE, HEAD_DIM)
    half = c // 2
    rhalf = _iota((half, w), 0)
    nsc = rows // c
    yield

    for sc in range(nsc):
        z_scr[sc, sc * c:(sc + 1) * c, :] = kd_scr[sc * c:(sc + 1) * c, :].astype(z_scr.dtype)
    for p in range(HG_HEADS // 2):
        ps = slice(p * LANE, (p + 1) * LANE)
        st = st_scr[p]
        for sc in range(nsc):
            snap_scr[sc, p] = st.astype(snap_scr.dtype)
            upd = jnp.dot(vT[ps, :], z_scr[sc, :, ps], preferred_element_type=F32)
            st = st * dec[sc * c:sc * c + 1, ps] + upd * pair_mask
        st_scr[p] = st
        yield

    for sc in range(nsc):
        r0 = sc * c
        q_top, q_bot = q_scr[r0:r0 + half, :], q_scr[r0 + half:r0 + c, :]
        g_top, g_bot = g_scr[r0:r0 + half, :], g_scr[r0 + half:r0 + c, :]
        bcast = lambda row: jnp.broadcast_to(row, (half, w))
        xs_top, xs_bot = [], []
        for s in range(c):
            hsr = bcast(k_scr[r0 + s:r0 + s + 1, :])
            if s < half:
                xs_top.append(q_top * jnp.exp2(jnp.where(rhalf >= s, g_top - hsr, -jnp.inf)))
                xs_bot.append(q_bot * jnp.exp2(g_bot - hsr))
            else:
                xs_bot.append(q_bot * jnp.exp2(jnp.where(rhalf >= s - half, g_bot - hsr, -jnp.inf)))
        y = jnp.dot(_mx(jnp.concatenate(xs_top + xs_bot, axis=0)), head_ones, preferred_element_type=F32)
        o_top = jnp.zeros((half, w), F32)
        o_bot = jnp.zeros((half, w), F32)
        for s in range(c):
            vsr = bcast(hg_ref[r0 + s:r0 + s + 1, 2 * w:3 * w])
            if s < half:
                o_top = o_top + y[s * half:(s + 1) * half, :] * vsr
            o_bot = o_bot + y[(half + s) * half:(half + s + 1) * half, :] * vsr
        o_diag = jnp.concatenate([o_top, o_bot], axis=0)
        o_pairs = [_dot_nt(qg_scr[r0:r0 + c, p * LANE:(p + 1) * LANE], snap_scr[sc, p])
                   for p in range(HG_HEADS // 2)]
        o_scr[r0:r0 + c, :] = jnp.concatenate(o_pairs, axis=1) + o_diag
        yield

    o = o_scr[...]
    ms = _dot01_right(o * o, head_ones) * (1.0 / HEAD_DIM)
    y = o * lax.rsqrt(ms + RMS_EPS) * gn_ref[...]
    o_ref[...] = (y * _silu(hg_ref[:, 3 * w:4 * w])).astype(o_ref.dtype)


def _hgrn(hg, lb, gn, batch, seq):
    rows = HG_ROWS
    ns = seq // rows
    n = batch * seq
    nb = HG_SEQS
    assert batch % nb == 0
    vm = lambda: pltpu.VMEM((nb, rows, HG_WIDTH), F32)
    out = pl.pallas_call(
        _hgrn_kernel,
        grid=(batch // nb, ns),
        in_specs=[pl.BlockSpec((nb, rows, 1024), lambda b, s: (b, s, 0)),
                  _const_spec((1, HG_WIDTH)), _const_spec((1, HG_WIDTH))],
        out_specs=pl.BlockSpec((nb, rows, HG_WIDTH), lambda b, s: (b, s, 0)),
        out_shape=jax.ShapeDtypeStruct((batch, seq, HG_WIDTH), MXU_DTYPE),
        scratch_shapes=[pltpu.VMEM((nb, HG_HEADS // 2, LANE, LANE), F32), vm(), vm(), vm(), vm(), vm(), vm(),
                        pltpu.VMEM((nb, rows // HG_SUB, rows, HG_WIDTH), MXU_DTYPE),
                        pltpu.VMEM((nb, rows // HG_SUB, HG_HEADS // 2, LANE, LANE), MXU_DTYPE)],
        compiler_params=_params("parallel", "arbitrary"),
        name="hgrn",
    )(hg.reshape(batch, seq, 1024), lb, gn)
    return out.reshape(n, HG_WIDTH)


def _mlstm_kernel(ml_ref, *refs):
    gT_refs = refs[:ML_SEQS]
    cw_ref, cb_ref, brow_ref, gn_ref, o_ref, prev_scr, c_scr, n_scr, m64_scr, m128_scr = refs[ML_SEQS:]

    @pl.when(pl.program_id(1) == 0)
    def _():
        prev_scr[...] = jnp.zeros(prev_scr.shape, F32)
        c_scr[...] = jnp.zeros(c_scr.shape, F32)
        n_scr[...] = jnp.zeros(n_scr.shape, F32)
        m64_scr[...] = jnp.full(m64_scr.shape, M_INIT, F32)
        m128_scr[...] = jnp.full(m128_scr.shape, M_INIT, F32)

    seqs = [_mlstm_sequence(ml_ref.at[bi], gT_ref, cw_ref, cb_ref, brow_ref, gn_ref, o_ref.at[bi],
                            prev_scr.at[bi], c_scr.at[bi], n_scr.at[bi], m64_scr.at[bi], m128_scr.at[bi])
            for bi, gT_ref in enumerate(gT_refs)]
    while seqs:
        for seq in list(seqs):
            if next(seq, "done") == "done":
                seqs.remove(seq)


def _mlstm_sequence(ml_ref, gT_ref, cw_ref, cb_ref, brow_ref, gn_ref, o_ref,
                    prev_scr, c_scr, n_scr, m64_scr, m128_scr):
    L = ML_CHUNK
    w = ML_WIDTH

    xqk = ml_ref[:, 0:512]
    prev = prev_scr[...]
    rows = _iota((L, 512), 0)
    acc = cb_ref[...] + cw_ref[ML_CONV - 1:ML_CONV, :] * xqk
    for sh in range(1, ML_CONV):
        shifted = jnp.where(rows >= sh, pltpu.roll(xqk, sh, 0), pltpu.roll(prev, sh, 0))
        acc = acc + cw_ref[ML_CONV - 1 - sh:ML_CONV - sh, :] * shifted
    prev_scr[...] = xqk
    qk = _silu(acc)
    q = qk[:, 0:w]
    k = qk[:, w:2 * w] * (HEAD_DIM ** -0.5)
    v = ml_ref[:, 512:768]

    srow = _iota((L, L), 0)
    scol = _iota((L, L), 1)
    tril = srow >= scol
    triu = _mx(jnp.where(srow <= scol, 1.0, 0.0))
    grow = gT_ref[0] + brow_ref[...]
    brow = _dot01_right(_log_sigmoid(grow), triu)
    yield

    def expand(x, width, row0):
        src = _iota((16, ML_HEADS * width), 0)
        dst = _iota((16, ML_HEADS * width), 1)
        e01 = _mx(jnp.where(src == row0 + dst // width, 1.0, 0.0))
        f = lambda p: lax.dot_general(p, e01, (((0,), (0,)), ((), ())), preferred_element_type=F32)
        hi, mid, lo = _split3(x)
        return f(hi) + f(mid) + f(lo)

    b128 = expand(brow, L, 12)
    i128 = expand(grow, L, 8)
    b64 = expand(brow, 64, 12)
    i64 = expand(grow, 64, 8)
    yield

    lane = _iota((L, LANE), 1)
    first = lane < HEAD_DIM
    pair_ones = _mx(_block_ones(LANE, HEAD_DIM))
    pair_mask = _block_ones(LANE, HEAD_DIM)
    m_prev128 = m128_scr[0:1, :]
    m_prev64 = m64_scr[0:1, :]
    n_row = n_scr[0:1, :]

    houts = []
    for p in range(ML_HEADS // 2):
        ps = slice(p * LANE, (p + 1) * LANE)
        q_pair, k_pair, v_pair = q[:, ps], k[:, ps], v[:, ps]
        per_head = []
        for hh in range(2):
            hd = 2 * p + hh
            hs = slice(hd * L, (hd + 1) * L)
            bt = b128[:, hs]
            d = bt - brow[12 + hd:13 + hd, :] + grow[8 + hd:9 + hd, :]
            d = jnp.where(tril, d, -jnp.inf)
            inter = bt + m_prev128[:, hs]
            m_t = jnp.maximum(inter, jnp.max(d, axis=1, keepdims=True))
            wgt = jnp.exp(d - m_t)
            a = jnp.exp(inter - m_t)
            qm = jnp.where(first if hh == 0 else ~first, q_pair, 0.0)
            s = _dot_nt(qm, k_pair) * wgt
            sv = _dot(s, v_pair)
            rs = jnp.sum(s, axis=1, keepdims=True)
            per_head.append((m_t[:, 0:LANE], a[:, 0:LANE], sv, rs))
            yield
        sel = lambda i: jnp.where(first, per_head[0][i], per_head[1][i])
        m_t, a, sv, rs = sel(0), sel(1), sel(2), sel(3)
        qc = _dot(q_pair, c_scr[p])
        qn = _dot(q_pair * n_row[:, ps], pair_ones)
        num = a * qc + sv
        den = a * qn + rs
        houts.append(num / jnp.maximum(jnp.abs(den), jnp.exp(-m_t)))
        yield
    hout = jnp.concatenate(houts, axis=1)

    def new_m(bx, ix, m_prev):
        blast = bx[L - 1:L, :]
        wl = blast - bx + ix
        m_new = jnp.maximum(blast + m_prev, jnp.max(wl, axis=0, keepdims=True))
        return blast, wl, m_new

    blast, wl, m_new64 = new_m(b64, i64, m_prev64)
    dec = jnp.exp(blast + m_prev64 - m_new64)
    kw = k * jnp.exp(wl - m_new64)
    for p in range(ML_HEADS // 2):
        ps = slice(p * LANE, (p + 1) * LANE)
        upd = jnp.dot(_mx(kw[:, ps].T), _mx(v[:, ps]), preferred_element_type=F32)
        c_scr[p] = c_scr[p] * dec[:, ps] + upd * pair_mask
    n_scr[...] = jnp.broadcast_to(dec * n_row + jnp.sum(kw, axis=0, keepdims=True), n_scr.shape)
    m64_scr[...] = jnp.broadcast_to(m_new64, m64_scr.shape)
    _, _, m_new128 = new_m(b128, i128, m_prev128)
    m128_scr[...] = jnp.broadcast_to(m_new128, m128_scr.shape)

    head_ones = _mx(_block_ones(w, HEAD_DIM))
    ms = _dot01_right(hout * hout, head_ones) * (1.0 / HEAD_DIM)
    y = hout * lax.rsqrt(ms + RMS_EPS) * gn_ref[...]
    o_ref[...] = (y * _sigmoid(ml_ref[:, 768:1024])).astype(o_ref.dtype)


def _mlstm(ml, wgT3, cw, cb, brow, gn, batch, seq):
    L = ML_CHUNK
    nc = seq // L
    n = batch * seq
    per_tile = DSA_TILE // L
    nb = ML_SEQS
    assert batch % nb == 0

    def gate_spec(off):
        def index(b, s):
            chunk = (nb * b + off) * nc + s
            return chunk // per_tile, 0, chunk % per_tile
        return pl.BlockSpec((1, 16, L), index)

    out = pl.pallas_call(
        _mlstm_kernel,
        grid=(batch // nb, nc),
        in_specs=[pl.BlockSpec((nb, L, 1024), lambda b, s: (b, s, 0))] + [gate_spec(o) for o in range(nb)]
                 + [_const_spec((ML_CONV, 512)), _const_spec((1, 512)),
                    _const_spec((16, L)), _const_spec((1, ML_WIDTH))],
        out_specs=pl.BlockSpec((nb, L, ML_WIDTH), lambda b, s: (b, s, 0)),
        out_shape=jax.ShapeDtypeStruct((batch, seq, ML_WIDTH), MXU_DTYPE),
        scratch_shapes=[pltpu.VMEM((nb, L, 512), F32), pltpu.VMEM((nb, 2, LANE, LANE), F32),
                        pltpu.VMEM((nb, 8, ML_WIDTH), F32), pltpu.VMEM((nb, 8, ML_WIDTH), F32),
                        pltpu.VMEM((nb, 8, ML_HEADS * L), F32)],
        compiler_params=_params("parallel", "arbitrary"),
        name="mlstm",
    )(ml.reshape(batch, seq, 1024), *([wgT3] * nb), cw, cb, brow, gn)
    return out.reshape(n, ML_WIDTH)


def _mixers_kernel(hg_ref, lb_ref, gnh_ref, ml_ref, *refs, ml_chunks):
    gT_refs = refs[:ML_SEQS]
    cw_ref, cb_ref, brow_ref, gnm_ref, oh_ref, om_ref = refs[ML_SEQS:ML_SEQS + 6]
    hscr = refs[ML_SEQS + 6:ML_SEQS + 15]
    mscr = refs[ML_SEQS + 15:]
    step = pl.program_id(0) * pl.num_programs(1) + pl.program_id(1)

    @pl.when(pl.program_id(1) == 0)
    def _():
        hscr[0][...] = jnp.zeros(hscr[0].shape, F32)
        hscr[7][...] = jnp.zeros(hscr[7].shape, hscr[7].dtype)

    @pl.when(step % ml_chunks == 0)
    def _():
        prev_scr, c_scr, n_scr, m64_scr, m128_scr = mscr
        prev_scr[...] = jnp.zeros(prev_scr.shape, F32)
        c_scr[...] = jnp.zeros(c_scr.shape, F32)
        n_scr[...] = jnp.zeros(n_scr.shape, F32)
        m64_scr[...] = jnp.full(m64_scr.shape, M_INIT, F32)
        m128_scr[...] = jnp.full(m128_scr.shape, M_INIT, F32)

    seqs = [_hgrn_sequence(hg_ref.at[bi], lb_ref, gnh_ref, oh_ref.at[bi], *(s.at[bi] for s in hscr))
            for bi in range(HG_SEQS)]
    seqs += [_mlstm_sequence(ml_ref.at[bi], gT_ref, cw_ref, cb_ref, brow_ref, gnm_ref, om_ref.at[bi],
                             *(s.at[bi] for s in mscr))
             for bi, gT_ref in enumerate(gT_refs)]
    while seqs:
        for seq in list(seqs):
            if next(seq, "done") == "done":
                seqs.remove(seq)


def _mixers(hg, lb, gnh, ml, wgT3, cw, cb, brow, gnm, batch, seq):
    rows, L = HG_ROWS, ML_CHUNK
    ns, nc = seq // rows, seq // L
    n = batch * seq
    assert HG_SEQS * rows == ML_SEQS * L and ns % nc == 0 and batch % HG_SEQS == 0
    per_tile = DSA_TILE // L

    def ml_pos(i, s):
        k = i * ns + s
        return k // nc, k % nc

    def gate_spec(off):
        def index(i, s):
            mb, mc = ml_pos(i, s)
            chunk = (ML_SEQS * mb + off) * nc + mc
            return chunk // per_tile, 0, chunk % per_tile
        return pl.BlockSpec((1, 16, L), index)

    hv = lambda: pltpu.VMEM((HG_SEQS, rows, HG_WIDTH), F32)
    oh, om = pl.pallas_call(
        functools.partial(_mixers_kernel, ml_chunks=nc),
        grid=(batch // HG_SEQS, ns),
        in_specs=[pl.BlockSpec((HG_SEQS, rows, 1024), lambda i, s: (i, s, 0)),
                  _const_spec((1, HG_WIDTH)), _const_spec((1, HG_WIDTH)),
                  pl.BlockSpec((ML_SEQS, L, 1024), lambda i, s: ml_pos(i, s) + (0,))]
                 + [gate_spec(o) for o in range(ML_SEQS)]
                 + [_const_spec((ML_CONV, 512)), _const_spec((1, 512)),
                    _const_spec((16, L)), _const_spec((1, ML_WIDTH))],
        out_specs=[pl.BlockSpec((HG_SEQS, rows, HG_WIDTH), lambda i, s: (i, s, 0)),
                   pl.BlockSpec((ML_SEQS, L, ML_WIDTH), lambda i, s: ml_pos(i, s) + (0,))],
        out_shape=[jax.ShapeDtypeStruct((batch, seq, HG_WIDTH), MXU_DTYPE),
                   jax.ShapeDtypeStruct((batch, seq, ML_WIDTH), MXU_DTYPE)],
        scratch_shapes=[pltpu.VMEM((HG_SEQS, HG_HEADS // 2, LANE, LANE), F32), hv(), hv(), hv(), hv(), hv(), hv(),
                        pltpu.VMEM((HG_SEQS, rows // HG_SUB, rows, HG_WIDTH), MXU_DTYPE),
                        pltpu.VMEM((HG_SEQS, rows // HG_SUB, HG_HEADS // 2, LANE, LANE), MXU_DTYPE),
                        pltpu.VMEM((ML_SEQS, L, 512), F32), pltpu.VMEM((ML_SEQS, 2, LANE, LANE), F32),
                        pltpu.VMEM((ML_SEQS, 8, ML_WIDTH), F32), pltpu.VMEM((ML_SEQS, 8, ML_WIDTH), F32),
                        pltpu.VMEM((ML_SEQS, 8, ML_HEADS * L), F32)],
        compiler_params=_params("parallel", "arbitrary"),
        name="mixers",
    )(hg.reshape(batch, seq, 1024), lb, gnh, ml.reshape(batch, seq, 1024), *([wgT3] * ML_SEQS), cw, cb, brow, gnm)
    return oh.reshape(n, HG_WIDTH), om.reshape(n, ML_WIDTH)


def _rms(y, g):
    ms = jnp.mean(y * y, axis=-1, keepdims=True)
    return y * lax.rsqrt(ms + RMS_EPS) * g


def _mem_kv_kernel(m_ref, g_ref, w_ref, k_ref, v_ref):
    h = _mx(_rms(m_ref[...], g_ref[...]))
    kv = jnp.dot(h, w_ref[...], preferred_element_type=F32)
    k_ref[...] = kv[:, 0:D_MODEL].astype(k_ref.dtype)
    v_ref[...] = kv[:, D_MODEL:2 * D_MODEL].astype(v_ref.dtype)


def _mem_kv(mem2, g, w):
    n = mem2.shape[0]
    tm = ROW_TILE
    row = pl.BlockSpec((tm, D_MODEL), lambda i: (i, 0))
    return pl.pallas_call(
        _mem_kv_kernel,
        grid=(n // tm,),
        in_specs=[row, _const_spec((1, D_MODEL)), _const_spec((D_MODEL, 2 * D_MODEL))],
        out_specs=[row, row],
        out_shape=[jax.ShapeDtypeStruct((n, D_MODEL), MXU_DTYPE)] * 2,
        compiler_params=_params("parallel"),
        name="mem_kv",
    )(mem2, g, w)


def _post_mix_kernel(a_ref, b_ref, c_ref, x_ref, k_ref, v_ref, wout_ref, wq_ref, wo_ref, wu_ref, wd_ref,
                     g_ref, o_ref):
    g = g_ref[...]
    y = (jnp.dot(a_ref[...], wout_ref[0:512, :], preferred_element_type=F32)
         + jnp.dot(b_ref[...], wout_ref[512:768, :], preferred_element_type=F32)
         + jnp.dot(c_ref[...], wout_ref[768:1024, :], preferred_element_type=F32))
    x = x_ref[...] + _rms(y, g[0:1])
    h = _mx(_rms(x, g[1:2]))
    q = jnp.dot(h, wq_ref[...], preferred_element_type=F32) * (CROSS_HEAD_DIM ** -0.5)
    outs = []
    for hd in range(CROSS_HEADS):
        hs = slice(hd * CROSS_HEAD_DIM, (hd + 1) * CROSS_HEAD_DIM)
        logits = _dot_nt(q[:, hs], k_ref[:, hs])
        p = jnp.exp(logits - jnp.max(logits, axis=-1, keepdims=True))
        o = jnp.dot(_mx(p), v_ref[:, hs], preferred_element_type=F32)
        outs.append(o / jnp.sum(p, axis=-1, keepdims=True))
    y = jnp.dot(_mx(jnp.concatenate(outs, axis=1)), wo_ref[...], preferred_element_type=F32)
    x = x + _rms(y, g[2:3])
    h = _mx(_rms(x, g[3:4]))
    y = jnp.zeros(x.shape, F32)
    for c in range(MLP_HIDDEN // D_MODEL):
        cs = slice(c * D_MODEL, (c + 1) * D_MODEL)
        u = jnp.maximum(jnp.dot(h, wu_ref[:, cs], preferred_element_type=F32), 0.0)
        y = y + jnp.dot(_mx(u * u), wd_ref[cs, :], preferred_element_type=F32)
    o_ref[...] = x + _rms(y, g[4:5])


def _post_mix(a, b, c, x2, kmem, vmem, wout, wq, wo, wu, wd, gains, seq, n_mem):
    n = x2.shape[0]
    tm = POST_TILE
    per_b = seq // tm
    row = lambda width: pl.BlockSpec((tm, width), lambda i: (i, 0))
    memb = pl.BlockSpec((n_mem, D_MODEL), lambda i: (i // per_b, 0))
    sq = _const_spec((D_MODEL, D_MODEL))
    return pl.pallas_call(
        _post_mix_kernel,
        grid=(n // tm,),
        in_specs=[row(512), row(256), row(256), row(D_MODEL), memb, memb, sq, sq, sq,
                  _const_spec((D_MODEL, MLP_HIDDEN)), _const_spec((MLP_HIDDEN, D_MODEL)),
                  _const_spec((8, D_MODEL))],
        out_specs=row(D_MODEL),
        out_shape=jax.ShapeDtypeStruct((n, D_MODEL), F32),
        compiler_params=_params("parallel"),
        name="post_mix",
    )(a, b, c, x2, kmem, vmem, wout, wq, wo, wu, wd, gains)


def _rope_tables(seq):
    half = HEAD_DIM // 2
    inv = ROPE_THETA ** (-jnp.arange(0, HEAD_DIM, 2, dtype=F32) / HEAD_DIM)
    ang = jnp.arange(seq).astype(F32)[:, None] * inv[None, :]
    cos, sin = jnp.cos(ang), jnp.sin(ang)
    lane = jnp.arange(LANE)
    freq = (lane % HEAD_DIM) % half
    first = (lane % HEAD_DIM) < half
    cosn = cos[:, freq]
    sinn = sin[:, freq]
    sina = jnp.where(first[None, :], -sinn, 0.0)
    sinb = jnp.where(first[None, :], 0.0, sinn)
    return cosn, sina, sinb, cos.T, sin.T


def _layout_w_in(w_in):
    depth = w_in.shape[0]
    z = lambda wd: jnp.zeros((depth, D_MODEL, wd), w_in.dtype)
    col = lambda a, b: w_in[:, :, a:b]
    wn = jnp.concatenate([col(_O_AK, _O_AV), col(_O_XK, _O_XW), z(64),
                          col(_O_HQ, _O_MQK), col(_O_MQK, _O_MG)], axis=-1)
    wt = jnp.concatenate([col(_O_AQ, _O_AK), col(_O_XQ, _O_XK), col(_O_AV, _O_XQ),
                          col(_O_XW, _O_HQ), col(_O_MG, _IN_WIDTH)], axis=-1)
    return _mx(wn), _mx(jnp.swapaxes(wt, 1, 2))


def kernel(x, mem, mix_pre_g, w_in, ml_conv_w, ml_conv_b, ml_gate_b, hg_lb, hg_norm_g, ml_norm_g,
           w_out, mix_post_g, cross_pre_g, mem_norm_g, w_cq, w_ckv, w_co, cross_post_g, mlp_pre_g,
           w_up, w_down, mlp_post_g):
    batch, seq, _ = x.shape
    n_mem = mem.shape[1]
    depth = w_in.shape[0]
    n = batch * seq
    assert seq % max(ROW_TILE, POST_TILE, HG_ROWS, ML_CHUNK, DSA_TILE) == 0
    assert (batch * n_mem) % ROW_TILE == 0
    assert ROW_TILE % DSA_TILE == 0 and DSA_TILE % ML_CHUNK == 0

    lbs = jnp.cumsum(jax.nn.softmax(hg_lb.astype(F32), axis=0), axis=0)
    lbs = lbs - lbs[0:1]

    tabs = _rope_tables(seq)
    wn_all, wt_all = _layout_w_in(w_in)
    row = lambda a: a.astype(F32)[None, :]
    brow = jnp.pad(ml_gate_b.astype(F32), ((0, 0), (8, 0)))

    x2 = x.reshape(n, D_MODEL)
    mem2 = mem.reshape(batch * n_mem, D_MODEL)
    nl = n // DSA_TILE
    for l in range(depth):
        knat, hg, ml, qT3, qiT3, vT3, wgT3 = _in_proj(x2, row(mix_pre_g[l]), wn_all[l], wt_all[l], tabs, seq)
        a_out = _dsa(qiT3, wgT3, knat.reshape(nl, DSA_TILE, 256), qT3, vT3, batch, seq)
        b_out, c_out = _mixers(hg, row(lbs[l]), row(jnp.tile(hg_norm_g[l], HG_HEADS)),
                               ml, wgT3, ml_conv_w[l].astype(F32), row(ml_conv_b[l]),
                               jnp.broadcast_to(brow[l][:, None], (16, ML_CHUNK)),
                               row(jnp.tile(ml_norm_g[l], ML_HEADS)), batch, seq)
        kmem, vmem = _mem_kv(mem2, row(mem_norm_g[l]), _mx(w_ckv[l]))
        gains = jnp.stack([mix_post_g[l], cross_pre_g[l], cross_post_g[l], mlp_pre_g[l], mlp_post_g[l]]
                          + [jnp.zeros_like(mix_post_g[l])] * 3).astype(F32)
        x2 = _post_mix(a_out, b_out, c_out, x2, kmem, vmem, _mx(w_out[l]), _mx(w_cq[l]), _mx(w_co[l]),
                       _mx(w_up[l]), _mx(w_down[l]), gains, seq, n_mem)
    return x2.reshape(batch, seq, D_MODEL)
```
